```python
import math
import jax, jax.numpy as jnp
from jax import lax
import numpy as np

D_MODEL = 1024
BATCH = 2
SEQ = 8192
DEPTH = 2

GRID_W = 64
CTX_LEN = 256
N_EVEN = (DEPTH + 1) // 2
N_ODD = DEPTH // 2
ALPHA = (2 * DEPTH) ** 0.25
BETA = (8 * DEPTH) ** -0.25
ROPE_BASE = 10000.0
Q_BLOCK = 128
LN_EPS = 1e-5
RMS_EPS = 1e-6
N_MOD = 6

HEAD_DIM = 64
DIFF_HEADS = 4
DIFF_V = 2 * HEAD_DIM
NA_HEADS = 8
NA_ROWS = 8
NA_COLS = 16
MLA_HEADS = 8
MLA_Q_RANK = 384
MLA_KV_RANK = 256
MLA_NOPE = 64
MLA_ROPE = 32
MLA_V = 64
GMLP_GROUPS = 4
GMLP_GROUP_CH = 128
GMLP_CHUNK = 128
GMLP_WIDTH = GMLP_GROUPS * GMLP_GROUP_CH
FFN_HIDDEN = -(-8 * D_MODEL // (3 * 256)) * 256

EVEN_SPLITS = [DIFF_HEADS * 2 * HEAD_DIM, DIFF_HEADS * 2 * HEAD_DIM, DIFF_HEADS * DIFF_V,
               NA_HEADS * HEAD_DIM, NA_HEADS * HEAD_DIM, NA_HEADS * HEAD_DIM]
EVEN_IN = sum(EVEN_SPLITS)
EVEN_MIX = DIFF_HEADS * DIFF_V + NA_HEADS * HEAD_DIM
ODD_SPLITS = [MLA_Q_RANK, MLA_KV_RANK, MLA_ROPE, GMLP_WIDTH, GMLP_WIDTH]
ODD_IN = sum(ODD_SPLITS)
ODD_MIX = MLA_HEADS * MLA_V + GMLP_WIDTH

kernel_name = "hybrid_diffnat_mla_gmlp_prefix_dit"


def layer_norm(x, g, b):
    xf = x.astype(jnp.float32)
    mu = jnp.mean(xf, -1, keepdims=True)
    var = jnp.mean(jnp.square(xf - mu), -1, keepdims=True)
    return ((xf - mu) * lax.rsqrt(var + LN_EPS) * g.astype(jnp.float32) + b.astype(jnp.float32)).astype(x.dtype)


def rms_norm(x, g):
    xf = x.astype(jnp.float32)
    ms = jnp.mean(jnp.square(xf), -1, keepdims=True)
    return (xf * lax.rsqrt(ms + RMS_EPS) * g.astype(jnp.float32)).astype(x.dtype)


def rope_1d(x, pos):
    d = x.shape[-1]
    inv = jnp.power(ROPE_BASE, -jnp.arange(0, d, 2, dtype=jnp.float32) / d)
    ang = pos.astype(jnp.float32)[:, None] * inv[None, :]
    cos, sin = jnp.cos(ang).astype(x.dtype), jnp.sin(ang).astype(x.dtype)
    x1, x2 = x[..., : d // 2], x[..., d // 2:]
    return jnp.concatenate([x1 * cos - x2 * sin, x2 * cos + x1 * sin], -1)


def rope_2d(x, row, col):
    h = x.shape[-1] // 2
    return jnp.concatenate([rope_1d(x[..., :h], row), rope_1d(x[..., h:], col)], -1)


def split_cols(p, sizes):
    return jnp.split(p, np.cumsum(sizes)[:-1].tolist(), axis=-1)


def heads(t, n):
    b, s, _ = t.shape
    return t.reshape(b, s, n, -1).transpose(0, 2, 1, 3)


def diff_heads(t):
    b, s, _ = t.shape
    return t.reshape(b, s, DIFF_HEADS, 2, HEAD_DIM).transpose(0, 2, 3, 1, 4)


def merge(o):
    b, h, s, d = o.shape
    return o.transpose(0, 2, 1, 3).reshape(b, s, h * d)


def softmax_attention(q, k, v, scale):
    b, h, t, dq = q.shape
    nb = t // Q_BLOCK
    qb = q.reshape(b, h, nb, Q_BLOCK, dq).transpose(2, 0, 1, 3, 4)

    def block(qi):
        s = jnp.einsum('bhqd,bhkd->bhqk', qi, k).astype(jnp.float32) * scale
        p = jax.nn.softmax(s, axis=-1).astype(v.dtype)
        return jnp.einsum('bhqk,bhkd->bhqd', p, v)

    o = lax.map(block, qb)
    return o.transpose(1, 2, 0, 3, 4).reshape(b, h, t, v.shape[-1])


def diff_attention(q, k, v, lam):
    b, h, _, t, dh = q.shape
    nb = t // Q_BLOCK
    scale = dh ** -0.5
    qb = q.reshape(b, h, 2, nb, Q_BLOCK, dh).transpose(3, 0, 1, 2, 4, 5)

    def block(qi):
        s = jnp.einsum('bhmqd,bhmkd->bhmqk', qi, k).astype(jnp.float32) * scale
        p = jax.nn.softmax(s, axis=-1)
        a = (p[:, :, 0] - lam * p[:, :, 1]).astype(v.dtype)
        return jnp.einsum('bhqk,bhkd->bhqd', a, v)

    o = lax.map(block, qb)
    return o.transpose(1, 2, 0, 3, 4).reshape(b, h, t, v.shape[-1])


def neighbourhood_attention(q, k, v, k_ctx, v_ctx, rpb):
    b, h, s, dh = q.shape
    rows = s // GRID_W
    wr = min(NA_ROWS, rows)
    wc = NA_COLS
    nw = wr * wc
    scale = dh ** -0.5
    kg = k.reshape(b, h, rows, GRID_W, dh)
    vg = v.reshape(b, h, rows, GRID_W, dh)
    r = jnp.arange(rows)
    cidx = jnp.arange(GRID_W)
    row_start = jnp.clip(r - wr // 2, 0, rows - wr)
    col_start = jnp.clip(cidx - wc // 2, 0, GRID_W - wc)
    col_keys = col_start[:, None] + jnp.arange(wc)[None, :]
    row_off = row_start[:, None] + jnp.arange(wr)[None, :] - r[:, None] + (NA_ROWS - 1)
    col_off = col_keys - cidx[:, None] + (NA_COLS - 1)
    qr = q.reshape(b, h, rows, GRID_W, dh).transpose(2, 0, 1, 3, 4)

    def row_block(args):
        q_row, start, roff = args
        k_rows = lax.dynamic_slice_in_dim(kg, start, wr, axis=2)
        v_rows = lax.dynamic_slice_in_dim(vg, start, wr, axis=2)
        k_win = k_rows[:, :, :, col_keys].transpose(0, 1, 3, 2, 4, 5).reshape(b, h, GRID_W, nw, dh)
        v_win = v_rows[:, :, :, col_keys].transpose(0, 1, 3, 2, 4, 5).reshape(b, h, GRID_W, nw, dh)
        bias = rpb[:, roff[:, None, None], col_off[None, :, :]]
        bias = bias.transpose(0, 2, 1, 3).reshape(h, GRID_W, nw).astype(jnp.float32)
        s_win = jnp.einsum('bhqd,bhqkd->bhqk', q_row, k_win).astype(jnp.float32) * scale + bias
        s_ctx = jnp.einsum('bhqd,bhkd->bhqk', q_row, k_ctx).astype(jnp.float32) * scale
        p = jax.nn.softmax(jnp.concatenate([s_win, s_ctx], -1), axis=-1).astype(v.dtype)
        return (jnp.einsum('bhqk,bhqkd->bhqd', p[..., :nw], v_win)
                + jnp.einsum('bhqk,bhkd->bhqd', p[..., nw:], v_ctx))

    o = lax.map(row_block, (qr, row_start, row_off))
    return o.transpose(1, 2, 0, 3, 4).reshape(b, h, s, dh)


def even_mixer(p_x, p_c, w_out, diff_lam, diff_subln_g, na_rpb, layer_idx, need_ctx):
    s = p_x.shape[1]
    t = jnp.arange(s)
    row, col = t // GRID_W, t % GRID_W
    aq_x, ak_x, av_x, bq_x, bk_x, bv_x = split_cols(p_x, EVEN_SPLITS)
    aq_c, ak_c, av_c, bq_c, bk_c, bv_c = split_cols(p_c, EVEN_SPLITS)
    lambda_init = 0.8 - 0.6 * math.exp(-0.3 * layer_idx)
    lf = diff_lam.astype(jnp.float32)
    lam = jnp.exp(jnp.sum(lf[0] * lf[1])) - jnp.exp(jnp.sum(lf[2] * lf[3])) + lambda_init

    def diff_out(o):
        return merge(rms_norm(o, diff_subln_g) * (1.0 - lambda_init))

    ka_c, va_c = diff_heads(ak_c), heads(av_c, DIFF_HEADS)
    kb_c, vb_c = heads(bk_c, NA_HEADS), heads(bv_c, NA_HEADS)
    qa_x = rope_2d(diff_heads(aq_x), row, col)
    ka_x = rope_2d(diff_heads(ak_x), row, col)
    oa_x = diff_attention(qa_x, jnp.concatenate([ka_x, ka_c], axis=3),
                          jnp.concatenate([heads(av_x, DIFF_HEADS), va_c], axis=2), lam)
    ob_x = neighbourhood_attention(heads(bq_x, NA_HEADS), heads(bk_x, NA_HEADS), heads(bv_x, NA_HEADS),
                                   kb_c, vb_c, na_rpb)
    y_x = jnp.concatenate([diff_out(oa_x), merge(ob_x)], -1) @ w_out
    if not need_ctx:
        return y_x, None
    oa_c = diff_attention(diff_heads(aq_c), ka_c, va_c, lam)
    ob_c = softmax_attention(heads(bq_c, NA_HEADS), kb_c, vb_c, HEAD_DIM ** -0.5)
    y_c = jnp.concatenate([diff_out(oa_c), merge(ob_c)], -1) @ w_out
    return y_x, y_c


def odd_mixer(p_x, p_c, w_out, mla_q_norm_g, mla_w_uq, mla_kv_norm_g, mla_w_ukv,
              gmlp_ln_g, gmlp_ln_b, gmlp_ws, gmlp_b, need_ctx):
    s = p_x.shape[1]
    t = jnp.arange(s)
    pos = (t // GRID_W, t % GRID_W)
    cq_x, ckv_x, kr_x, gu_x, gv_x = split_cols(p_x, ODD_SPLITS)
    cq_c, ckv_c, kr_c, gu_c, gv_c = split_cols(p_c, ODD_SPLITS)
    scale = (MLA_NOPE + MLA_ROPE) ** -0.5

    def mla_q(cq, rpos):
        q = heads(rms_norm(cq, mla_q_norm_g) @ mla_w_uq, MLA_HEADS)
        q_nope, q_rope = q[..., :MLA_NOPE], q[..., MLA_NOPE:]
        if rpos is not None:
            q_rope = rope_2d(q_rope, *rpos)
        return jnp.concatenate([q_nope, q_rope], -1)

    def mla_kv(ckv, kr, rpos):
        kv = heads(rms_norm(ckv, mla_kv_norm_g) @ mla_w_ukv, MLA_HEADS)
        k_nope, v = kv[..., :MLA_NOPE], kv[..., MLA_NOPE:]
        k_rope = kr[:, None]
        if rpos is not None:
            k_rope = rope_2d(k_rope, *rpos)
        k = jnp.concatenate([k_nope, jnp.broadcast_to(k_rope, k_nope.shape[:-1] + (MLA_ROPE,))], -1)
        return k, v

    def chunk_gmlp(gu, gv):
        b, tl, _ = gu.shape
        u = jax.nn.gelu(gu, approximate=False)
        v = layer_norm(jax.nn.gelu(gv, approximate=False), gmlp_ln_g, gmlp_ln_b)
        v = v.reshape(b, tl // GMLP_CHUNK, GMLP_CHUNK, GMLP_GROUPS, GMLP_GROUP_CH)
        mixed = jnp.einsum('gij,bnjgc->bnigc', gmlp_ws, v) + gmlp_b.T[:, :, None]
        return u * mixed.reshape(b, tl, GMLP_WIDTH)

    k_c, v_c = mla_kv(ckv_c, kr_c, None)
    k_x, v_x = mla_kv(ckv_x, kr_x, pos)
    oc_x = softmax_attention(mla_q(cq_x, pos), jnp.concatenate([k_x, k_c], axis=2),
                             jnp.concatenate([v_x, v_c], axis=2), scale)
    y_x = jnp.concatenate([merge(oc_x), chunk_gmlp(gu_x, gv_x)], -1) @ w_out
    if not need_ctx:
        return y_x, None
    oc_c = softmax_attention(mla_q(cq_c, None), k_c, v_c, scale)
    y_c = jnp.concatenate([merge(oc_c), chunk_gmlp(gu_c, gv_c)], -1) @ w_out
    return y_x, y_c


def modulation(cond, w, b):
    return (jax.nn.silu(cond) @ w + b).reshape(cond.shape[:-1] + (N_MOD, cond.shape[-1]))


def modulate(h, m, k):
    return h * (1.0 + m[..., k + 1, :]) + m[..., k, :]


def swiglu(u, w_in, w_out):
    g, a = jnp.split(u @ w_in, 2, axis=-1)
    return (jax.nn.silu(g) * a) @ w_out


def setup_inputs(seed: int = 0) -> dict:
    key = jax.random.key(seed)
    ks = iter(jax.random.split(key, 32))

    def nrm(shape, s):
        return s * jax.random.normal(next(ks), shape, jnp.float32)

    D = D_MODEL
    return {
        "x": nrm((BATCH, SEQ, D), 1.0),
        "c": nrm((BATCH, D), 1.0),
        "ctx": nrm((BATCH, CTX_LEN, D), 1.0),
        "c_ctx": nrm((D,), 1.0),
        "mod_w": nrm((DEPTH, D, N_MOD * D), D ** -0.5),
        "mod_b": nrm((DEPTH, N_MOD * D), 0.01),
        "ln_mix_g": 1.0 + nrm((DEPTH, D), 0.02),
        "ln_mix_b": nrm((DEPTH, D), 0.02),
        "ln_ffn_g": 1.0 + nrm((DEPTH, D), 0.02),
        "ln_ffn_b": nrm((DEPTH, D), 0.02),
        "ffn_w_in": nrm((DEPTH, D, 2 * FFN_HIDDEN), D ** -0.5),
        "ffn_w_out": nrm((DEPTH, FFN_HIDDEN, D), BETA * FFN_HIDDEN ** -0.5),
        "ev_w_in": nrm((N_EVEN, D, EVEN_IN), D ** -0.5),
        "ev_w_out": nrm((N_EVEN, EVEN_MIX, D), BETA * EVEN_MIX ** -0.5),
        "diff_lambda": nrm((N_EVEN, 4, HEAD_DIM), 0.1),
        "diff_subln_g": 1.0 + nrm((N_EVEN, DIFF_V), 0.02),
        "na_rpb": nrm((N_EVEN, NA_HEADS, 2 * NA_ROWS - 1, 2 * NA_COLS - 1), 0.05),
        "od_w_in": nrm((N_ODD, D, ODD_IN), D ** -0.5),
        "od_w_out": nrm((N_ODD, ODD_MIX, D), BETA * ODD_MIX ** -0.5),
        "mla_q_norm_g": 1.0 + nrm((N_ODD, MLA_Q_RANK), 0.02),
        "mla_w_uq": nrm((N_ODD, MLA_Q_RANK, MLA_HEADS * (MLA_NOPE + MLA_ROPE)), MLA_Q_RANK ** -0.5),
        "mla_kv_norm_g": 1.0 + nrm((N_ODD, MLA_KV_RANK), 0.02),
        "mla_w_ukv": nrm((N_ODD, MLA_KV_RANK, MLA_HEADS * (MLA_NOPE + MLA_V)), MLA_KV_RANK ** -0.5),
        "gmlp_ln_g": 1.0 + nrm((N_ODD, GMLP_WIDTH), 0.02),
        "gmlp_ln_b": nrm((N_ODD, GMLP_WIDTH), 0.02),
        "gmlp_ws": nrm((N_ODD, GMLP_GROUPS, GMLP_CHUNK, GMLP_CHUNK), GMLP_CHUNK ** -0.5),
        "gmlp_b": 1.0 + nrm((N_ODD, GMLP_GROUPS, GMLP_CHUNK), 0.02),
    }


def reference(x, c, ctx, c_ctx, mod_w, mod_b, ln_mix_g, ln_mix_b, ln_ffn_g, ln_ffn_b,
              ffn_w_in, ffn_w_out, ev_w_in, ev_w_out, diff_lambda, diff_subln_g, na_rpb,
              od_w_in, od_w_out, mla_q_norm_g, mla_w_uq, mla_kv_norm_g, mla_w_ukv,
              gmlp_ln_g, gmlp_ln_b, gmlp_ws, gmlp_b):
    h_x, h_c = x, ctx
    for i in range(DEPTH):
        last = i == DEPTH - 1
        j = i // 2
        m_x = modulation(c, mod_w[i], mod_b[i])[:, None]
        m_c = modulation(c_ctx, mod_w[i], mod_b[i])[None, None]
        u_x, u_c = modulate(h_x, m_x, 0), modulate(h_c, m_c, 0)
        if i % 2 == 0:
            y_x, y_c = even_mixer(u_x @ ev_w_in[j], u_c @ ev_w_in[j], ev_w_out[j], diff_lambda[j],
                                  diff_subln_g[j], na_rpb[j], i, not last)
        else:
            y_x, y_c = odd_mixer(u_x @ od_w_in[j], u_c @ od_w_in[j], od_w_out[j], mla_q_norm_g[j],
                                 mla_w_uq[j], mla_kv_norm_g[j], mla_w_ukv[j], gmlp_ln_g[j],
                                 gmlp_ln_b[j], gmlp_ws[j], gmlp_b[j], not last)
        h_x = layer_norm(ALPHA * h_x + m_x[..., 2, :] * y_x, ln_mix_g[i], ln_mix_b[i])
        y_x = swiglu(modulate(h_x, m_x, 3), ffn_w_in[i], ffn_w_out[i])
        h_x = layer_norm(ALPHA * h_x + m_x[..., 5, :] * y_x, ln_ffn_g[i], ln_ffn_b[i])
        if not last:
            h_c = layer_norm(ALPHA * h_c + m_c[..., 2, :] * y_c, ln_mix_g[i], ln_mix_b[i])
            y_c = swiglu(modulate(h_c, m_c, 3), ffn_w_in[i], ffn_w_out[i])
            h_c = layer_norm(ALPHA * h_c + m_c[..., 5, :] * y_c, ln_ffn_g[i], ln_ffn_b[i])
    return h_x
```

```python
import functools
import math

import jax
import jax.numpy as jnp
import numpy as np
from jax import lax
from jax.experimental import pallas as pl
from jax.experimental.pallas import tpu as pltpu

F32 = jnp.float32
BF16 = jnp.bfloat16

D = 1024
B = 2
S = 8192
L = 256
N = S + L
DEPTH = 2
GRID_W = 64
ROWS = S // GRID_W
ALPHA = (2 * DEPTH) ** 0.25
ROPE_BASE = 10000.0
LN_EPS = 1e-5
RMS_EPS = 1e-6
N_MOD = 6
HEAD_DIM = 64
DIFF_HEADS = 4
DIFF_V = 128
NA_HEADS = 8
NA_ROWS = 8
NA_COLS = 16
MLA_HEADS = 8
MLA_Q_RANK = 384
MLA_KV_RANK = 256
MLA_NOPE = 64
MLA_ROPE = 32
MLA_V = 64
GMLP_GROUPS = 4
GMLP_CH = 128
GMLP_CHUNK = 128
GMLP_WIDTH = 512
FFN_HIDDEN = 2816
LAMBDA_INIT_0 = 0.8 - 0.6 * math.exp(-0.3 * 0)

LANE = 128
TM = 256
NT_ALL = N // TM
NT_LAT = S // TM
TQ = 256
TK = 512
NA_WIN_ROWS = 10
NA_WIN = NA_WIN_ROWS * GRID_W
NA_PATTERNS = 5
NEG = -1e30
VMEM_LIMIT = 56 * 1024 * 1024


def _cparams(n_axes):
    return pltpu.CompilerParams(dimension_semantics=("parallel",) * n_axes,
                                vmem_limit_bytes=VMEM_LIMIT)


def _mod_row(b, i):
    return jnp.where(i < NT_LAT, b, 2)


def _mod_kernel(cond_ref, w_ref, b_ref, o_ref):
    cnd = cond_ref[...]
    act = cnd * (1.0 / (1.0 + jnp.exp(-cnd)))
    o_ref[0] = jnp.dot(act, w_ref[0], preferred_element_type=F32) + b_ref[0]


def _modulation(cond, mod_w, mod_b):
    tn = 1536
    return pl.pallas_call(
        _mod_kernel,
        out_shape=jax.ShapeDtypeStruct((DEPTH, 8, N_MOD * D), F32),
        grid=(DEPTH, N_MOD * D // tn),
        in_specs=[
            pl.BlockSpec((8, D), lambda l, j: (0, 0)),
            pl.BlockSpec((1, D, tn), lambda l, j: (l, 0, j)),
            pl.BlockSpec((1, 1, tn), lambda l, j: (l, 0, j)),
        ],
        out_specs=pl.BlockSpec((1, 8, tn), lambda l, j: (l, 0, j)),
        compiler_params=_cparams(2),
        name="modulation",
    )(cond, mod_w, mod_b.reshape(DEPTH, 1, N_MOD * D))


def _modulated(h_ref, mv_ref, k):
    mv = mv_ref[0]
    return (h_ref[0] * (1.0 + mv[k + 1:k + 2]) + mv[k:k + 1]).astype(BF16)


def _layer_norm(x, g, b):
    mu = jnp.mean(x, axis=-1, keepdims=True)
    xc = x - mu
    var = jnp.mean(xc * xc, axis=-1, keepdims=True)
    return xc * lax.rsqrt(var + LN_EPS) * g + b


def _inproj_even_kernel(h_ref, mv_ref, wtok_ref, wfeat_ref, cosf_ref, sinf_ref,
                        cost_ref, sina_ref, sinb_ref, tok_ref, feat_ref):
    u = _modulated(h_ref, mv_ref, 0)
    tok = jnp.dot(u, wtok_ref[...], preferred_element_type=F32)
    ct, sa, sb = cost_ref[...], sina_ref[...], sinb_ref[...]
    for j in range(4):
        xs = tok[:, j * LANE:(j + 1) * LANE]
        ro = xs * ct + pltpu.roll(xs, LANE - 16, 1) * sa + pltpu.roll(xs, 16, 1) * sb
        tok_ref[0, :, j * LANE:(j + 1) * LANE] = ro.astype(BF16)
    tok_ref[0, :, 512:] = tok[:, 512:].astype(BF16)
    feat = lax.dot_general(wfeat_ref[...], u, (((1,), (1,)), ((), ())),
                           preferred_element_type=F32)
    cf, sf = cosf_ref[...], sinf_ref[...]
    for blk in range(8):
        for half in range(2):
            base = blk * 64 + half * 32
            x1 = feat[base:base + 16]
            x2 = feat[base + 16:base + 32]
            c = cf[half * 16:(half + 1) * 16]
            s = sf[half * 16:(half + 1) * 16]
            feat_ref[0, base:base + 16, :] = (x1 * c - x2 * s).astype(BF16)
            feat_ref[0, base + 16:base + 32, :] = (x2 * c + x1 * s).astype(BF16)
    feat_ref[0, 512:, :] = feat[512:].astype(BF16)


def _inproj_even(h, modv, wtok, wfeat, tabs):
    cosf, sinf, cost, sina, sinb = tabs
    const = lambda b, i: (0, 0)
    return pl.pallas_call(
        _inproj_even_kernel,
        out_shape=(jax.ShapeDtypeStruct((B, N, 1024), BF16),
                   jax.ShapeDtypeStruct((B, 2048, N), BF16)),
        grid=(B, NT_ALL),
        in_specs=[
            pl.BlockSpec((1, TM, D), lambda b, i: (b, i, 0)),
            pl.BlockSpec((1, N_MOD, D), lambda b, i: (_mod_row(b, i), 0, 0)),
            pl.BlockSpec((D, 1024), const),
            pl.BlockSpec((2048, D), const),
            pl.BlockSpec((32, TM), lambda b, i: (0, i)),
            pl.BlockSpec((32, TM), lambda b, i: (0, i)),
            pl.BlockSpec((TM, LANE), lambda b, i: (i, 0)),
            pl.BlockSpec((TM, LANE), lambda b, i: (i, 0)),
            pl.BlockSpec((TM, LANE), lambda b, i: (i, 0)),
        ],
        out_specs=(pl.BlockSpec((1, TM, 1024), lambda b, i: (b, i, 0)),
                   pl.BlockSpec((1, 2048, TM), lambda b, i: (b, 0, i))),
        compiler_params=_cparams(2),
        name="inproj_even",
    )(h, modv, wtok, wfeat, cosf, sinf, cost, sina, sinb)


def _block_diag2(q_t):
    n = q_t.shape[1]
    z = jnp.zeros((64, n), q_t.dtype)
    left = jnp.concatenate([q_t[:64], z], axis=0)
    right = jnp.concatenate([z, q_t[64:]], axis=0)
    return jnp.concatenate([left, right], axis=1)


def _online_chunk(k_c, vt_c, rhs, m, l, acc_ref):
    s = jnp.dot(k_c, rhs, preferred_element_type=F32)
    m_new = jnp.maximum(m, jnp.max(s, axis=0, keepdims=True))
    alpha = jnp.exp(m - m_new)
    p = jnp.exp(s - m_new)
    l_new = alpha * l + jnp.sum(p, axis=0, keepdims=True)
    acc_ref[...] = acc_ref[...] * alpha + jnp.dot(vt_c, p.astype(BF16), preferred_element_type=F32)
    return m_new, l_new


def _flash(rhs, k_ref, vt_ref, acc_ref, is_latent):
    width = rhs.shape[1]
    acc_ref[...] = jnp.zeros_like(acc_ref)
    m0 = jnp.full((1, width), -jnp.inf, F32)
    l0 = jnp.zeros((1, width), F32)

    def body(j, carry):
        off = pl.multiple_of(j * TK, TK)
        return _online_chunk(k_ref[0, pl.ds(off, TK), :], vt_ref[0, :, pl.ds(off, TK)],
                             rhs, carry[0], carry[1], acc_ref)

    n_lat = jnp.where(is_latent, S // TK, 0)
    m, l = lax.fori_loop(0, n_lat, body, (m0, l0))
    m, l = _online_chunk(k_ref[0, S:, :], vt_ref[0, :, S:], rhs, m, l, acc_ref)
    return l


def _diff_attn_kernel(lam_ref, g_ref, q_ref, k_ref, vt_ref, o_ref, acc_ref):
    i = pl.program_id(2)
    rhs = _block_diag2(q_ref[0])
    l = _flash(rhs, k_ref, vt_ref, acc_ref, i < S // TQ)
    lf = lam_ref[...]
    lam = (jnp.exp(jnp.sum(lf[0:1] * lf[1:2], axis=1, keepdims=True))
           - jnp.exp(jnp.sum(lf[2:3] * lf[3:4], axis=1, keepdims=True)) + LAMBDA_INIT_0)
    acc = acc_ref[...]
    o = acc[:, :TQ] / l[:, :TQ] - lam * (acc[:, TQ:] / l[:, TQ:])
    ms = jnp.mean(o * o, axis=0, keepdims=True)
    o = o * lax.rsqrt(ms + RMS_EPS) * g_ref[...] * (1.0 - LAMBDA_INIT_0)
    o_ref[0] = o.astype(BF16)


def _diff_attention(tok, feat, diff_lambda, subln_g):
    return pl.pallas_call(
        _diff_attn_kernel,
        out_shape=jax.ShapeDtypeStruct((B, 512, N), BF16),
        grid=(B, DIFF_HEADS, N // TQ),
        in_specs=[
            pl.BlockSpec((4, HEAD_DIM), lambda b, h, i: (0, 0)),
            pl.BlockSpec((DIFF_V, 1), lambda b, h, i: (0, 0)),
            pl.BlockSpec((1, 128, TQ), lambda b, h, i: (b, h, i)),
            pl.BlockSpec((1, N, 128), lambda b, h, i: (b, 0, h)),
            pl.BlockSpec((1, 128, N), lambda b, h, i: (b, 4 + h, 0)),
        ],
        out_specs=pl.BlockSpec((1, 128, TQ), lambda b, h, i: (b, h, i)),
        scratch_shapes=[pltpu.VMEM((DIFF_V, 2 * TQ), F32)],
        compiler_params=_cparams(3),
        name="diff_attention",
    )(diff_lambda, subln_g.reshape(DIFF_V, 1), feat, tok, feat)


def _na_kernel(q_ref, k_ref, vt_ref, bias_ref, o_ref):
    i = pl.program_id(2)
    k_ctx = k_ref[0, S:, :]
    vt_ctx = vt_ref[0, :, S:]

    def attend(rhs, k_win, vt_win, bias):
        s_c = jnp.dot(k_ctx, rhs, preferred_element_type=F32)
        m = jnp.max(s_c, axis=0, keepdims=True)
        if k_win is not None:
            s_w = jnp.dot(k_win, rhs, preferred_element_type=F32) + bias
            m = jnp.maximum(m, jnp.max(s_w, axis=0, keepdims=True))
        p_c = jnp.exp(s_c - m)
        l = jnp.sum(p_c, axis=0, keepdims=True)
        o = jnp.dot(vt_ctx, p_c.astype(BF16), preferred_element_type=F32)
        if k_win is not None:
            p_w = jnp.exp(s_w - m)
            l = l + jnp.sum(p_w, axis=0, keepdims=True)
            o = o + jnp.dot(vt_win, p_w.astype(BF16), preferred_element_type=F32)
        o = o / l
        n = rhs.shape[1] // 2
        return jnp.concatenate([o[:64, :n], o[64:, n:]], axis=0).astype(BF16)

    @pl.when(i < S // TQ)
    def _latent():
        for jj in range(TQ // LANE):
            r = 2 * (i * (TQ // LANE) + jj)
            start = jnp.clip(r - NA_ROWS // 2, 0, ROWS - NA_WIN_ROWS)
            pat = jnp.where(r == 0, 0, jnp.where(r == 2, 1, jnp.where(
                r == ROWS - 4, 3, jnp.where(r == ROWS - 2, 4, 2))))
            off = pl.multiple_of(start * GRID_W, LANE)
            rhs = _block_diag2(q_ref[0, :, jj * LANE:(jj + 1) * LANE])
            o_ref[0, :, jj * LANE:(jj + 1) * LANE] = attend(
                rhs, k_ref[0, pl.ds(off, NA_WIN), :], vt_ref[0, :, pl.ds(off, NA_WIN)],
                bias_ref[0, pat])

    @pl.when(i == S // TQ)
    def _context():
        o_ref[0] = attend(_block_diag2(q_ref[0]), None, None, None)


def _na_attention(tok, feat, bias):
    return pl.pallas_call(
        _na_kernel,
        out_shape=jax.ShapeDtypeStruct((B, 512, N), BF16),
        grid=(B, NA_HEADS // 2, N // TQ),
        in_specs=[
            pl.BlockSpec((1, 128, TQ), lambda b, h, i: (b, 8 + h, i)),
            pl.BlockSpec((1, N, 128), lambda b, h, i: (b, 0, 4 + h)),
            pl.BlockSpec((1, 128, N), lambda b, h, i: (b, 12 + h, 0)),
            pl.BlockSpec((1, NA_PATTERNS, NA_WIN, 2 * LANE), lambda b, h, i: (h, 0, 0, 0)),
        ],
        out_specs=pl.BlockSpec((1, 128, TQ), lambda b, h, i: (b, h, i)),
        compiler_params=_cparams(3),
        name="na_attention",
    )(feat, tok, feat, bias)


def _na_bias_table(rpb):
    pats = [(0, 0), (2, 0), (8, 4), (ROWS - 4, ROWS - NA_WIN_ROWS), (ROWS - 2, ROWS - NA_WIN_ROWS)]
    key = np.arange(NA_WIN)
    kr_rel, kc = key // GRID_W, key % GRID_W
    lane = np.arange(LANE)
    j, c = lane // GRID_W, lane % GRID_W
    cs = np.clip(c - NA_COLS // 2, 0, GRID_W - NA_COLS)
    col_ok = (kc[:, None] >= cs[None, :]) & (kc[:, None] < cs[None, :] + NA_COLS)
    dcol = np.clip(kc[:, None] - c[None, :] + NA_COLS - 1, 0, 2 * NA_COLS - 2)
    tables = []
    for r, start in pats:
        rq = r + j
        rs = np.clip(rq - NA_ROWS // 2, 0, ROWS - NA_ROWS)
        kr = start + kr_rel
        row_ok = (kr[:, None] >= rs[None, :]) & (kr[:, None] < rs[None, :] + NA_ROWS)
        drow = np.clip(kr[:, None] - rq[None, :] + NA_ROWS - 1, 0, 2 * NA_ROWS - 2)
        vals = rpb[:, drow, dcol]
        tables.append(jnp.where(jnp.asarray(row_ok & col_ok)[None], vals, NEG))
    t = jnp.stack(tables, axis=1)
    t = t.reshape(NA_HEADS // 2, 2, NA_PATTERNS, NA_WIN, LANE).transpose(0, 2, 3, 1, 4)
    return t.reshape(NA_HEADS // 2, NA_PATTERNS, NA_WIN, 2 * LANE).astype(F32)


def _outproj_kernel(a_ref, b_ref, w_ref, h_ref, mv_ref, g_ref, beta_ref, o_ref, *, b_token_major):
    ta = (((0,), (0,)), ((), ()))
    y = lax.dot_general(a_ref[0], w_ref[:512, :], ta, preferred_element_type=F32)
    if b_token_major:
        y = y + jnp.dot(b_ref[0], w_ref[512:, :], preferred_element_type=F32)
    else:
        y = y + lax.dot_general(b_ref[0], w_ref[512:, :], ta, preferred_element_type=F32)
    gate = mv_ref[0][2:3]
    o_ref[0] = _layer_norm(ALPHA * h_ref[0] + gate * y, g_ref[...], beta_ref[...])


def _outproj(mix_a, mix_b, w, h, modv, g, beta, n_tiles, b_token_major):
    if b_token_major:
        b_spec = pl.BlockSpec((1, TM, 512), lambda b, i: (b, i, 0))
    else:
        b_spec = pl.BlockSpec((1, 512, TM), lambda b, i: (b, 0, i))
    return pl.pallas_call(
        functools.partial(_outproj_kernel, b_token_major=b_token_major),
        out_shape=jax.ShapeDtypeStruct((B, n_tiles * TM, D), F32),
        grid=(B, n_tiles),
        in_specs=[
            pl.BlockSpec((1, 512, TM), lambda b, i: (b, 0, i)),
            b_spec,
            pl.BlockSpec((D, D), lambda b, i: (0, 0)),
            pl.BlockSpec((1, TM, D), lambda b, i: (b, i, 0)),
            pl.BlockSpec((1, N_MOD, D), lambda b, i: (_mod_row(b, i), 0, 0)),
            pl.BlockSpec((1, D), lambda b, i: (0, 0)),
            pl.BlockSpec((1, D), lambda b, i: (0, 0)),
        ],
        out_specs=pl.BlockSpec((1, TM, D), lambda b, i: (b, i, 0)),
        compiler_params=_cparams(2),
        name="outproj_ln",
    )(mix_a, mix_b, w, h, modv, g.reshape(1, D), beta.reshape(1, D))


def _ffn_kernel(h_ref, mv_ref, win_ref, wout_ref, g_ref, beta_ref, o_ref):
    u = _modulated(h_ref, mv_ref, 3)
    ga = jnp.dot(u, win_ref[...], preferred_element_type=F32)
    gt, a = ga[:, :FFN_HIDDEN], ga[:, FFN_HIDDEN:]
    act = (gt * (1.0 / (1.0 + jnp.exp(-gt))) * a).astype(BF16)
    y = jnp.dot(act, wout_ref[...], preferred_element_type=F32)
    gate = mv_ref[0][5:6]
    o_ref[0] = _layer_norm(ALPHA * h_ref[0] + gate * y, g_ref[...], beta_ref[...])


def _ffn(h, modv, w_in, w_out, g, beta, n_tiles):
    return pl.pallas_call(
        _ffn_kernel,
        out_shape=jax.ShapeDtypeStruct((B, n_tiles * TM, D), F32),
        grid=(B, n_tiles),
        in_specs=[
            pl.BlockSpec((1, TM, D), lambda b, i: (b, i, 0)),
            pl.BlockSpec((1, N_MOD, D), lambda b, i: (_mod_row(b, i), 0, 0)),
            pl.BlockSpec((D, 2 * FFN_HIDDEN), lambda b, i: (0, 0)),
            pl.BlockSpec((FFN_HIDDEN, D), lambda b, i: (0, 0)),
            pl.BlockSpec((1, D), lambda b, i: (0, 0)),
            pl.BlockSpec((1, D), lambda b, i: (0, 0)),
        ],
        out_specs=pl.BlockSpec((1, TM, D), lambda b, i: (b, i, 0)),
        compiler_params=_cparams(2),
        name="ffn_ln",
    )(h, modv, w_in, w_out, g.reshape(1, D), beta.reshape(1, D))


def _gelu(x):
    return 0.5 * x * (1.0 + lax.erf(x * (1.0 / math.sqrt(2.0))))


def _rms(x, g):
    ms = jnp.mean(x * x, axis=-1, keepdims=True)
    return x * lax.rsqrt(ms + RMS_EPS) * g


def _inproj_odd_kernel(h_ref, mv_ref, win_ref, gq_ref, gkv_ref, wuq_ref, wkn_ref, wkr_ref, wv_ref,
                       cosf_ref, sinf_ref, cost_ref, sina_ref, sinb_ref,
                       lng_ref, lnb_ref, ws_ref, gb_ref,
                       q_ref, k_ref, vt_ref, gm_ref):
    nt = (((1,), (1,)), ((), ()))
    u = _modulated(h_ref, mv_ref, 0)
    p = jnp.dot(u, win_ref[...], preferred_element_type=F32)
    cq = _rms(p[:, :MLA_Q_RANK], gq_ref[...]).astype(BF16)
    ckv = _rms(p[:, MLA_Q_RANK:MLA_Q_RANK + MLA_KV_RANK], gkv_ref[...]).astype(BF16)
    kr = p[:, 640:768]
    qt = lax.dot_general(wuq_ref[...], cq, nt, preferred_element_type=F32)
    qt = qt * ((MLA_NOPE + MLA_ROPE) ** -0.5)
    cf, sf = cosf_ref[...], sinf_ref[...]
    for hd in range(MLA_HEADS):
        base = hd * 128
        parts = [qt[base:base + 64]]
        for half in range(2):
            x1 = qt[base + 64 + half * 16:base + 72 + half * 16]
            x2 = qt[base + 72 + half * 16:base + 80 + half * 16]
            c = cf[half * 8:(half + 1) * 8]
            s = sf[half * 8:(half + 1) * 8]
            parts += [x1 * c - x2 * s, x2 * c + x1 * s]
        parts.append(qt[base + 96:base + 128])
        q_ref[0, base:base + 128, :] = jnp.concatenate(parts, axis=0).astype(BF16)
    kr = kr * cost_ref[...] + pltpu.roll(kr, LANE - 8, 1) * sina_ref[...] + pltpu.roll(kr, 8, 1) * sinb_ref[...]
    k = (jnp.dot(ckv, wkn_ref[...], preferred_element_type=F32)
         + jnp.dot(kr.astype(BF16), wkr_ref[...], preferred_element_type=F32))
    k_ref[0] = k.astype(BF16)
    vt_ref[0] = lax.dot_general(wv_ref[...], ckv, nt, preferred_element_type=F32).astype(BF16)
    gu = _gelu(p[:, 768:1280])
    gv = _layer_norm(_gelu(p[:, 1280:1792]), lng_ref[...], lnb_ref[...]).astype(BF16)
    for ch in range(TM // GMLP_CHUNK):
        rows = slice(ch * GMLP_CHUNK, (ch + 1) * GMLP_CHUNK)
        for grp in range(GMLP_GROUPS):
            cols = slice(grp * GMLP_CH, (grp + 1) * GMLP_CH)
            mixed = jnp.dot(ws_ref[grp], gv[rows, cols], preferred_element_type=F32) + gb_ref[grp]
            gm_ref[0, rows, cols] = (gu[rows, cols] * mixed).astype(BF16)


def _inproj_odd(h, modv, w, tabs):
    cosf, sinf, cost, sina, sinb = tabs
    const2 = lambda b, i: (0, 0)
    const3 = lambda b, i: (0, 0, 0)
    return pl.pallas_call(
        _inproj_odd_kernel,
        out_shape=(jax.ShapeDtypeStruct((B, 1024, N), BF16),
                   jax.ShapeDtypeStruct((B, N, 1024), BF16),
                   jax.ShapeDtypeStruct((B, 512, N), BF16),
                   jax.ShapeDtypeStruct((B, N, 512), BF16)),
        grid=(B, NT_ALL),
        in_specs=[
            pl.BlockSpec((1, TM, D), lambda b, i: (b, i, 0)),
            pl.BlockSpec((1, N_MOD, D), lambda b, i: (_mod_row(b, i), 0, 0)),
            pl.BlockSpec((D, 1792), const2),
            pl.BlockSpec((1, MLA_Q_RANK), const2),
            pl.BlockSpec((1, MLA_KV_RANK), const2),
            pl.BlockSpec((1024, MLA_Q_RANK), const2),
            pl.BlockSpec((MLA_KV_RANK, 1024), const2),
            pl.BlockSpec((LANE, 1024), const2),
            pl.BlockSpec((512, MLA_KV_RANK), const2),
            pl.BlockSpec((16, TM), lambda b, i: (0, i)),
            pl.BlockSpec((16, TM), lambda b, i: (0, i)),
            pl.BlockSpec((TM, LANE), lambda b, i: (i, 0)),
            pl.BlockSpec((TM, LANE), lambda b, i: (i, 0)),
            pl.BlockSpec((TM, LANE), lambda b, i: (i, 0)),
            pl.BlockSpec((1, GMLP_WIDTH), const2),
            pl.BlockSpec((1, GMLP_WIDTH), const2),
            pl.BlockSpec((GMLP_GROUPS, GMLP_CHUNK, GMLP_CHUNK), const3),
            pl.BlockSpec((GMLP_GROUPS, GMLP_CHUNK, 1), const3),
        ],
        out_specs=(pl.BlockSpec((1, 1024, TM), lambda b, i: (b, 0, i)),
                   pl.BlockSpec((1, TM, 1024), lambda b, i: (b, i, 0)),
                   pl.BlockSpec((1, 512, TM), lambda b, i: (b, 0, i)),
                   pl.BlockSpec((1, TM, 512), lambda b, i: (b, i, 0))),
        compiler_params=_cparams(2),
        name="inproj_odd",
    )(h, modv, w["win"], w["gq"], w["gkv"], w["wuq"], w["wkn"], w["wkr"], w["wv"],
      cosf, sinf, cost, sina, sinb, w["lng"], w["lnb"], w["ws"], w["gb"])


def _mla_attn_kernel(q_ref, k_ref, vt_ref, o_ref, acc_ref):
    l = _flash(q_ref[0], k_ref, vt_ref, acc_ref, True)
    o_ref[0] = (acc_ref[...] / l).astype(BF16)


def _mla_attention(q_t, k, v_t):
    return pl.pallas_call(
        _mla_attn_kernel,
        out_shape=jax.ShapeDtypeStruct((B, 512, S), BF16),
        grid=(B, MLA_HEADS, S // TQ),
        in_specs=[
            pl.BlockSpec((1, 128, TQ), lambda b, h, i: (b, h, i)),
            pl.BlockSpec((1, N, 128), lambda b, h, i: (b, 0, h)),
            pl.BlockSpec((1, MLA_V, N), lambda b, h, i: (b, h, 0)),
        ],
        out_specs=pl.BlockSpec((1, MLA_V, TQ), lambda b, h, i: (b, h, i)),
        scratch_shapes=[pltpu.VMEM((MLA_V, TQ), F32)],
        compiler_params=_cparams(3),
        name="mla_attention",
    )(q_t, k, v_t)


def _rope_tables(n_freq):
    t = jnp.arange(S)
    inv = jnp.power(ROPE_BASE, -jnp.arange(0, 2 * n_freq, 2, dtype=F32) / (2 * n_freq))
    ang_r = (t // GRID_W).astype(F32)[:, None] * inv[None, :]
    ang_c = (t % GRID_W).astype(F32)[:, None] * inv[None, :]
    cos = jnp.concatenate([jnp.cos(ang_r), jnp.cos(ang_c)], axis=1)
    sin = jnp.concatenate([jnp.sin(ang_r), jnp.sin(ang_c)], axis=1)
    cos = jnp.concatenate([cos, jnp.ones((L, 2 * n_freq), F32)], axis=0)
    sin = jnp.concatenate([sin, jnp.zeros((L, 2 * n_freq), F32)], axis=0)
    lane = np.arange(LANE)
    period = 4 * n_freq
    src = ((lane % period) // (2 * n_freq)) * n_freq + lane % n_freq
    is_x1 = (lane % (2 * n_freq)) < n_freq
    cos_t = cos[:, src]
    sin_t = sin[:, src]
    sin_a = jnp.where(jnp.asarray(is_x1)[None, :], -sin_t, 0.0)
    sin_b = jnp.where(jnp.asarray(is_x1)[None, :], 0.0, sin_t)
    return cos.T, sin.T, cos_t, sin_a, sin_b


def _even_weights(ev_w_in):
    aq, ak, av, bq, bk, bv = jnp.split(ev_w_in, [512, 1024, 1536, 2048, 2560], axis=1)
    scale = HEAD_DIM ** -0.5
    wtok = jnp.concatenate([ak, bk], axis=1).astype(BF16)
    wfeat = jnp.concatenate([aq * scale, av, bq * scale, bv], axis=1).T.astype(BF16)
    return wtok, wfeat


def _odd_weights(od_w_in, gq, w_uq, gkv, w_ukv, ln_g, ln_b, ws, gb):
    cq, ckv, kr, gu, gv = jnp.split(od_w_in, [384, 640, 672, 1184], axis=1)
    win = jnp.concatenate([cq, ckv, kr, jnp.zeros((D, LANE - MLA_ROPE), F32), gu, gv], axis=1)
    dq = MLA_NOPE + MLA_ROPE
    wuq = w_uq.reshape(MLA_Q_RANK, MLA_HEADS, dq)
    wuq = jnp.pad(wuq, ((0, 0), (0, 0), (0, LANE - dq))).reshape(MLA_Q_RANK, MLA_HEADS * LANE)
    wukv = w_ukv.reshape(MLA_KV_RANK, MLA_HEADS, MLA_NOPE + MLA_V)
    wkn = jnp.pad(wukv[:, :, :MLA_NOPE], ((0, 0), (0, 0), (0, LANE - MLA_NOPE)))
    wkn = wkn.reshape(MLA_KV_RANK, MLA_HEADS * LANE)
    wv = wukv[:, :, MLA_NOPE:].reshape(MLA_KV_RANK, MLA_HEADS * MLA_V)
    place = np.zeros((LANE, MLA_HEADS * LANE), np.float32)
    for hd in range(MLA_HEADS):
        place[np.arange(MLA_ROPE), hd * LANE + MLA_NOPE + np.arange(MLA_ROPE)] = 1.0
    return {
        "win": win.astype(BF16), "gq": gq.reshape(1, -1), "gkv": gkv.reshape(1, -1),
        "wuq": wuq.T.astype(BF16), "wkn": wkn.astype(BF16), "wkr": jnp.asarray(place, BF16),
        "wv": wv.T.astype(BF16), "lng": ln_g.reshape(1, -1), "lnb": ln_b.reshape(1, -1),
        "ws": ws.astype(BF16), "gb": gb.reshape(GMLP_GROUPS, GMLP_CHUNK, 1),
    }


def kernel(x, c, ctx, c_ctx, mod_w, mod_b, ln_mix_g, ln_mix_b, ln_ffn_g, ln_ffn_b, ffn_w_in, ffn_w_out,
           ev_w_in, ev_w_out, diff_lambda, diff_subln_g, na_rpb, od_w_in, od_w_out, mla_q_norm_g,
           mla_w_uq, mla_kv_norm_g, mla_w_ukv, gmlp_ln_g, gmlp_ln_b, gmlp_ws, gmlp_b):
    h = jnp.concatenate([x, ctx], axis=1)
    cond = jnp.concatenate([c, c_ctx[None], jnp.zeros((8 - B - 1, D), F32)], axis=0)
    mod = _modulation(cond, mod_w, mod_b).reshape(DEPTH, 8, N_MOD, D)

    wtok, wfeat = _even_weights(ev_w_in[0])
    tok, feat = _inproj_even(h, mod[0], wtok, wfeat, _rope_tables(16))
    mix_a = _diff_attention(tok, feat, diff_lambda[0], diff_subln_g[0])
    mix_b = _na_attention(tok, feat, _na_bias_table(na_rpb[0]))
    h = _outproj(mix_a, mix_b, ev_w_out[0].astype(BF16), h, mod[0], ln_mix_g[0], ln_mix_b[0],
                 NT_ALL, False)
    h = _ffn(h, mod[0], ffn_w_in[0].astype(BF16), ffn_w_out[0].astype(BF16),
             ln_ffn_g[0], ln_ffn_b[0], NT_ALL)

    w1 = _odd_weights(od_w_in[0], mla_q_norm_g[0], mla_w_uq[0], mla_kv_norm_g[0], mla_w_ukv[0],
                      gmlp_ln_g[0], gmlp_ln_b[0], gmlp_ws[0], gmlp_b[0])
    q_t, k, v_t, gm = _inproj_odd(h, mod[1], w1, _rope_tables(8))
    mix_a = _mla_attention(q_t, k, v_t)
    h = _outproj(mix_a, gm, od_w_out[0].astype(BF16), h, mod[1], ln_mix_g[1], ln_mix_b[1],
                 NT_LAT, True)
    return _ffn(h, mod[1], ffn_w_in[1].astype(BF16), ffn_w_out[1].astype(BF16),
                ln_ffn_g[1], ln_ffn_b[1], NT_LAT)
```

```python
import functools
import math

import jax
import jax.numpy as jnp
import numpy as np
from jax import lax
from jax.experimental import pallas as pl
from jax.experimental.pallas import tpu as pltpu

F32 = jnp.float32
BF16 = jnp.bfloat16

D = 1024
B = 2
S = 8192
L = 256
N = S + L
DEPTH = 2
GRID_W = 64
ROWS = S // GRID_W
ALPHA = (2 * DEPTH) ** 0.25
ROPE_BASE = 10000.0
LN_EPS = 1e-5
RMS_EPS = 1e-6
N_MOD = 6
HEAD_DIM = 64
DIFF_HEADS = 4
DIFF_V = 128
NA_HEADS = 8
NA_ROWS = 8
NA_COLS = 16
MLA_HEADS = 8
MLA_Q_RANK = 384
MLA_KV_RANK = 256
MLA_NOPE = 64
MLA_ROPE = 32
MLA_V = 64
GMLP_GROUPS = 4
GMLP_CH = 128
GMLP_CHUNK = 128
GMLP_WIDTH = 512
FFN_HIDDEN = 2816
LAMBDA_INIT_0 = 0.8 - 0.6 * math.exp(-0.3 * 0)

LANE = 128
TM = 256
NT_ALL = N // TM
NT_LAT = S // TM
TQ = 256
TK = 512
NA_WIN_ROWS = 10
NA_WIN = NA_WIN_ROWS * GRID_W
NA_PATTERNS = 5
NEG = -1e30
VMEM_LIMIT = 56 * 1024 * 1024


def _cparams(n_axes):
    return pltpu.CompilerParams(dimension_semantics=("parallel",) * n_axes,
                                vmem_limit_bytes=VMEM_LIMIT)


def _mod_row(b, i):
    return jnp.where(i < NT_LAT, b, 2)


def _mod_kernel(cond_ref, w_ref, b_ref, o_ref):
    cnd = cond_ref[...]
    act = cnd * (1.0 / (1.0 + jnp.exp(-cnd)))
    o_ref[0] = jnp.dot(act, w_ref[0], preferred_element_type=F32) + b_ref[0]


def _modulation(cond, mod_w, mod_b):
    tn = 1536
    return pl.pallas_call(
        _mod_kernel,
        out_shape=jax.ShapeDtypeStruct((DEPTH, 8, N_MOD * D), F32),
        grid=(DEPTH, N_MOD * D // tn),
        in_specs=[
            pl.BlockSpec((8, D), lambda l, j: (0, 0)),
            pl.BlockSpec((1, D, tn), lambda l, j: (l, 0, j)),
            pl.BlockSpec((1, 1, tn), lambda l, j: (l, 0, j)),
        ],
        out_specs=pl.BlockSpec((1, 8, tn), lambda l, j: (l, 0, j)),
        compiler_params=_cparams(2),
        name="modulation",
    )(cond, mod_w, mod_b.reshape(DEPTH, 1, N_MOD * D))


def _modulated(h_ref, mv_ref, k):
    mv = mv_ref[0]
    return (h_ref[0] * (1.0 + mv[k + 1:k + 2]) + mv[k:k + 1]).astype(BF16)


def _layer_norm(x, g, b):
    mu = jnp.mean(x, axis=-1, keepdims=True)
    xc = x - mu
    var = jnp.mean(xc * xc, axis=-1, keepdims=True)
    return xc * lax.rsqrt(var + LN_EPS) * g + b


def _inproj_even_kernel(h_ref, mv_ref, wtok_ref, wfeat_ref, cosf_ref, sinf_ref,
                        cost_ref, sina_ref, sinb_ref, tok_ref, feat_ref):
    u = _modulated(h_ref, mv_ref, 0)
    tok = jnp.dot(u, wtok_ref[...], preferred_element_type=F32)
    ct, sa, sb = cost_ref[...], sina_ref[...], sinb_ref[...]
    for j in range(4):
        xs = tok[:, j * LANE:(j + 1) * LANE]
        ro = xs * ct + pltpu.roll(xs, LANE - 16, 1) * sa + pltpu.roll(xs, 16, 1) * sb
        tok_ref[0, :, j * LANE:(j + 1) * LANE] = ro.astype(BF16)
    tok_ref[0, :, 512:] = tok[:, 512:].astype(BF16)
    feat = lax.dot_general(wfeat_ref[...], u, (((1,), (1,)), ((), ())),
                           preferred_element_type=F32)
    cf, sf = cosf_ref[...], sinf_ref[...]
    for blk in range(8):
        for half in range(2):
            base = blk * 64 + half * 32
            x1 = feat[base:base + 16]
            x2 = feat[base + 16:base + 32]
            c = cf[half * 16:(half + 1) * 16]
            s = sf[half * 16:(half + 1) * 16]
            feat_ref[0, base:base + 16, :] = (x1 * c - x2 * s).astype(BF16)
            feat_ref[0, base + 16:base + 32, :] = (x2 * c + x1 * s).astype(BF16)
    feat_ref[0, 512:, :] = feat[512:].astype(BF16)


def _inproj_even(h, modv, wtok, wfeat, tabs):
    cosf, sinf, cost, sina, sinb = tabs
    const = lambda b, i: (0, 0)
    return pl.pallas_call(
        _inproj_even_kernel,
        out_shape=(jax.ShapeDtypeStruct((B, N, 1024), BF16),
                   jax.ShapeDtypeStruct((B, 2048, N), BF16)),
        grid=(B, NT_ALL),
        in_specs=[
            pl.BlockSpec((1, TM, D), lambda b, i: (b, i, 0)),
            pl.BlockSpec((1, N_MOD, D), lambda b, i: (_mod_row(b, i), 0, 0)),
            pl.BlockSpec((D, 1024), const),
            pl.BlockSpec((2048, D), const),
            pl.BlockSpec((32, TM), lambda b, i: (0, i)),
            pl.BlockSpec((32, TM), lambda b, i: (0, i)),
            pl.BlockSpec((TM, LANE), lambda b, i: (i, 0)),
            pl.BlockSpec((TM, LANE), lambda b, i: (i, 0)),
            pl.BlockSpec((TM, LANE), lambda b, i: (i, 0)),
        ],
        out_specs=(pl.BlockSpec((1, TM, 1024), lambda b, i: (b, i, 0)),
                   pl.BlockSpec((1, 2048, TM), lambda b, i: (b, 0, i))),
        compiler_params=_cparams(2),
        name="inproj_even",
    )(h, modv, wtok, wfeat, cosf, sinf, cost, sina, sinb)


def _block_diag2(q_t):
    n = q_t.shape[1]
    z = jnp.zeros((64, n), q_t.dtype)
    left = jnp.concatenate([q_t[:64], z], axis=0)
    right = jnp.concatenate([z, q_t[64:]], axis=0)
    return jnp.concatenate([left, right], axis=1)


def _flash(rhs, k_ref, vt_ref, chunks):
    width = rhs.shape[1]
    m = jnp.full((1, width), -jnp.inf, F32)
    l = jnp.zeros((1, width), F32)
    acc = jnp.zeros((vt_ref.shape[1], width), F32)

    def scores(idx):
        st, sz = chunks[idx]
        return jnp.dot(k_ref[0, st:st + sz, :], rhs, preferred_element_type=F32)

    s_next = scores(0)
    for idx, (st, sz) in enumerate(chunks):
        s = s_next
        if idx + 1 < len(chunks):
            s_next = scores(idx + 1)
        m_new = jnp.maximum(m, jnp.max(s, axis=0, keepdims=True))
        alpha = jnp.exp(m - m_new)
        p = jnp.exp(s - m_new)
        l = alpha * l + jnp.sum(p, axis=0, keepdims=True)
        acc = acc * alpha + jnp.dot(vt_ref[0, :, st:st + sz], p.astype(BF16),
                                    preferred_element_type=F32)
        m = m_new
    return acc, l


LATENT_CHUNKS = [(j * TK, TK) for j in range(S // TK)] + [(S, L)]
CONTEXT_CHUNKS = [(S, L)]


def _diff_attn_kernel(lam_ref, g_ref, q_ref, k_ref, vt_ref, o_ref):
    i = pl.program_id(2)

    def attend(chunks):
        acc, l = _flash(_block_diag2(q_ref[0]), k_ref, vt_ref, chunks)
        lf = lam_ref[...]
        lam = (jnp.exp(jnp.sum(lf[0:1] * lf[1:2], axis=1, keepdims=True))
               - jnp.exp(jnp.sum(lf[2:3] * lf[3:4], axis=1, keepdims=True)) + LAMBDA_INIT_0)
        o = acc[:, :TQ] / l[:, :TQ] - lam * (acc[:, TQ:] / l[:, TQ:])
        ms = jnp.mean(o * o, axis=0, keepdims=True)
        o = o * lax.rsqrt(ms + RMS_EPS) * g_ref[...] * (1.0 - LAMBDA_INIT_0)
        o_ref[0] = o.astype(BF16)

    @pl.when(i < S // TQ)
    def _latent():
        attend(LATENT_CHUNKS)

    @pl.when(i == S // TQ)
    def _context():
        attend(CONTEXT_CHUNKS)


def _diff_attention(tok, feat, diff_lambda, subln_g):
    return pl.pallas_call(
        _diff_attn_kernel,
        out_shape=jax.ShapeDtypeStruct((B, 512, N), BF16),
        grid=(B, DIFF_HEADS, N // TQ),
        in_specs=[
            pl.BlockSpec((4, HEAD_DIM), lambda b, h, i: (0, 0)),
            pl.BlockSpec((DIFF_V, 1), lambda b, h, i: (0, 0)),
            pl.BlockSpec((1, 128, TQ), lambda b, h, i: (b, h, i)),
            pl.BlockSpec((1, N, 128), lambda b, h, i: (b, 0, h)),
            pl.BlockSpec((1, 128, N), lambda b, h, i: (b, 4 + h, 0)),
        ],
        out_specs=pl.BlockSpec((1, 128, TQ), lambda b, h, i: (b, h, i)),
        compiler_params=_cparams(3),
        name="diff_attention",
    )(diff_lambda, subln_g.reshape(DIFF_V, 1), feat, tok, feat)


def _na_kernel(q_ref, k_ref, vt_ref, bias_ref, o_ref):
    i = pl.program_id(2)
    k_ctx = k_ref[0, S:, :]
    vt_ctx = vt_ref[0, :, S:]

    def attend(rhs, k_win, vt_win, bias):
        s_c = jnp.dot(k_ctx, rhs, preferred_element_type=F32)
        m = jnp.max(s_c, axis=0, keepdims=True)
        if k_win is not None:
            s_w = jnp.dot(k_win, rhs, preferred_element_type=F32) + bias
            m = jnp.maximum(m, jnp.max(s_w, axis=0, keepdims=True))
        p_c = jnp.exp(s_c - m)
        l = jnp.sum(p_c, axis=0, keepdims=True)
        o = jnp.dot(vt_ctx, p_c.astype(BF16), preferred_element_type=F32)
        if k_win is not None:
            p_w = jnp.exp(s_w - m)
            l = l + jnp.sum(p_w, axis=0, keepdims=True)
            o = o + jnp.dot(vt_win, p_w.astype(BF16), preferred_element_type=F32)
        o = o / l
        n = rhs.shape[1] // 2
        return jnp.concatenate([o[:64, :n], o[64:, n:]], axis=0).astype(BF16)

    @pl.when(i < S // TQ)
    def _latent():
        for jj in range(TQ // LANE):
            r = 2 * (i * (TQ // LANE) + jj)
            start = jnp.clip(r - NA_ROWS // 2, 0, ROWS - NA_WIN_ROWS)
            pat = jnp.where(r == 0, 0, jnp.where(r == 2, 1, jnp.where(
                r == ROWS - 4, 3, jnp.where(r == ROWS - 2, 4, 2))))
            off = pl.multiple_of(start * GRID_W, LANE)
            rhs = _block_diag2(q_ref[0, :, jj * LANE:(jj + 1) * LANE])
            o_ref[0, :, jj * LANE:(jj + 1) * LANE] = attend(
                rhs, k_ref[0, pl.ds(off, NA_WIN), :], vt_ref[0, :, pl.ds(off, NA_WIN)],
                bias_ref[0, pat])

    @pl.when(i == S // TQ)
    def _context():
        o_ref[0] = attend(_block_diag2(q_ref[0]), None, None, None)


def _na_attention(tok, feat, bias):
    return pl.pallas_call(
        _na_kernel,
        out_shape=jax.ShapeDtypeStruct((B, 512, N), BF16),
        grid=(B, NA_HEADS // 2, N // TQ),
        in_specs=[
            pl.BlockSpec((1, 128, TQ), lambda b, h, i: (b, 8 + h, i)),
            pl.BlockSpec((1, N, 128), lambda b, h, i: (b, 0, 4 + h)),
            pl.BlockSpec((1, 128, N), lambda b, h, i: (b, 12 + h, 0)),
            pl.BlockSpec((1, NA_PATTERNS, NA_WIN, 2 * LANE), lambda b, h, i: (h, 0, 0, 0)),
        ],
        out_specs=pl.BlockSpec((1, 128, TQ), lambda b, h, i: (b, h, i)),
        compiler_params=_cparams(3),
        name="na_attention",
    )(feat, tok, feat, bias)


def _na_bias_table(rpb):
    pats = [(0, 0), (2, 0), (8, 4), (ROWS - 4, ROWS - NA_WIN_ROWS), (ROWS - 2, ROWS - NA_WIN_ROWS)]
    kr_rel = np.arange(NA_WIN_ROWS)
    j = np.arange(2)
    drow = np.zeros((NA_PATTERNS, NA_WIN_ROWS, 2), np.int32)
    row_ok = np.zeros((NA_PATTERNS, NA_WIN_ROWS, 2), bool)
    for p, (r, start) in enumerate(pats):
        rq = r + j
        rs = np.clip(rq - NA_ROWS // 2, 0, ROWS - NA_ROWS)
        kr = start + kr_rel
        row_ok[p] = (kr[:, None] >= rs[None, :]) & (kr[:, None] < rs[None, :] + NA_ROWS)
        drow[p] = np.clip(kr[:, None] - rq[None, :] + NA_ROWS - 1, 0, 2 * NA_ROWS - 2)
    kc = np.arange(GRID_W)
    c = np.arange(GRID_W)
    cs = np.clip(c - NA_COLS // 2, 0, GRID_W - NA_COLS)
    col_ok = (kc[:, None] >= cs[None, :]) & (kc[:, None] < cs[None, :] + NA_COLS)
    dcol = kc[:, None] - c[None, :] + NA_COLS - 1
    onehot = (dcol[None] == np.arange(2 * NA_COLS - 1)[:, None, None]) & col_ok[None]
    rows = rpb[:, drow.reshape(-1), :]
    t = jnp.einsum("hrd,dkc->hrkc", rows, jnp.asarray(onehot, F32), precision=lax.Precision.HIGHEST)
    valid = row_ok.reshape(-1)[None, :, None, None] & col_ok[None, None]
    t = jnp.where(jnp.asarray(valid), t, NEG)
    t = t.reshape(NA_HEADS // 2, 2, NA_PATTERNS, NA_WIN_ROWS, 2, GRID_W, GRID_W)
    t = t.transpose(0, 2, 3, 5, 1, 4, 6)
    return t.reshape(NA_HEADS // 2, NA_PATTERNS, NA_WIN, 2 * LANE)


def _outproj_kernel(a_ref, b_ref, w_ref, h_ref, mv_ref, g_ref, beta_ref, o_ref, *, b_token_major):
    ta = (((0,), (0,)), ((), ()))
    y = lax.dot_general(a_ref[0], w_ref[:512, :], ta, preferred_element_type=F32)
    if b_token_major:
        y = y + jnp.dot(b_ref[0], w_ref[512:, :], preferred_element_type=F32)
    else:
        y = y + lax.dot_general(b_ref[0], w_ref[512:, :], ta, preferred_element_type=F32)
    gate = mv_ref[0][2:3]
    o_ref[0] = _layer_norm(ALPHA * h_ref[0] + gate * y, g_ref[...], beta_ref[...])


def _outproj(mix_a, mix_b, w, h, modv, g, beta, n_tiles, b_token_major):
    if b_token_major:
        b_spec = pl.BlockSpec((1, TM, 512), lambda b, i: (b, i, 0))
    else:
        b_spec = pl.BlockSpec((1, 512, TM), lambda b, i: (b, 0, i))
    return pl.pallas_call(
        functools.partial(_outproj_kernel, b_token_major=b_token_major),
        out_shape=jax.ShapeDtypeStruct((B, n_tiles * TM, D), F32),
        grid=(B, n_tiles),
        in_specs=[
            pl.BlockSpec((1, 512, TM), lambda b, i: (b, 0, i)),
            b_spec,
            pl.BlockSpec((D, D), lambda b, i: (0, 0)),
            pl.BlockSpec((1, TM, D), lambda b, i: (b, i, 0)),
            pl.BlockSpec((1, N_MOD, D), lambda b, i: (_mod_row(b, i), 0, 0)),
            pl.BlockSpec((1, D), lambda b, i: (0, 0)),
            pl.BlockSpec((1, D), lambda b, i: (0, 0)),
        ],
        out_specs=pl.BlockSpec((1, TM, D), lambda b, i: (b, i, 0)),
        compiler_params=_cparams(2),
        name="outproj_ln",
    )(mix_a, mix_b, w, h, modv, g.reshape(1, D), beta.reshape(1, D))


def _ffn_kernel(h_ref, mv_ref, win_ref, wout_ref, g_ref, beta_ref, o_ref):
    u = _modulated(h_ref, mv_ref, 3)
    ga = jnp.dot(u, win_ref[...], preferred_element_type=F32)
    gt, a = ga[:, :FFN_HIDDEN], ga[:, FFN_HIDDEN:]
    act = (gt * (1.0 / (1.0 + jnp.exp(-gt))) * a).astype(BF16)
    y = jnp.dot(act, wout_ref[...], preferred_element_type=F32)
    gate = mv_ref[0][5:6]
    o_ref[0] = _layer_norm(ALPHA * h_ref[0] + gate * y, g_ref[...], beta_ref[...])


def _ffn(h, modv, w_in, w_out, g, beta, n_tiles):
    return pl.pallas_call(
        _ffn_kernel,
        out_shape=jax.ShapeDtypeStruct((B, n_tiles * TM, D), F32),
        grid=(B, n_tiles),
        in_specs=[
            pl.BlockSpec((1, TM, D), lambda b, i: (b, i, 0)),
            pl.BlockSpec((1, N_MOD, D), lambda b, i: (_mod_row(b, i), 0, 0)),
            pl.BlockSpec((D, 2 * FFN_HIDDEN), lambda b, i: (0, 0)),
            pl.BlockSpec((FFN_HIDDEN, D), lambda b, i: (0, 0)),
            pl.BlockSpec((1, D), lambda b, i: (0, 0)),
            pl.BlockSpec((1, D), lambda b, i: (0, 0)),
        ],
        out_specs=pl.BlockSpec((1, TM, D), lambda b, i: (b, i, 0)),
        compiler_params=_cparams(2),
        name="ffn_ln",
    )(h, modv, w_in, w_out, g.reshape(1, D), beta.reshape(1, D))


def _gelu(x):
    return 0.5 * x * (1.0 + lax.erf(x * (1.0 / math.sqrt(2.0))))


def _rms(x, g):
    ms = jnp.mean(x * x, axis=-1, keepdims=True)
    return x * lax.rsqrt(ms + RMS_EPS) * g


def _inproj_odd_kernel(h_ref, mv_ref, win_ref, gq_ref, gkv_ref, wuq_ref, wkn_ref, wkr_ref, wv_ref,
                       cosf_ref, sinf_ref, cost_ref, sina_ref, sinb_ref,
                       lng_ref, lnb_ref, ws_ref, gb_ref,
                       q_ref, k_ref, vt_ref, gm_ref):
    nt = (((1,), (1,)), ((), ()))
    u = _modulated(h_ref, mv_ref, 0)
    p = jnp.dot(u, win_ref[...], preferred_element_type=F32)
    cq = _rms(p[:, :MLA_Q_RANK], gq_ref[...]).astype(BF16)
    ckv = _rms(p[:, MLA_Q_RANK:MLA_Q_RANK + MLA_KV_RANK], gkv_ref[...]).astype(BF16)
    kr = p[:, 640:768]
    qt = lax.dot_general(wuq_ref[...], cq, nt, preferred_element_type=F32)
    qt = qt * ((MLA_NOPE + MLA_ROPE) ** -0.5)
    cf, sf = cosf_ref[...], sinf_ref[...]
    for hd in range(MLA_HEADS):
        base = hd * 128
        parts = [qt[base:base + 64]]
        for half in range(2):
            x1 = qt[base + 64 + half * 16:base + 72 + half * 16]
            x2 = qt[base + 72 + half * 16:base + 80 + half * 16]
            c = cf[half * 8:(half + 1) * 8]
            s = sf[half * 8:(half + 1) * 8]
            parts += [x1 * c - x2 * s, x2 * c + x1 * s]
        parts.append(qt[base + 96:base + 128])
        q_ref[0, base:base + 128, :] = jnp.concatenate(parts, axis=0).astype(BF16)
    kr = kr * cost_ref[...] + pltpu.roll(kr, LANE - 8, 1) * sina_ref[...] + pltpu.roll(kr, 8, 1) * sinb_ref[...]
    k = (jnp.dot(ckv, wkn_ref[...], preferred_element_type=F32)
         + jnp.dot(kr.astype(BF16), wkr_ref[...], preferred_element_type=F32))
    k_ref[0] = k.astype(BF16)
    vt_ref[0] = lax.dot_general(wv_ref[...], ckv, nt, preferred_element_type=F32).astype(BF16)
    gu = _gelu(p[:, 768:1280])
    gv = _layer_norm(_gelu(p[:, 1280:1792]), lng_ref[...], lnb_ref[...]).astype(BF16)
    for ch in range(TM // GMLP_CHUNK):
        rows = slice(ch * GMLP_CHUNK, (ch + 1) * GMLP_CHUNK)
        for grp in range(GMLP_GROUPS):
            cols = slice(grp * GMLP_CH, (grp + 1) * GMLP_CH)
            mixed = jnp.dot(ws_ref[grp], gv[rows, cols], preferred_element_type=F32) + gb_ref[grp]
            gm_ref[0, rows, cols] = (gu[rows, cols] * mixed).astype(BF16)


def _inproj_odd(h, modv, w, tabs):
    cosf, sinf, cost, sina, sinb = tabs
    const2 = lambda b, i: (0, 0)
    const3 = lambda b, i: (0, 0, 0)
    return pl.pallas_call(
        _inproj_odd_kernel,
        out_shape=(jax.ShapeDtypeStruct((B, 1024, N), BF16),
                   jax.ShapeDtypeStruct((B, N, 1024), BF16),
                   jax.ShapeDtypeStruct((B, 512, N), BF16),
                   jax.ShapeDtypeStruct((B, N, 512), BF16)),
        grid=(B, NT_ALL),
        in_specs=[
            pl.BlockSpec((1, TM, D), lambda b, i: (b, i, 0)),
            pl.BlockSpec((1, N_MOD, D), lambda b, i: (_mod_row(b, i), 0, 0)),
            pl.BlockSpec((D, 1792), const2),
            pl.BlockSpec((1, MLA_Q_RANK), const2),
            pl.BlockSpec((1, MLA_KV_RANK), const2),
            pl.BlockSpec((1024, MLA_Q_RANK), const2),
            pl.BlockSpec((MLA_KV_RANK, 1024), const2),
            pl.BlockSpec((LANE, 1024), const2),
            pl.BlockSpec((512, MLA_KV_RANK), const2),
            pl.BlockSpec((16, TM), lambda b, i: (0, i)),
            pl.BlockSpec((16, TM), lambda b, i: (0, i)),
            pl.BlockSpec((TM, LANE), lambda b, i: (i, 0)),
            pl.BlockSpec((TM, LANE), lambda b, i: (i, 0)),
            pl.BlockSpec((TM, LANE), lambda b, i: (i, 0)),
            pl.BlockSpec((1, GMLP_WIDTH), const2),
            pl.BlockSpec((1, GMLP_WIDTH), const2),
            pl.BlockSpec((GMLP_GROUPS, GMLP_CHUNK, GMLP_CHUNK), const3),
            pl.BlockSpec((GMLP_GROUPS, GMLP_CHUNK, 1), const3),
        ],
        out_specs=(pl.BlockSpec((1, 1024, TM), lambda b, i: (b, 0, i)),
                   pl.BlockSpec((1, TM, 1024), lambda b, i: (b, i, 0)),
                   pl.BlockSpec((1, 512, TM), lambda b, i: (b, 0, i)),
                   pl.BlockSpec((1, TM, 512), lambda b, i: (b, i, 0))),
        compiler_params=_cparams(2),
        name="inproj_odd",
    )(h, modv, w["win"], w["gq"], w["gkv"], w["wuq"], w["wkn"], w["wkr"], w["wv"],
      cosf, sinf, cost, sina, sinb, w["lng"], w["lnb"], w["ws"], w["gb"])


def _mla_attn_kernel(q_ref, k_ref, vt_ref, o_ref):
    acc, l = _flash(q_ref[0], k_ref, vt_ref, LATENT_CHUNKS)
    o_ref[0] = (acc / l).astype(BF16)


def _mla_attention(q_t, k, v_t):
    return pl.pallas_call(
        _mla_attn_kernel,
        out_shape=jax.ShapeDtypeStruct((B, 512, S), BF16),
        grid=(B, MLA_HEADS, S // TQ),
        in_specs=[
            pl.BlockSpec((1, 128, TQ), lambda b, h, i: (b, h, i)),
            pl.BlockSpec((1, N, 128), lambda b, h, i: (b, 0, h)),
            pl.BlockSpec((1, MLA_V, N), lambda b, h, i: (b, h, 0)),
        ],
        out_specs=pl.BlockSpec((1, MLA_V, TQ), lambda b, h, i: (b, h, i)),
        compiler_params=_cparams(3),
        name="mla_attention",
    )(q_t, k, v_t)


def _rope_tables(n_freq):
    t = jnp.arange(S)
    inv = jnp.power(ROPE_BASE, -jnp.arange(0, 2 * n_freq, 2, dtype=F32) / (2 * n_freq))
    ang_r = (t // GRID_W).astype(F32)[:, None] * inv[None, :]
    ang_c = (t % GRID_W).astype(F32)[:, None] * inv[None, :]
    cos = jnp.concatenate([jnp.cos(ang_r), jnp.cos(ang_c)], axis=1)
    sin = jnp.concatenate([jnp.sin(ang_r), jnp.sin(ang_c)], axis=1)
    cos = jnp.concatenate([cos, jnp.ones((L, 2 * n_freq), F32)], axis=0)
    sin = jnp.concatenate([sin, jnp.zeros((L, 2 * n_freq), F32)], axis=0)
    lane = np.arange(LANE)
    period = 4 * n_freq
    src = ((lane % period) // (2 * n_freq)) * n_freq + lane % n_freq
    is_x1 = (lane % (2 * n_freq)) < n_freq
    cos_t = cos[:, src]
    sin_t = sin[:, src]
    sin_a = jnp.where(jnp.asarray(is_x1)[None, :], -sin_t, 0.0)
    sin_b = jnp.where(jnp.asarray(is_x1)[None, :], 0.0, sin_t)
    return cos.T, sin.T, cos_t, sin_a, sin_b


def _even_weights(ev_w_in):
    aq, ak, av, bq, bk, bv = jnp.split(ev_w_in, [512, 1024, 1536, 2048, 2560], axis=1)
    scale = HEAD_DIM ** -0.5
    wtok = jnp.concatenate([ak, bk], axis=1).astype(BF16)
    wfeat = jnp.concatenate([aq * scale, av, bq * scale, bv], axis=1).T.astype(BF16)
    return wtok, wfeat


def _odd_weights(od_w_in, gq, w_uq, gkv, w_ukv, ln_g, ln_b, ws, gb):
    cq, ckv, kr, gu, gv = jnp.split(od_w_in, [384, 640, 672, 1184], axis=1)
    win = jnp.concatenate([cq, ckv, kr, jnp.zeros((D, LANE - MLA_ROPE), F32), gu, gv], axis=1)
    dq = MLA_NOPE + MLA_ROPE
    wuq = w_uq.reshape(MLA_Q_RANK, MLA_HEADS, dq)
    wuq = jnp.pad(wuq, ((0, 0), (0, 0), (0, LANE - dq))).reshape(MLA_Q_RANK, MLA_HEADS * LANE)
    wukv = w_ukv.reshape(MLA_KV_RANK, MLA_HEADS, MLA_NOPE + MLA_V)
    wkn = jnp.pad(wukv[:, :, :MLA_NOPE], ((0, 0), (0, 0), (0, LANE - MLA_NOPE)))
    wkn = wkn.reshape(MLA_KV_RANK, MLA_HEADS * LANE)
    wv = wukv[:, :, MLA_NOPE:].reshape(MLA_KV_RANK, MLA_HEADS * MLA_V)
    place = np.zeros((LANE, MLA_HEADS * LANE), np.float32)
    for hd in range(MLA_HEADS):
        place[np.arange(MLA_ROPE), hd * LANE + MLA_NOPE + np.arange(MLA_ROPE)] = 1.0
    return {
        "win": win.astype(BF16), "gq": gq.reshape(1, -1), "gkv": gkv.reshape(1, -1),
        "wuq": wuq.T.astype(BF16), "wkn": wkn.astype(BF16), "wkr": jnp.asarray(place, BF16),
        "wv": wv.T.astype(BF16), "lng": ln_g.reshape(1, -1), "lnb": ln_b.reshape(1, -1),
        "ws": ws.astype(BF16), "gb": gb.reshape(GMLP_GROUPS, GMLP_CHUNK, 1),
    }


def kernel(x, c, ctx, c_ctx, mod_w, mod_b, ln_mix_g, ln_mix_b, ln_ffn_g, ln_ffn_b, ffn_w_in, ffn_w_out,
           ev_w_in, ev_w_out, diff_lambda, diff_subln_g, na_rpb, od_w_in, od_w_out, mla_q_norm_g,
           mla_w_uq, mla_kv_norm_g, mla_w_ukv, gmlp_ln_g, gmlp_ln_b, gmlp_ws, gmlp_b):
    h = jnp.concatenate([x, ctx], axis=1)
    cond = jnp.concatenate([c, c_ctx[None], jnp.zeros((8 - B - 1, D), F32)], axis=0)
    mod = _modulation(cond, mod_w, mod_b).reshape(DEPTH, 8, N_MOD, D)

    wtok, wfeat = _even_weights(ev_w_in[0])
    tok, feat = _inproj_even(h, mod[0], wtok, wfeat, _rope_tables(16))
    mix_a = _diff_attention(tok, feat, diff_lambda[0], diff_subln_g[0])
    mix_b = _na_attention(tok, feat, _na_bias_table(na_rpb[0]))
    h = _outproj(mix_a, mix_b, ev_w_out[0].astype(BF16), h, mod[0], ln_mix_g[0], ln_mix_b[0],
                 NT_ALL, False)
    h = _ffn(h, mod[0], ffn_w_in[0].astype(BF16), ffn_w_out[0].astype(BF16),
             ln_ffn_g[0], ln_ffn_b[0], NT_ALL)

    w1 = _odd_weights(od_w_in[0], mla_q_norm_g[0], mla_w_uq[0], mla_kv_norm_g[0], mla_w_ukv[0],
                      gmlp_ln_g[0], gmlp_ln_b[0], gmlp_ws[0], gmlp_b[0])
    q_t, k, v_t, gm = _inproj_odd(h, mod[1], w1, _rope_tables(8))
    mix_a = _mla_attention(q_t, k, v_t)
    h = _outproj(mix_a, gm, od_w_out[0].astype(BF16), h, mod[1], ln_mix_g[1], ln_mix_b[1],
                 NT_LAT, True)
    return _ffn(h, mod[1], ffn_w_in[1].astype(BF16), ffn_w_out[1].astype(BF16),
                ln_ffn_g[1], ln_ffn_b[1], NT_LAT)
```

```python
import functools
import math

import jax
import jax.numpy as jnp
import numpy as np
from jax import lax
from jax.experimental import pallas as pl
from jax.experimental.pallas import tpu as pltpu

F32 = jnp.float32
BF16 = jnp.bfloat16

D = 1024
B = 2
S = 8192
L = 256
N = S + L
DEPTH = 2
GRID_W = 64
ROWS = S // GRID_W
ALPHA = (2 * DEPTH) ** 0.25
ROPE_BASE = 10000.0
LN_EPS = 1e-5
RMS_EPS = 1e-6
N_MOD = 6
HEAD_DIM = 64
DIFF_HEADS = 4
DIFF_V = 128
NA_HEADS = 8
NA_ROWS = 8
NA_COLS = 16
MLA_HEADS = 8
MLA_Q_RANK = 384
MLA_KV_RANK = 256
MLA_NOPE = 64
MLA_ROPE = 32
MLA_V = 64
GMLP_GROUPS = 4
GMLP_CH = 128
GMLP_CHUNK = 128
GMLP_WIDTH = 512
FFN_HIDDEN = 2816
LAMBDA_INIT_0 = 0.8 - 0.6 * math.exp(-0.3 * 0)
LOG2E = math.log2(math.e)

LANE = 128
TM = 256
NT_ALL = N // TM
NT_LAT = S // TM
TQ_DIFF, TK_DIFF = 512, 256
TQ_MLA, TK_MLA = 1024, 256
TQ_NA = 512
NA_WIN_ROWS = 10
NA_WIN = NA_WIN_ROWS * GRID_W
NA_PATTERNS = 5
NEG = -1e30
VMEM_LIMIT = 56 * 1024 * 1024


def _cparams(n_axes):
    return pltpu.CompilerParams(dimension_semantics=("parallel",) * n_axes,
                                vmem_limit_bytes=VMEM_LIMIT)


def _mod_row(b, i):
    return jnp.where(i < NT_LAT, b, 2)


def _mod_kernel(cond_ref, w_ref, b_ref, o_ref):
    cnd = cond_ref[...]
    act = cnd * (1.0 / (1.0 + jnp.exp(-cnd)))
    o_ref[0] = jnp.dot(act, w_ref[0], preferred_element_type=F32) + b_ref[0]


def _modulation(cond, mod_w, mod_b):
    tn = 1536
    return pl.pallas_call(
        _mod_kernel,
        out_shape=jax.ShapeDtypeStruct((DEPTH, 8, N_MOD * D), F32),
        grid=(DEPTH, N_MOD * D // tn),
        in_specs=[
            pl.BlockSpec((8, D), lambda l, j: (0, 0)),
            pl.BlockSpec((1, D, tn), lambda l, j: (l, 0, j)),
            pl.BlockSpec((1, 1, tn), lambda l, j: (l, 0, j)),
        ],
        out_specs=pl.BlockSpec((1, 8, tn), lambda l, j: (l, 0, j)),
        compiler_params=_cparams(2),
        name="modulation",
    )(cond, mod_w, mod_b.reshape(DEPTH, 1, N_MOD * D))


def _modulated(h_ref, mv_ref, k):
    mv = mv_ref[0]
    return (h_ref[0] * (1.0 + mv[k + 1:k + 2]) + mv[k:k + 1]).astype(BF16)


def _layer_norm(x, g, b):
    mu = jnp.mean(x, axis=-1, keepdims=True)
    xc = x - mu
    var = jnp.mean(xc * xc, axis=-1, keepdims=True)
    return xc * lax.rsqrt(var + LN_EPS) * g + b


def _inproj_even_kernel(h_ref, mv_ref, wtok_ref, wfeat_ref, cosf_ref, sinf_ref,
                        cost_ref, sina_ref, sinb_ref, tok_ref, feat_ref):
    u = _modulated(h_ref, mv_ref, 0)
    tok = jnp.dot(u, wtok_ref[...], preferred_element_type=F32)
    ct, sa, sb = cost_ref[...], sina_ref[...], sinb_ref[...]
    for j in range(4):
        xs = tok[:, j * LANE:(j + 1) * LANE]
        ro = xs * ct + pltpu.roll(xs, LANE - 16, 1) * sa + pltpu.roll(xs, 16, 1) * sb
        tok_ref[0, :, j * LANE:(j + 1) * LANE] = ro.astype(BF16)
    tok_ref[0, :, 512:] = tok[:, 512:].astype(BF16)
    feat = lax.dot_general(wfeat_ref[...], u, (((1,), (1,)), ((), ())),
                           preferred_element_type=F32)
    cf, sf = cosf_ref[...], sinf_ref[...]
    for blk in range(8):
        for half in range(2):
            base = blk * 64 + half * 32
            x1 = feat[base:base + 16]
            x2 = feat[base + 16:base + 32]
            c = cf[half * 16:(half + 1) * 16]
            s = sf[half * 16:(half + 1) * 16]
            feat_ref[0, base:base + 16, :] = (x1 * c - x2 * s).astype(BF16)
            feat_ref[0, base + 16:base + 32, :] = (x2 * c + x1 * s).astype(BF16)
    feat_ref[0, 512:1024, :] = feat[512:1024].astype(BF16)
    feat_ref[0, 1024:1536, :] = (feat[1024:1536] * LOG2E).astype(BF16)
    feat_ref[0, 1536:, :] = feat[1536:].astype(BF16)


def _inproj_even(h, modv, wtok, wfeat, tabs):
    cosf, sinf, cost, sina, sinb = tabs
    const = lambda b, i: (0, 0)
    return pl.pallas_call(
        _inproj_even_kernel,
        out_shape=(jax.ShapeDtypeStruct((B, N, 1024), BF16),
                   jax.ShapeDtypeStruct((B, 2048, N), BF16)),
        grid=(B, NT_ALL),
        in_specs=[
            pl.BlockSpec((1, TM, D), lambda b, i: (b, i, 0)),
            pl.BlockSpec((1, N_MOD, D), lambda b, i: (_mod_row(b, i), 0, 0)),
            pl.BlockSpec((D, 1024), const),
            pl.BlockSpec((2048, D), const),
            pl.BlockSpec((32, TM), lambda b, i: (0, i)),
            pl.BlockSpec((32, TM), lambda b, i: (0, i)),
            pl.BlockSpec((TM, LANE), lambda b, i: (i, 0)),
            pl.BlockSpec((TM, LANE), lambda b, i: (i, 0)),
            pl.BlockSpec((TM, LANE), lambda b, i: (i, 0)),
        ],
        out_specs=(pl.BlockSpec((1, TM, 1024), lambda b, i: (b, i, 0)),
                   pl.BlockSpec((1, 2048, TM), lambda b, i: (b, 0, i))),
        compiler_params=_cparams(2),
        name="inproj_even",
    )(h, modv, wtok, wfeat, cosf, sinf, cost, sina, sinb)


def _block_diag2(q_t):
    n = q_t.shape[1]
    z = jnp.zeros((64, n), q_t.dtype)
    left = jnp.concatenate([q_t[:64], z], axis=0)
    right = jnp.concatenate([z, q_t[64:]], axis=0)
    return jnp.concatenate([left, right], axis=1)


ONES_ROWS = 16


def _flash(rhs, k_ref, vt_ref, chunks):
    width = rhs.shape[1]
    dv = vt_ref.shape[1]
    m = jnp.full((1, width), -jnp.inf, F32)
    acc = jnp.zeros((dv + ONES_ROWS, width), F32)

    def scores(idx):
        st, sz = chunks[idx]
        return jnp.dot(k_ref[0, st:st + sz, :], rhs, preferred_element_type=F32)

    s_next = scores(0)
    for idx, (st, sz) in enumerate(chunks):
        s = s_next
        if idx + 1 < len(chunks):
            s_next = scores(idx + 1)
        m_new = jnp.maximum(m, jnp.max(s, axis=0, keepdims=True))
        alpha = jnp.exp2(m - m_new)
        p = jnp.exp2(s - m_new).astype(BF16)
        vt1 = jnp.concatenate([vt_ref[0, :, st:st + sz], jnp.ones((ONES_ROWS, sz), BF16)], axis=0)
        acc = acc * alpha + jnp.dot(vt1, p, preferred_element_type=F32)
        m = m_new
    return acc[:dv], acc[dv:dv + 1]


def _latent_chunks(tk):
    return [(j * tk, tk) for j in range(S // tk)] + [(S, L)]


def _diff_attn_kernel(lam_ref, g_ref, q_ref, k_ref, vt_ref, o_ref, *, chunks):
    tq = q_ref.shape[2]
    acc, l = _flash(_block_diag2(q_ref[0]), k_ref, vt_ref, chunks)
    lf = lam_ref[...]
    lam = (jnp.exp(jnp.sum(lf[0:1] * lf[1:2], axis=1, keepdims=True))
           - jnp.exp(jnp.sum(lf[2:3] * lf[3:4], axis=1, keepdims=True)) + LAMBDA_INIT_0)
    o = acc[:, :tq] / l[:, :tq] - lam * (acc[:, tq:] / l[:, tq:])
    ms = jnp.mean(o * o, axis=0, keepdims=True)
    o = o * lax.rsqrt(ms + RMS_EPS) * g_ref[...] * (1.0 - LAMBDA_INIT_0)
    o_ref[0] = o.astype(BF16)


def _diff_attention(tok, feat, diff_lambda, subln_g, context):
    if context:
        tq, nq, chunks = L, 1, [(0, L)]
        q_spec = pl.BlockSpec((1, 128, L), lambda b, h, i: (b, h, S // L))
        k_spec = pl.BlockSpec((1, L, 128), lambda b, h, i: (b, S // L, h))
        vt_spec = pl.BlockSpec((1, 128, L), lambda b, h, i: (b, 4 + h, S // L))
    else:
        tq, nq, chunks = TQ_DIFF, S // TQ_DIFF, _latent_chunks(TK_DIFF)
        q_spec = pl.BlockSpec((1, 128, tq), lambda b, h, i: (b, h, i))
        k_spec = pl.BlockSpec((1, N, 128), lambda b, h, i: (b, 0, h))
        vt_spec = pl.BlockSpec((1, 128, N), lambda b, h, i: (b, 4 + h, 0))
    return pl.pallas_call(
        functools.partial(_diff_attn_kernel, chunks=chunks),
        out_shape=jax.ShapeDtypeStruct((B, 512, nq * tq), BF16),
        grid=(B, DIFF_HEADS, nq),
        in_specs=[
            pl.BlockSpec((4, HEAD_DIM), lambda b, h, i: (0, 0)),
            pl.BlockSpec((DIFF_V, 1), lambda b, h, i: (0, 0)),
            q_spec, k_spec, vt_spec,
        ],
        out_specs=pl.BlockSpec((1, 128, tq), lambda b, h, i: (b, h, i)),
        compiler_params=_cparams(3),
        name="diff_attention_ctx" if context else "diff_attention",
    )(diff_lambda, subln_g.reshape(DIFF_V, 1), feat, tok, feat)


def _na_attend(rhs, k_ctx, vt_ctx, k_win=None, vt_win=None, bias=None):
    s_c = jnp.dot(k_ctx, rhs, preferred_element_type=F32)
    m = jnp.max(s_c, axis=0, keepdims=True)
    if k_win is not None:
        s_w = jnp.dot(k_win, rhs, preferred_element_type=F32) + bias
        m = jnp.maximum(m, jnp.max(s_w, axis=0, keepdims=True))
    p_c = jnp.exp2(s_c - m)
    l = jnp.sum(p_c, axis=0, keepdims=True)
    o = jnp.dot(vt_ctx, p_c.astype(BF16), preferred_element_type=F32)
    if k_win is not None:
        p_w = jnp.exp2(s_w - m)
        l = l + jnp.sum(p_w, axis=0, keepdims=True)
        o = o + jnp.dot(vt_win, p_w.astype(BF16), preferred_element_type=F32)
    o = o / l
    n = rhs.shape[1] // 2
    return jnp.concatenate([o[:64, :n], o[64:, n:]], axis=0).astype(BF16)


def _na_kernel(q_ref, k_ref, vt_ref, bias_ref, o_ref):
    i = pl.program_id(2)
    k_ctx = k_ref[0, S:, :]
    vt_ctx = vt_ref[0, :, S:]
    for jj in range(TQ_NA // LANE):
        r = 2 * (i * (TQ_NA // LANE) + jj)
        start = jnp.clip(r - NA_ROWS // 2, 0, ROWS - NA_WIN_ROWS)
        pat = jnp.where(r == 0, 0, jnp.where(r == 2, 1, jnp.where(
            r == ROWS - 4, 3, jnp.where(r == ROWS - 2, 4, 2))))
        off = pl.multiple_of(start * GRID_W, LANE)
        rhs = _block_diag2(q_ref[0, :, jj * LANE:(jj + 1) * LANE])
        o_ref[0, :, jj * LANE:(jj + 1) * LANE] = _na_attend(
            rhs, k_ctx, vt_ctx, k_ref[0, pl.ds(off, NA_WIN), :], vt_ref[0, :, pl.ds(off, NA_WIN)],
            bias_ref[0, pat])


def _na_ctx_kernel(q_ref, k_ref, vt_ref, o_ref):
    o_ref[0] = _na_attend(_block_diag2(q_ref[0]), k_ref[0], vt_ref[0])


def _na_attention(tok, feat, bias):
    return pl.pallas_call(
        _na_kernel,
        out_shape=jax.ShapeDtypeStruct((B, 512, S), BF16),
        grid=(B, NA_HEADS // 2, S // TQ_NA),
        in_specs=[
            pl.BlockSpec((1, 128, TQ_NA), lambda b, h, i: (b, 8 + h, i)),
            pl.BlockSpec((1, N, 128), lambda b, h, i: (b, 0, 4 + h)),
            pl.BlockSpec((1, 128, N), lambda b, h, i: (b, 12 + h, 0)),
            pl.BlockSpec((1, NA_PATTERNS, NA_WIN, 2 * LANE), lambda b, h, i: (h, 0, 0, 0)),
        ],
        out_specs=pl.BlockSpec((1, 128, TQ_NA), lambda b, h, i: (b, h, i)),
        compiler_params=_cparams(3),
        name="na_attention",
    )(feat, tok, feat, bias)


def _na_attention_ctx(tok, feat):
    return pl.pallas_call(
        _na_ctx_kernel,
        out_shape=jax.ShapeDtypeStruct((B, 512, L), BF16),
        grid=(B, NA_HEADS // 2),
        in_specs=[
            pl.BlockSpec((1, 128, L), lambda b, h: (b, 8 + h, S // L)),
            pl.BlockSpec((1, L, 128), lambda b, h: (b, S // L, 4 + h)),
            pl.BlockSpec((1, 128, L), lambda b, h: (b, 12 + h, S // L)),
        ],
        out_specs=pl.BlockSpec((1, 128, L), lambda b, h: (b, h, 0)),
        compiler_params=_cparams(2),
        name="na_attention_ctx",
    )(feat, tok, feat)


def _na_bias_table(rpb):
    pats = [(0, 0), (2, 0), (8, 4), (ROWS - 4, ROWS - NA_WIN_ROWS), (ROWS - 2, ROWS - NA_WIN_ROWS)]
    kr_rel = np.arange(NA_WIN_ROWS)
    j = np.arange(2)
    drow = np.zeros((NA_PATTERNS, NA_WIN_ROWS, 2), np.int32)
    row_ok = np.zeros((NA_PATTERNS, NA_WIN_ROWS, 2), bool)
    for p, (r, start) in enumerate(pats):
        rq = r + j
        rs = np.clip(rq - NA_ROWS // 2, 0, ROWS - NA_ROWS)
        kr = start + kr_rel
        row_ok[p] = (kr[:, None] >= rs[None, :]) & (kr[:, None] < rs[None, :] + NA_ROWS)
        drow[p] = np.clip(kr[:, None] - rq[None, :] + NA_ROWS - 1, 0, 2 * NA_ROWS - 2)
    kc = np.arange(GRID_W)
    c = np.arange(GRID_W)
    cs = np.clip(c - NA_COLS // 2, 0, GRID_W - NA_COLS)
    col_ok = (kc[:, None] >= cs[None, :]) & (kc[:, None] < cs[None, :] + NA_COLS)
    dcol = kc[:, None] - c[None, :] + NA_COLS - 1
    onehot = (dcol[None] == np.arange(2 * NA_COLS - 1)[:, None, None]) & col_ok[None]
    rows = rpb[:, drow.reshape(-1), :]
    t = jnp.einsum("hrd,dkc->hrkc", rows, jnp.asarray(onehot, F32), precision=lax.Precision.HIGHEST)
    valid = row_ok.reshape(-1)[None, :, None, None] & col_ok[None, None]
    t = jnp.where(jnp.asarray(valid), t, NEG)
    t = t.reshape(NA_HEADS // 2, 2, NA_PATTERNS, NA_WIN_ROWS, 2, GRID_W, GRID_W)
    t = t.transpose(0, 2, 3, 5, 1, 4, 6)
    return t.reshape(NA_HEADS // 2, NA_PATTERNS, NA_WIN, 2 * LANE) * LOG2E


def _outproj_kernel(a_ref, b_ref, w_ref, h_ref, mv_ref, g_ref, beta_ref, o_ref, *, b_token_major):
    ta = (((0,), (0,)), ((), ()))
    y = lax.dot_general(a_ref[0], w_ref[:512, :], ta, preferred_element_type=F32)
    if b_token_major:
        y = y + jnp.dot(b_ref[0], w_ref[512:, :], preferred_element_type=F32)
    else:
        y = y + lax.dot_general(b_ref[0], w_ref[512:, :], ta, preferred_element_type=F32)
    gate = mv_ref[0][2:3]
    o_ref[0] = _layer_norm(ALPHA * h_ref[0] + gate * y, g_ref[...], beta_ref[...])


def _outproj(mix_a, mix_b, w, h, modv, g, beta, n_tiles, b_token_major):
    if b_token_major:
        b_spec = pl.BlockSpec((1, TM, 512), lambda b, i: (b, i, 0))
    else:
        b_spec = pl.BlockSpec((1, 512, TM), lambda b, i: (b, 0, i))
    return pl.pallas_call(
        functools.partial(_outproj_kernel, b_token_major=b_token_major),
        out_shape=jax.ShapeDtypeStruct((B, n_tiles * TM, D), F32),
        grid=(B, n_tiles),
        in_specs=[
            pl.BlockSpec((1, 512, TM), lambda b, i: (b, 0, i)),
            b_spec,
            pl.BlockSpec((D, D), lambda b, i: (0, 0)),
            pl.BlockSpec((1, TM, D), lambda b, i: (b, i, 0)),
            pl.BlockSpec((1, N_MOD, D), lambda b, i: (_mod_row(b, i), 0, 0)),
            pl.BlockSpec((1, D), lambda b, i: (0, 0)),
            pl.BlockSpec((1, D), lambda b, i: (0, 0)),
        ],
        out_specs=pl.BlockSpec((1, TM, D), lambda b, i: (b, i, 0)),
        compiler_params=_cparams(2),
        name="outproj_ln",
    )(mix_a, mix_b, w, h, modv, g.reshape(1, D), beta.reshape(1, D))


def _ffn_kernel(h_ref, mv_ref, win_ref, wout_ref, g_ref, beta_ref, o_ref):
    u = _modulated(h_ref, mv_ref, 3)
    ga = jnp.dot(u, win_ref[...], preferred_element_type=F32)
    gt, a = ga[:, :FFN_HIDDEN], ga[:, FFN_HIDDEN:]
    act = (gt * (1.0 / (1.0 + jnp.exp(-gt))) * a).astype(BF16)
    y = jnp.dot(act, wout_ref[...], preferred_element_type=F32)
    gate = mv_ref[0][5:6]
    o_ref[0] = _layer_norm(ALPHA * h_ref[0] + gate * y, g_ref[...], beta_ref[...])


def _ffn(h, modv, w_in, w_out, g, beta, n_tiles):
    return pl.pallas_call(
        _ffn_kernel,
        out_shape=jax.ShapeDtypeStruct((B, n_tiles * TM, D), F32),
        grid=(B, n_tiles),
        in_specs=[
            pl.BlockSpec((1, TM, D), lambda b, i: (b, i, 0)),
            pl.BlockSpec((1, N_MOD, D), lambda b, i: (_mod_row(b, i), 0, 0)),
            pl.BlockSpec((D, 2 * FFN_HIDDEN), lambda b, i: (0, 0)),
            pl.BlockSpec((FFN_HIDDEN, D), lambda b, i: (0, 0)),
            pl.BlockSpec((1, D), lambda b, i: (0, 0)),
            pl.BlockSpec((1, D), lambda b, i: (0, 0)),
        ],
        out_specs=pl.BlockSpec((1, TM, D), lambda b, i: (b, i, 0)),
        compiler_params=_cparams(2),
        name="ffn_ln",
    )(h, modv, w_in, w_out, g.reshape(1, D), beta.reshape(1, D))


def _gelu(x):
    return 0.5 * x * (1.0 + lax.erf(x * (1.0 / math.sqrt(2.0))))


def _rms(x, g):
    ms = jnp.mean(x * x, axis=-1, keepdims=True)
    return x * lax.rsqrt(ms + RMS_EPS) * g


def _inproj_odd_kernel(h_ref, mv_ref, win_ref, gq_ref, gkv_ref, wuq_ref, wkn_ref, wkr_ref, wv_ref,
                       cosf_ref, sinf_ref, cost_ref, sina_ref, sinb_ref,
                       lng_ref, lnb_ref, ws_ref, gb_ref,
                       q_ref, k_ref, vt_ref, gm_ref):
    nt = (((1,), (1,)), ((), ()))
    u = _modulated(h_ref, mv_ref, 0)
    p = jnp.dot(u, win_ref[...], preferred_element_type=F32)
    cq = _rms(p[:, :MLA_Q_RANK], gq_ref[...]).astype(BF16)
    ckv = _rms(p[:, MLA_Q_RANK:MLA_Q_RANK + MLA_KV_RANK], gkv_ref[...]).astype(BF16)
    kr = p[:, 640:768]
    qt = lax.dot_general(wuq_ref[...], cq, nt, preferred_element_type=F32)
    qt = qt * ((MLA_NOPE + MLA_ROPE) ** -0.5 * LOG2E)
    cf, sf = cosf_ref[...], sinf_ref[...]
    for hd in range(MLA_HEADS):
        base = hd * 128
        parts = [qt[base:base + 64]]
        for half in range(2):
            x1 = qt[base + 64 + half * 16:base + 72 + half * 16]
            x2 = qt[base + 72 + half * 16:base + 80 + half * 16]
            c = cf[half * 8:(half + 1) * 8]
            s = sf[half * 8:(half + 1) * 8]
            parts += [x1 * c - x2 * s, x2 * c + x1 * s]
        parts.append(qt[base + 96:base + 128])
        q_ref[0, base:base + 128, :] = jnp.concatenate(parts, axis=0).astype(BF16)
    kr = kr * cost_ref[...] + pltpu.roll(kr, LANE - 8, 1) * sina_ref[...] + pltpu.roll(kr, 8, 1) * sinb_ref[...]
    k = (jnp.dot(ckv, wkn_ref[...], preferred_element_type=F32)
         + jnp.dot(kr.astype(BF16), wkr_ref[...], preferred_element_type=F32))
    k_ref[0] = k.astype(BF16)
    vt_ref[0] = lax.dot_general(wv_ref[...], ckv, nt, preferred_element_type=F32).astype(BF16)
    gu = _gelu(p[:, 768:1280])
    gv = _layer_norm(_gelu(p[:, 1280:1792]), lng_ref[...], lnb_ref[...]).astype(BF16)
    for ch in range(TM // GMLP_CHUNK):
        rows = slice(ch * GMLP_CHUNK, (ch + 1) * GMLP_CHUNK)
        for grp in range(GMLP_GROUPS):
            cols = slice(grp * GMLP_CH, (grp + 1) * GMLP_CH)
            mixed = jnp.dot(ws_ref[grp], gv[rows, cols], preferred_element_type=F32) + gb_ref[grp]
            gm_ref[0, rows, cols] = (gu[rows, cols] * mixed).astype(BF16)


def _inproj_odd(h, modv, w, tabs):
    cosf, sinf, cost, sina, sinb = tabs
    const2 = lambda b, i: (0, 0)
    const3 = lambda b, i: (0, 0, 0)
    return pl.pallas_call(
        _inproj_odd_kernel,
        out_shape=(jax.ShapeDtypeStruct((B, 1024, N), BF16),
                   jax.ShapeDtypeStruct((B, N, 1024), BF16),
                   jax.ShapeDtypeStruct((B, 512, N), BF16),
                   jax.ShapeDtypeStruct((B, N, 512), BF16)),
        grid=(B, NT_ALL),
        in_specs=[
            pl.BlockSpec((1, TM, D), lambda b, i: (b, i, 0)),
            pl.BlockSpec((1, N_MOD, D), lambda b, i: (_mod_row(b, i), 0, 0)),
            pl.BlockSpec((D, 1792), const2),
            pl.BlockSpec((1, MLA_Q_RANK), const2),
            pl.BlockSpec((1, MLA_KV_RANK), const2),
            pl.BlockSpec((1024, MLA_Q_RANK), const2),
            pl.BlockSpec((MLA_KV_RANK, 1024), const2),
            pl.BlockSpec((LANE, 1024), const2),
            pl.BlockSpec((512, MLA_KV_RANK), const2),
            pl.BlockSpec((16, TM), lambda b, i: (0, i)),
            pl.BlockSpec((16, TM), lambda b, i: (0, i)),
            pl.BlockSpec((TM, LANE), lambda b, i: (i, 0)),
            pl.BlockSpec((TM, LANE), lambda b, i: (i, 0)),
            pl.BlockSpec((TM, LANE), lambda b, i: (i, 0)),
            pl.BlockSpec((1, GMLP_WIDTH), const2),
            pl.BlockSpec((1, GMLP_WIDTH), const2),
            pl.BlockSpec((GMLP_GROUPS, GMLP_CHUNK, GMLP_CHUNK), const3),
            pl.BlockSpec((GMLP_GROUPS, GMLP_CHUNK, 1), const3),
        ],
        out_specs=(pl.BlockSpec((1, 1024, TM), lambda b, i: (b, 0, i)),
                   pl.BlockSpec((1, TM, 1024), lambda b, i: (b, i, 0)),
                   pl.BlockSpec((1, 512, TM), lambda b, i: (b, 0, i)),
                   pl.BlockSpec((1, TM, 512), lambda b, i: (b, i, 0))),
        compiler_params=_cparams(2),
        name="inproj_odd",
    )(h, modv, w["win"], w["gq"], w["gkv"], w["wuq"], w["wkn"], w["wkr"], w["wv"],
      cosf, sinf, cost, sina, sinb, w["lng"], w["lnb"], w["ws"], w["gb"])


def _mla_attn_kernel(q_ref, k_ref, vt_ref, o_ref):
    acc, l = _flash(q_ref[0], k_ref, vt_ref, _latent_chunks(TK_MLA))
    o_ref[0] = (acc / l).astype(BF16)


def _mla_attention(q_t, k, v_t):
    return pl.pallas_call(
        _mla_attn_kernel,
        out_shape=jax.ShapeDtypeStruct((B, 512, S), BF16),
        grid=(B, MLA_HEADS, S // TQ_MLA),
        in_specs=[
            pl.BlockSpec((1, 128, TQ_MLA), lambda b, h, i: (b, h, i)),
            pl.BlockSpec((1, N, 128), lambda b, h, i: (b, 0, h)),
            pl.BlockSpec((1, MLA_V, N), lambda b, h, i: (b, h, 0)),
        ],
        out_specs=pl.BlockSpec((1, MLA_V, TQ_MLA), lambda b, h, i: (b, h, i)),
        compiler_params=_cparams(3),
        name="mla_attention",
    )(q_t, k, v_t)


def _rope_tables(n_freq):
    t = jnp.arange(S)
    inv = jnp.power(ROPE_BASE, -jnp.arange(0, 2 * n_freq, 2, dtype=F32) / (2 * n_freq))
    ang_r = (t // GRID_W).astype(F32)[:, None] * inv[None, :]
    ang_c = (t % GRID_W).astype(F32)[:, None] * inv[None, :]
    cos = jnp.concatenate([jnp.cos(ang_r), jnp.cos(ang_c)], axis=1)
    sin = jnp.concatenate([jnp.sin(ang_r), jnp.sin(ang_c)], axis=1)
    cos = jnp.concatenate([cos, jnp.ones((L, 2 * n_freq), F32)], axis=0)
    sin = jnp.concatenate([sin, jnp.zeros((L, 2 * n_freq), F32)], axis=0)
    lane = np.arange(LANE)
    period = 4 * n_freq
    src = ((lane % period) // (2 * n_freq)) * n_freq + lane % n_freq
    is_x1 = (lane % (2 * n_freq)) < n_freq
    cos_t = cos[:, src]
    sin_t = sin[:, src]
    sin_a = jnp.where(jnp.asarray(is_x1)[None, :], -sin_t, 0.0)
    sin_b = jnp.where(jnp.asarray(is_x1)[None, :], 0.0, sin_t)
    return cos.T, sin.T, cos_t, sin_a, sin_b


def _even_weights(ev_w_in):
    aq, ak, av, bq, bk, bv = jnp.split(ev_w_in, [512, 1024, 1536, 2048, 2560], axis=1)
    scale = HEAD_DIM ** -0.5
    wtok = jnp.concatenate([ak, bk], axis=1).astype(BF16)
    wfeat = jnp.concatenate([aq * scale, av, bq * scale, bv], axis=1).T.astype(BF16)
    return wtok, wfeat


def _odd_weights(od_w_in, gq, w_uq, gkv, w_ukv, ln_g, ln_b, ws, gb):
    cq, ckv, kr, gu, gv = jnp.split(od_w_in, [384, 640, 672, 1184], axis=1)
    win = jnp.concatenate([cq, ckv, kr, jnp.zeros((D, LANE - MLA_ROPE), F32), gu, gv], axis=1)
    dq = MLA_NOPE + MLA_ROPE
    wuq = w_uq.reshape(MLA_Q_RANK, MLA_HEADS, dq)
    wuq = jnp.pad(wuq, ((0, 0), (0, 0), (0, LANE - dq))).reshape(MLA_Q_RANK, MLA_HEADS * LANE)
    wukv = w_ukv.reshape(MLA_KV_RANK, MLA_HEADS, MLA_NOPE + MLA_V)
    wkn = jnp.pad(wukv[:, :, :MLA_NOPE], ((0, 0), (0, 0), (0, LANE - MLA_NOPE)))
    wkn = wkn.reshape(MLA_KV_RANK, MLA_HEADS * LANE)
    wv = wukv[:, :, MLA_NOPE:].reshape(MLA_KV_RANK, MLA_HEADS * MLA_V)
    place = np.zeros((LANE, MLA_HEADS * LANE), np.float32)
    for hd in range(MLA_HEADS):
        place[np.arange(MLA_ROPE), hd * LANE + MLA_NOPE + np.arange(MLA_ROPE)] = 1.0
    return {
        "win": win.astype(BF16), "gq": gq.reshape(1, -1), "gkv": gkv.reshape(1, -1),
        "wuq": wuq.T.astype(BF16), "wkn": wkn.astype(BF16), "wkr": jnp.asarray(place, BF16),
        "wv": wv.T.astype(BF16), "lng": ln_g.reshape(1, -1), "lnb": ln_b.reshape(1, -1),
        "ws": ws.astype(BF16), "gb": gb.reshape(GMLP_GROUPS, GMLP_CHUNK, 1),
    }


def kernel(x, c, ctx, c_ctx, mod_w, mod_b, ln_mix_g, ln_mix_b, ln_ffn_g, ln_ffn_b, ffn_w_in, ffn_w_out,
           ev_w_in, ev_w_out, diff_lambda, diff_subln_g, na_rpb, od_w_in, od_w_out, mla_q_norm_g,
           mla_w_uq, mla_kv_norm_g, mla_w_ukv, gmlp_ln_g, gmlp_ln_b, gmlp_ws, gmlp_b):
    h = jnp.concatenate([x, ctx], axis=1)
    cond = jnp.concatenate([c, c_ctx[None], jnp.zeros((8 - B - 1, D), F32)], axis=0)
    mod = _modulation(cond, mod_w, mod_b).reshape(DEPTH, 8, N_MOD, D)

    wtok, wfeat = _even_weights(ev_w_in[0])
    cosf, sinf, cost, sina, sinb = _rope_tables(16)
    tok, feat = _inproj_even(h, mod[0], wtok, wfeat, (cosf * LOG2E, sinf * LOG2E, cost, sina, sinb))
    mix_a = jnp.concatenate([_diff_attention(tok, feat, diff_lambda[0], diff_subln_g[0], False),
                             _diff_attention(tok, feat, diff_lambda[0], diff_subln_g[0], True)], axis=2)
    mix_b = jnp.concatenate([_na_attention(tok, feat, _na_bias_table(na_rpb[0])),
                             _na_attention_ctx(tok, feat)], axis=2)
    h = _outproj(mix_a, mix_b, ev_w_out[0].astype(BF16), h, mod[0], ln_mix_g[0], ln_mix_b[0],
                 NT_ALL, False)
    h = _ffn(h, mod[0], ffn_w_in[0].astype(BF16), ffn_w_out[0].astype(BF16),
             ln_ffn_g[0], ln_ffn_b[0], NT_ALL)

    w1 = _odd_weights(od_w_in[0], mla_q_norm_g[0], mla_w_uq[0], mla_kv_norm_g[0], mla_w_ukv[0],
                      gmlp_ln_g[0], gmlp_ln_b[0], gmlp_ws[0], gmlp_b[0])
    q_t, k, v_t, gm = _inproj_odd(h, mod[1], w1, _rope_tables(8))
    mix_a = _mla_attention(q_t, k, v_t)
    h = _outproj(mix_a, gm, od_w_out[0].astype(BF16), h, mod[1], ln_mix_g[1], ln_mix_b[1],
                 NT_LAT, True)
    return _ffn(h, mod[1], ffn_w_in[1].astype(BF16), ffn_w_out[1].astype(BF16),
                ln_ffn_g[1], ln_ffn_b[1], NT_LAT)
```

```python
import functools
import math

import jax
import jax.numpy as jnp
import numpy as np
from jax import lax
from jax.experimental import pallas as pl
from jax.experimental.pallas import tpu as pltpu

F32 = jnp.float32
BF16 = jnp.bfloat16

D = 1024
B = 2
S = 8192
L = 256
N = S + L
DEPTH = 2
GRID_W = 64
ROWS = S // GRID_W
ALPHA = (2 * DEPTH) ** 0.25
ROPE_BASE = 10000.0
LN_EPS = 1e-5
RMS_EPS = 1e-6
N_MOD = 6
HEAD_DIM = 64
DIFF_HEADS = 4
DIFF_V = 128
NA_HEADS = 8
NA_ROWS = 8
NA_COLS = 16
MLA_HEADS = 8
MLA_Q_RANK = 384
MLA_KV_RANK = 256
MLA_NOPE = 64
MLA_ROPE = 32
MLA_V = 64
GMLP_GROUPS = 4
GMLP_CH = 128
GMLP_CHUNK = 128
GMLP_WIDTH = 512
FFN_HIDDEN = 2816
LAMBDA_INIT_0 = 0.8 - 0.6 * math.exp(-0.3 * 0)
LOG2E = math.log2(math.e)

LANE = 128
TM = 256
NT_ALL = N // TM
NT_LAT = S // TM
TQ_DIFF, TK_DIFF = 512, 256
TQ_MLA, TK_MLA = 1024, 256
TQ_NA = 1024
NA_WIN_ROWS = 10
NA_WIN = NA_WIN_ROWS * GRID_W
NA_PATTERNS = 5
NEG = -1e30
VMEM_LIMIT = 56 * 1024 * 1024


def _cparams(n_axes):
    return pltpu.CompilerParams(dimension_semantics=("parallel",) * n_axes,
                                vmem_limit_bytes=VMEM_LIMIT)


def _mod_row(b, i):
    return jnp.where(i < NT_LAT, b, 2)


def _mod_kernel(cond_ref, w_ref, b_ref, o_ref):
    cnd = cond_ref[...]
    act = cnd * (1.0 / (1.0 + jnp.exp(-cnd)))
    o_ref[0] = jnp.dot(act, w_ref[0], preferred_element_type=F32) + b_ref[0]


def _modulation(cond, mod_w, mod_b):
    tn = 1536
    return pl.pallas_call(
        _mod_kernel,
        out_shape=jax.ShapeDtypeStruct((DEPTH, 8, N_MOD * D), F32),
        grid=(DEPTH, N_MOD * D // tn),
        in_specs=[
            pl.BlockSpec((8, D), lambda l, j: (0, 0)),
            pl.BlockSpec((1, D, tn), lambda l, j: (l, 0, j)),
            pl.BlockSpec((1, 1, tn), lambda l, j: (l, 0, j)),
        ],
        out_specs=pl.BlockSpec((1, 8, tn), lambda l, j: (l, 0, j)),
        compiler_params=_cparams(2),
        name="modulation",
    )(cond, mod_w, mod_b.reshape(DEPTH, 1, N_MOD * D))


def _modulated(h_ref, mv_ref, k):
    mv = mv_ref[0]
    return (h_ref[0] * (1.0 + mv[k + 1:k + 2]) + mv[k:k + 1]).astype(BF16)


def _layer_norm(x, g, b):
    mu = jnp.mean(x, axis=-1, keepdims=True)
    xc = x - mu
    var = jnp.mean(xc * xc, axis=-1, keepdims=True)
    return xc * lax.rsqrt(var + LN_EPS) * g + b


def _inproj_even_kernel(h_ref, mv_ref, wtok_ref, wfeat_ref, cosf_ref, sinf_ref,
                        cost_ref, sina_ref, sinb_ref, tok_ref, feat_ref):
    u = _modulated(h_ref, mv_ref, 0)
    tok = jnp.dot(u, wtok_ref[...], preferred_element_type=F32)
    ct, sa, sb = cost_ref[...], sina_ref[...], sinb_ref[...]
    for j in range(4):
        xs = tok[:, j * LANE:(j + 1) * LANE]
        ro = xs * ct + pltpu.roll(xs, LANE - 16, 1) * sa + pltpu.roll(xs, 16, 1) * sb
        tok_ref[0, :, j * LANE:(j + 1) * LANE] = ro.astype(BF16)
    tok_ref[0, :, 512:] = tok[:, 512:].astype(BF16)
    feat = lax.dot_general(wfeat_ref[...], u, (((1,), (1,)), ((), ())),
                           preferred_element_type=F32)
    cf, sf = cosf_ref[...], sinf_ref[...]
    for blk in range(8):
        for half in range(2):
            base = blk * 64 + half * 32
            x1 = feat[base:base + 16]
            x2 = feat[base + 16:base + 32]
            c = cf[half * 16:(half + 1) * 16]
            s = sf[half * 16:(half + 1) * 16]
            feat_ref[0, base:base + 16, :] = (x1 * c - x2 * s).astype(BF16)
            feat_ref[0, base + 16:base + 32, :] = (x2 * c + x1 * s).astype(BF16)
    feat_ref[0, 512:1024, :] = feat[512:1024].astype(BF16)
    feat_ref[0, 1024:1536, :] = (feat[1024:1536] * LOG2E).astype(BF16)
    feat_ref[0, 1536:, :] = feat[1536:].astype(BF16)


def _inproj_even(h, modv, wtok, wfeat, tabs):
    cosf, sinf, cost, sina, sinb = tabs
    const = lambda b, i: (0, 0)
    return pl.pallas_call(
        _inproj_even_kernel,
        out_shape=(jax.ShapeDtypeStruct((B, N, 1024), BF16),
                   jax.ShapeDtypeStruct((B, 2048, N), BF16)),
        grid=(B, NT_ALL),
        in_specs=[
            pl.BlockSpec((1, TM, D), lambda b, i: (b, i, 0)),
            pl.BlockSpec((1, N_MOD, D), lambda b, i: (_mod_row(b, i), 0, 0)),
            pl.BlockSpec((D, 1024), const),
            pl.BlockSpec((2048, D), const),
            pl.BlockSpec((32, TM), lambda b, i: (0, i)),
            pl.BlockSpec((32, TM), lambda b, i: (0, i)),
            pl.BlockSpec((TM, LANE), lambda b, i: (i, 0)),
            pl.BlockSpec((TM, LANE), lambda b, i: (i, 0)),
            pl.BlockSpec((TM, LANE), lambda b, i: (i, 0)),
        ],
        out_specs=(pl.BlockSpec((1, TM, 1024), lambda b, i: (b, i, 0)),
                   pl.BlockSpec((1, 2048, TM), lambda b, i: (b, 0, i))),
        compiler_params=_cparams(2),
        name="inproj_even",
    )(h, modv, wtok, wfeat, cosf, sinf, cost, sina, sinb)


def _block_diag2(q_t):
    n = q_t.shape[1]
    z = jnp.zeros((64, n), q_t.dtype)
    left = jnp.concatenate([q_t[:64], z], axis=0)
    right = jnp.concatenate([z, q_t[64:]], axis=0)
    return jnp.concatenate([left, right], axis=1)


ONES_ROWS = 16


def _flash(rhs, k_ref, vt_ref, chunks):
    width = rhs.shape[1]
    dv = vt_ref.shape[1]
    m = jnp.full((1, width), -jnp.inf, F32)
    acc = jnp.zeros((dv + ONES_ROWS, width), F32)

    def scores(idx):
        st, sz = chunks[idx]
        return jnp.dot(k_ref[0, st:st + sz, :], rhs, preferred_element_type=F32)

    s_next = scores(0)
    for idx, (st, sz) in enumerate(chunks):
        s = s_next
        if idx + 1 < len(chunks):
            s_next = scores(idx + 1)
        m_new = jnp.maximum(m, jnp.max(s, axis=0, keepdims=True))
        alpha = jnp.exp2(m - m_new)
        p = jnp.exp2(s - m_new).astype(BF16)
        vt1 = jnp.concatenate([vt_ref[0, :, st:st + sz], jnp.ones((ONES_ROWS, sz), BF16)], axis=0)
        acc = acc * alpha + jnp.dot(vt1, p, preferred_element_type=F32)
        m = m_new
    return acc[:dv], acc[dv:dv + 1]


SCORES_AHEAD = 2
P_LIMIT = 2.0 ** 64


def _flash_fixed_ref(rhs, k_ref, vt_ref, chunks):
    width = rhs.shape[1]
    dv = vt_ref.shape[1]

    def scores(idx):
        st, sz = chunks[idx]
        return jnp.dot(k_ref[0, st:st + sz, :], rhs, preferred_element_type=F32)

    def values(idx):
        st, sz = chunks[idx]
        return jnp.concatenate([vt_ref[0, :, st:st + sz], jnp.ones((ONES_ROWS, sz), BF16)], axis=0)

    n = len(chunks)
    pending = [scores(i) for i in range(min(SCORES_AHEAD, n))]
    s = pending.pop(0)
    m_ref = jnp.max(s, axis=0, keepdims=True)
    if n > SCORES_AHEAD:
        pending.append(scores(SCORES_AHEAD))
    acc = jnp.dot(values(0), jnp.exp2(s - m_ref).astype(BF16), preferred_element_type=F32)
    p_max = jnp.zeros((8, width), F32)
    for idx in range(1, n):
        s = pending.pop(0)
        if idx + SCORES_AHEAD < n:
            pending.append(scores(idx + SCORES_AHEAD))
        p = jnp.exp2(s - m_ref)
        p_max = jnp.maximum(p_max, jnp.max(p.reshape(-1, 8, width), axis=0))
        acc = acc + jnp.dot(values(idx), p.astype(BF16), preferred_element_type=F32)
    return acc[:dv], acc[dv:dv + 1], jnp.max(p_max) > P_LIMIT


def _latent_chunks(tk):
    return [(j * tk, tk) for j in range(S // tk)] + [(S, L)]


def _diff_attn_kernel(lam_ref, g_ref, q_ref, k_ref, vt_ref, o_ref, *, chunks):
    tq = q_ref.shape[2]
    rhs = _block_diag2(q_ref[0])

    def finish(acc, l):
        lf = lam_ref[...]
        lam = (jnp.exp(jnp.sum(lf[0:1] * lf[1:2], axis=1, keepdims=True))
               - jnp.exp(jnp.sum(lf[2:3] * lf[3:4], axis=1, keepdims=True)) + LAMBDA_INIT_0)
        o = acc[:, :tq] / l[:, :tq] - lam * (acc[:, tq:] / l[:, tq:])
        ms = jnp.mean(o * o, axis=0, keepdims=True)
        o = o * lax.rsqrt(ms + RMS_EPS) * g_ref[...] * (1.0 - LAMBDA_INIT_0)
        o_ref[0] = o.astype(BF16)

    acc, l, overflow = _flash_fixed_ref(rhs, k_ref, vt_ref, chunks)
    finish(acc, l)

    @pl.when(overflow)
    def _recompute():
        finish(*_flash(rhs, k_ref, vt_ref, chunks))


def _diff_attention(tok, feat, diff_lambda, subln_g, context):
    if context:
        tq, nq, chunks = L, 1, [(0, L)]
        q_spec = pl.BlockSpec((1, 128, L), lambda b, h, i: (b, h, S // L))
        k_spec = pl.BlockSpec((1, L, 128), lambda b, h, i: (b, S // L, h))
        vt_spec = pl.BlockSpec((1, 128, L), lambda b, h, i: (b, 4 + h, S // L))
    else:
        tq, nq, chunks = TQ_DIFF, S // TQ_DIFF, _latent_chunks(TK_DIFF)
        q_spec = pl.BlockSpec((1, 128, tq), lambda b, h, i: (b, h, i))
        k_spec = pl.BlockSpec((1, N, 128), lambda b, h, i: (b, 0, h))
        vt_spec = pl.BlockSpec((1, 128, N), lambda b, h, i: (b, 4 + h, 0))
    return pl.pallas_call(
        functools.partial(_diff_attn_kernel, chunks=chunks),
        out_shape=jax.ShapeDtypeStruct((B, 512, nq * tq), BF16),
        grid=(B, DIFF_HEADS, nq),
        in_specs=[
            pl.BlockSpec((4, HEAD_DIM), lambda b, h, i: (0, 0)),
            pl.BlockSpec((DIFF_V, 1), lambda b, h, i: (0, 0)),
            q_spec, k_spec, vt_spec,
        ],
        out_specs=pl.BlockSpec((1, 128, tq), lambda b, h, i: (b, h, i)),
        compiler_params=_cparams(3),
        name="diff_attention_ctx" if context else "diff_attention",
    )(diff_lambda, subln_g.reshape(DIFF_V, 1), feat, tok, feat)


def _na_softmax_pv(s_c, vt_ctx, s_w=None, vt_win=None):
    m = jnp.max(s_c, axis=0, keepdims=True)
    if s_w is not None:
        m = jnp.maximum(m, jnp.max(s_w, axis=0, keepdims=True))
    p_c = jnp.exp2(s_c - m)
    l = jnp.sum(p_c, axis=0, keepdims=True)
    o = jnp.dot(vt_ctx, p_c.astype(BF16), preferred_element_type=F32)
    if s_w is not None:
        p_w = jnp.exp2(s_w - m)
        l = l + jnp.sum(p_w, axis=0, keepdims=True)
        o = o + jnp.dot(vt_win, p_w.astype(BF16), preferred_element_type=F32)
    o = o / l
    n = s_c.shape[1] // 2
    return jnp.concatenate([o[:64, :n], o[64:, n:]], axis=0).astype(BF16)


def _na_kernel(q_ref, k_ref, vt_ref, bias_ref, o_ref):
    i = pl.program_id(2)
    k_ctx = k_ref[0, S:, :]
    vt_ctx = vt_ref[0, :, S:]
    n_pairs = TQ_NA // LANE

    def scores(jj):
        r = 2 * (i * n_pairs + jj)
        start = jnp.clip(r - NA_ROWS // 2, 0, ROWS - NA_WIN_ROWS)
        pat = jnp.where(r == 0, 0, jnp.where(r == 2, 1, jnp.where(
            r == ROWS - 4, 3, jnp.where(r == ROWS - 2, 4, 2))))
        off = pl.multiple_of(start * GRID_W, LANE)
        rhs = _block_diag2(q_ref[0, :, jj * LANE:(jj + 1) * LANE])
        s_c = jnp.dot(k_ctx, rhs, preferred_element_type=F32)
        s_w = jnp.dot(k_ref[0, pl.ds(off, NA_WIN), :], rhs, preferred_element_type=F32) + bias_ref[0, pat]
        return s_c, s_w, off

    nxt = scores(0)
    for jj in range(n_pairs):
        s_c, s_w, off = nxt
        if jj + 1 < n_pairs:
            nxt = scores(jj + 1)
        o_ref[0, :, jj * LANE:(jj + 1) * LANE] = _na_softmax_pv(
            s_c, vt_ctx, s_w, vt_ref[0, :, pl.ds(off, NA_WIN)])


def _na_ctx_kernel(q_ref, k_ref, vt_ref, o_ref):
    s_c = jnp.dot(k_ref[0], _block_diag2(q_ref[0]), preferred_element_type=F32)
    o_ref[0] = _na_softmax_pv(s_c, vt_ref[0])


def _na_attention(tok, feat, bias):
    return pl.pallas_call(
        _na_kernel,
        out_shape=jax.ShapeDtypeStruct((B, 512, S), BF16),
        grid=(B, NA_HEADS // 2, S // TQ_NA),
        in_specs=[
            pl.BlockSpec((1, 128, TQ_NA), lambda b, h, i: (b, 8 + h, i)),
            pl.BlockSpec((1, N, 128), lambda b, h, i: (b, 0, 4 + h)),
            pl.BlockSpec((1, 128, N), lambda b, h, i: (b, 12 + h, 0)),
            pl.BlockSpec((1, NA_PATTERNS, NA_WIN, 2 * LANE), lambda b, h, i: (h, 0, 0, 0)),
        ],
        out_specs=pl.BlockSpec((1, 128, TQ_NA), lambda b, h, i: (b, h, i)),
        compiler_params=_cparams(3),
        name="na_attention",
    )(feat, tok, feat, bias)


def _na_attention_ctx(tok, feat):
    return pl.pallas_call(
        _na_ctx_kernel,
        out_shape=jax.ShapeDtypeStruct((B, 512, L), BF16),
        grid=(B, NA_HEADS // 2),
        in_specs=[
            pl.BlockSpec((1, 128, L), lambda b, h: (b, 8 + h, S // L)),
            pl.BlockSpec((1, L, 128), lambda b, h: (b, S // L, 4 + h)),
            pl.BlockSpec((1, 128, L), lambda b, h: (b, 12 + h, S // L)),
        ],
        out_specs=pl.BlockSpec((1, 128, L), lambda b, h: (b, h, 0)),
        compiler_params=_cparams(2),
        name="na_attention_ctx",
    )(feat, tok, feat)


def _na_bias_table(rpb):
    pats = [(0, 0), (2, 0), (8, 4), (ROWS - 4, ROWS - NA_WIN_ROWS), (ROWS - 2, ROWS - NA_WIN_ROWS)]
    kr_rel = np.arange(NA_WIN_ROWS)
    j = np.arange(2)
    drow = np.zeros((NA_PATTERNS, NA_WIN_ROWS, 2), np.int32)
    row_ok = np.zeros((NA_PATTERNS, NA_WIN_ROWS, 2), bool)
    for p, (r, start) in enumerate(pats):
        rq = r + j
        rs = np.clip(rq - NA_ROWS // 2, 0, ROWS - NA_ROWS)
        kr = start + kr_rel
        row_ok[p] = (kr[:, None] >= rs[None, :]) & (kr[:, None] < rs[None, :] + NA_ROWS)
        drow[p] = np.clip(kr[:, None] - rq[None, :] + NA_ROWS - 1, 0, 2 * NA_ROWS - 2)
    kc = np.arange(GRID_W)
    c = np.arange(GRID_W)
    cs = np.clip(c - NA_COLS // 2, 0, GRID_W - NA_COLS)
    col_ok = (kc[:, None] >= cs[None, :]) & (kc[:, None] < cs[None, :] + NA_COLS)
    dcol = kc[:, None] - c[None, :] + NA_COLS - 1
    onehot = (dcol[None] == np.arange(2 * NA_COLS - 1)[:, None, None]) & col_ok[None]
    rows = rpb[:, drow.reshape(-1), :]
    t = jnp.einsum("hrd,dkc->hrkc", rows, jnp.asarray(onehot, F32), precision=lax.Precision.HIGHEST)
    valid = row_ok.reshape(-1)[None, :, None, None] & col_ok[None, None]
    t = jnp.where(jnp.asarray(valid), t, NEG)
    t = t.reshape(NA_HEADS // 2, 2, NA_PATTERNS, NA_WIN_ROWS, 2, GRID_W, GRID_W)
    t = t.transpose(0, 2, 3, 5, 1, 4, 6)
    return t.reshape(NA_HEADS // 2, NA_PATTERNS, NA_WIN, 2 * LANE) * LOG2E


def _outproj_kernel(a_ref, b_ref, w_ref, h_ref, mv_ref, g_ref, beta_ref, o_ref, *, b_token_major):
    ta = (((0,), (0,)), ((), ()))
    y = lax.dot_general(a_ref[0], w_ref[:512, :], ta, preferred_element_type=F32)
    if b_token_major:
        y = y + jnp.dot(b_ref[0], w_ref[512:, :], preferred_element_type=F32)
    else:
        y = y + lax.dot_general(b_ref[0], w_ref[512:, :], ta, preferred_element_type=F32)
    gate = mv_ref[0][2:3]
    o_ref[0] = _layer_norm(ALPHA * h_ref[0] + gate * y, g_ref[...], beta_ref[...])


def _outproj(mix_a, mix_b, w, h, modv, g, beta, n_tiles, b_token_major):
    if b_token_major:
        b_spec = pl.BlockSpec((1, TM, 512), lambda b, i: (b, i, 0))
    else:
        b_spec = pl.BlockSpec((1, 512, TM), lambda b, i: (b, 0, i))
    return pl.pallas_call(
        functools.partial(_outproj_kernel, b_token_major=b_token_major),
        out_shape=jax.ShapeDtypeStruct((B, n_tiles * TM, D), F32),
        grid=(B, n_tiles),
        in_specs=[
            pl.BlockSpec((1, 512, TM), lambda b, i: (b, 0, i)),
            b_spec,
            pl.BlockSpec((D, D), lambda b, i: (0, 0)),
            pl.BlockSpec((1, TM, D), lambda b, i: (b, i, 0)),
            pl.BlockSpec((1, N_MOD, D), lambda b, i: (_mod_row(b, i), 0, 0)),
            pl.BlockSpec((1, D), lambda b, i: (0, 0)),
            pl.BlockSpec((1, D), lambda b, i: (0, 0)),
        ],
        out_specs=pl.BlockSpec((1, TM, D), lambda b, i: (b, i, 0)),
        compiler_params=_cparams(2),
        name="outproj_ln",
    )(mix_a, mix_b, w, h, modv, g.reshape(1, D), beta.reshape(1, D))


def _ffn_kernel(h_ref, mv_ref, win_ref, wout_ref, g_ref, beta_ref, o_ref):
    u = _modulated(h_ref, mv_ref, 3)
    ga = jnp.dot(u, win_ref[...], preferred_element_type=F32)
    gt, a = ga[:, :FFN_HIDDEN], ga[:, FFN_HIDDEN:]
    act = (gt * (1.0 / (1.0 + jnp.exp(-gt))) * a).astype(BF16)
    y = jnp.dot(act, wout_ref[...], preferred_element_type=F32)
    gate = mv_ref[0][5:6]
    o_ref[0] = _layer_norm(ALPHA * h_ref[0] + gate * y, g_ref[...], beta_ref[...])


def _ffn(h, modv, w_in, w_out, g, beta, n_tiles):
    return pl.pallas_call(
        _ffn_kernel,
        out_shape=jax.ShapeDtypeStruct((B, n_tiles * TM, D), F32),
        grid=(B, n_tiles),
        in_specs=[
            pl.BlockSpec((1, TM, D), lambda b, i: (b, i, 0)),
            pl.BlockSpec((1, N_MOD, D), lambda b, i: (_mod_row(b, i), 0, 0)),
            pl.BlockSpec((D, 2 * FFN_HIDDEN), lambda b, i: (0, 0)),
            pl.BlockSpec((FFN_HIDDEN, D), lambda b, i: (0, 0)),
            pl.BlockSpec((1, D), lambda b, i: (0, 0)),
            pl.BlockSpec((1, D), lambda b, i: (0, 0)),
        ],
        out_specs=pl.BlockSpec((1, TM, D), lambda b, i: (b, i, 0)),
        compiler_params=_cparams(2),
        name="ffn_ln",
    )(h, modv, w_in, w_out, g.reshape(1, D), beta.reshape(1, D))


def _gelu(x):
    return 0.5 * x * (1.0 + lax.erf(x * (1.0 / math.sqrt(2.0))))


def _rms(x, g):
    ms = jnp.mean(x * x, axis=-1, keepdims=True)
    return x * lax.rsqrt(ms + RMS_EPS) * g


def _inproj_odd_kernel(h_ref, mv_ref, win_ref, gq_ref, gkv_ref, wuq_ref, wkn_ref, wkr_ref, wv_ref,
                       cosf_ref, sinf_ref, cost_ref, sina_ref, sinb_ref,
                       lng_ref, lnb_ref, ws_ref, gb_ref,
                       q_ref, k_ref, vt_ref, gm_ref):
    nt = (((1,), (1,)), ((), ()))
    u = _modulated(h_ref, mv_ref, 0)
    p = jnp.dot(u, win_ref[...], preferred_element_type=F32)
    cq = _rms(p[:, :MLA_Q_RANK], gq_ref[...]).astype(BF16)
    ckv = _rms(p[:, MLA_Q_RANK:MLA_Q_RANK + MLA_KV_RANK], gkv_ref[...]).astype(BF16)
    kr = p[:, 640:768]
    qt = lax.dot_general(wuq_ref[...], cq, nt, preferred_element_type=F32)
    qt = qt * ((MLA_NOPE + MLA_ROPE) ** -0.5 * LOG2E)
    cf, sf = cosf_ref[...], sinf_ref[...]
    for hd in range(MLA_HEADS):
        base = hd * 128
        parts = [qt[base:base + 64]]
        for half in range(2):
            x1 = qt[base + 64 + half * 16:base + 72 + half * 16]
            x2 = qt[base + 72 + half * 16:base + 80 + half * 16]
            c = cf[half * 8:(half + 1) * 8]
            s = sf[half * 8:(half + 1) * 8]
            parts += [x1 * c - x2 * s, x2 * c + x1 * s]
        parts.append(qt[base + 96:base + 128])
        q_ref[0, base:base + 128, :] = jnp.concatenate(parts, axis=0).astype(BF16)
    kr = kr * cost_ref[...] + pltpu.roll(kr, LANE - 8, 1) * sina_ref[...] + pltpu.roll(kr, 8, 1) * sinb_ref[...]
    k = (jnp.dot(ckv, wkn_ref[...], preferred_element_type=F32)
         + jnp.dot(kr.astype(BF16), wkr_ref[...], preferred_element_type=F32))
    k_ref[0] = k.astype(BF16)
    vt_ref[0] = lax.dot_general(wv_ref[...], ckv, nt, preferred_element_type=F32).astype(BF16)
    gu = _gelu(p[:, 768:1280])
    gv = _layer_norm(_gelu(p[:, 1280:1792]), lng_ref[...], lnb_ref[...]).astype(BF16)
    for ch in range(TM // GMLP_CHUNK):
        rows = slice(ch * GMLP_CHUNK, (ch + 1) * GMLP_CHUNK)
        for grp in range(GMLP_GROUPS):
            cols = slice(grp * GMLP_CH, (grp + 1) * GMLP_CH)
            mixed = jnp.dot(ws_ref[grp], gv[rows, cols], preferred_element_type=F32) + gb_ref[grp]
            gm_ref[0, rows, cols] = (gu[rows, cols] * mixed).astype(BF16)


def _inproj_odd(h, modv, w, tabs):
    cosf, sinf, cost, sina, sinb = tabs
    const2 = lambda b, i: (0, 0)
    const3 = lambda b, i: (0, 0, 0)
    return pl.pallas_call(
        _inproj_odd_kernel,
        out_shape=(jax.ShapeDtypeStruct((B, 1024, N), BF16),
                   jax.ShapeDtypeStruct((B, N, 1024), BF16),
                   jax.ShapeDtypeStruct((B, 512, N), BF16),
                   jax.ShapeDtypeStruct((B, N, 512), BF16)),
        grid=(B, NT_ALL),
        in_specs=[
            pl.BlockSpec((1, TM, D), lambda b, i: (b, i, 0)),
            pl.BlockSpec((1, N_MOD, D), lambda b, i: (_mod_row(b, i), 0, 0)),
            pl.BlockSpec((D, 1792), const2),
            pl.BlockSpec((1, MLA_Q_RANK), const2),
            pl.BlockSpec((1, MLA_KV_RANK), const2),
            pl.BlockSpec((1024, MLA_Q_RANK), const2),
            pl.BlockSpec((MLA_KV_RANK, 1024), const2),
            pl.BlockSpec((LANE, 1024), const2),
            pl.BlockSpec((512, MLA_KV_RANK), const2),
            pl.BlockSpec((16, TM), lambda b, i: (0, i)),
            pl.BlockSpec((16, TM), lambda b, i: (0, i)),
            pl.BlockSpec((TM, LANE), lambda b, i: (i, 0)),
            pl.BlockSpec((TM, LANE), lambda b, i: (i, 0)),
            pl.BlockSpec((TM, LANE), lambda b, i: (i, 0)),
            pl.BlockSpec((1, GMLP_WIDTH), const2),
            pl.BlockSpec((1, GMLP_WIDTH), const2),
            pl.BlockSpec((GMLP_GROUPS, GMLP_CHUNK, GMLP_CHUNK), const3),
            pl.BlockSpec((GMLP_GROUPS, GMLP_CHUNK, 1), const3),
        ],
        out_specs=(pl.BlockSpec((1, 1024, TM), lambda b, i: (b, 0, i)),
                   pl.BlockSpec((1, TM, 1024), lambda b, i: (b, i, 0)),
                   pl.BlockSpec((1, 512, TM), lambda b, i: (b, 0, i)),
                   pl.BlockSpec((1, TM, 512), lambda b, i: (b, i, 0))),
        compiler_params=_cparams(2),
        name="inproj_odd",
    )(h, modv, w["win"], w["gq"], w["gkv"], w["wuq"], w["wkn"], w["wkr"], w["wv"],
      cosf, sinf, cost, sina, sinb, w["lng"], w["lnb"], w["ws"], w["gb"])


def _mla_attn_kernel(q_ref, k_ref, vt_ref, o_ref):
    chunks = _latent_chunks(TK_MLA)
    acc, l, overflow = _flash_fixed_ref(q_ref[0], k_ref, vt_ref, chunks)
    o_ref[0] = (acc / l).astype(BF16)

    @pl.when(overflow)
    def _recompute():
        acc, l = _flash(q_ref[0], k_ref, vt_ref, chunks)
        o_ref[0] = (acc / l).astype(BF16)


def _mla_attention(q_t, k, v_t):
    return pl.pallas_call(
        _mla_attn_kernel,
        out_shape=jax.ShapeDtypeStruct((B, 512, S), BF16),
        grid=(B, MLA_HEADS, S // TQ_MLA),
        in_specs=[
            pl.BlockSpec((1, 128, TQ_MLA), lambda b, h, i: (b, h, i)),
            pl.BlockSpec((1, N, 128), lambda b, h, i: (b, 0, h)),
            pl.BlockSpec((1, MLA_V, N), lambda b, h, i: (b, h, 0)),
        ],
        out_specs=pl.BlockSpec((1, MLA_V, TQ_MLA), lambda b, h, i: (b, h, i)),
        compiler_params=_cparams(3),
        name="mla_attention",
    )(q_t, k, v_t)


def _rope_tables(n_freq):
    t = jnp.arange(S)
    inv = jnp.power(ROPE_BASE, -jnp.arange(0, 2 * n_freq, 2, dtype=F32) / (2 * n_freq))
    ang_r = (t // GRID_W).astype(F32)[:, None] * inv[None, :]
    ang_c = (t % GRID_W).astype(F32)[:, None] * inv[None, :]
    cos = jnp.concatenate([jnp.cos(ang_r), jnp.cos(ang_c)], axis=1)
    sin = jnp.concatenate([jnp.sin(ang_r), jnp.sin(ang_c)], axis=1)
    cos = jnp.concatenate([cos, jnp.ones((L, 2 * n_freq), F32)], axis=0)
    sin = jnp.concatenate([sin, jnp.zeros((L, 2 * n_freq), F32)], axis=0)
    lane = np.arange(LANE)
    period = 4 * n_freq
    src = ((lane % period) // (2 * n_freq)) * n_freq + lane % n_freq
    is_x1 = (lane % (2 * n_freq)) < n_freq
    cos_t = cos[:, src]
    sin_t = sin[:, src]
    sin_a = jnp.where(jnp.asarray(is_x1)[None, :], -sin_t, 0.0)
    sin_b = jnp.where(jnp.asarray(is_x1)[None, :], 0.0, sin_t)
    return cos.T, sin.T, cos_t, sin_a, sin_b


def _even_weights(ev_w_in):
    aq, ak, av, bq, bk, bv = jnp.split(ev_w_in, [512, 1024, 1536, 2048, 2560], axis=1)
    scale = HEAD_DIM ** -0.5
    wtok = jnp.concatenate([ak, bk], axis=1).astype(BF16)
    wfeat = jnp.concatenate([aq * scale, av, bq * scale, bv], axis=1).T.astype(BF16)
    return wtok, wfeat


def _odd_weights(od_w_in, gq, w_uq, gkv, w_ukv, ln_g, ln_b, ws, gb):
    cq, ckv, kr, gu, gv = jnp.split(od_w_in, [384, 640, 672, 1184], axis=1)
    win = jnp.concatenate([cq, ckv, kr, jnp.zeros((D, LANE - MLA_ROPE), F32), gu, gv], axis=1)
    dq = MLA_NOPE + MLA_ROPE
    wuq = w_uq.reshape(MLA_Q_RANK, MLA_HEADS, dq)
    wuq = jnp.pad(wuq, ((0, 0), (0, 0), (0, LANE - dq))).reshape(MLA_Q_RANK, MLA_HEADS * LANE)
    wukv = w_ukv.reshape(MLA_KV_RANK, MLA_HEADS, MLA_NOPE + MLA_V)
    wkn = jnp.pad(wukv[:, :, :MLA_NOPE], ((0, 0), (0, 0), (0, LANE - MLA_NOPE)))
    wkn = wkn.reshape(MLA_KV_RANK, MLA_HEADS * LANE)
    wv = wukv[:, :, MLA_NOPE:].reshape(MLA_KV_RANK, MLA_HEADS * MLA_V)
    place = np.zeros((LANE, MLA_HEADS * LANE), np.float32)
    for hd in range(MLA_HEADS):
        place[np.arange(MLA_ROPE), hd * LANE + MLA_NOPE + np.arange(MLA_ROPE)] = 1.0
    return {
        "win": win.astype(BF16), "gq": gq.reshape(1, -1), "gkv": gkv.reshape(1, -1),
        "wuq": wuq.T.astype(BF16), "wkn": wkn.astype(BF16), "wkr": jnp.asarray(place, BF16),
        "wv": wv.T.astype(BF16), "lng": ln_g.reshape(1, -1), "lnb": ln_b.reshape(1, -1),
        "ws": ws.astype(BF16), "gb": gb.reshape(GMLP_GROUPS, GMLP_CHUNK, 1),
    }


def kernel(x, c, ctx, c_ctx, mod_w, mod_b, ln_mix_g, ln_mix_b, ln_ffn_g, ln_ffn_b, ffn_w_in, ffn_w_out,
           ev_w_in, ev_w_out, diff_lambda, diff_subln_g, na_rpb, od_w_in, od_w_out, mla_q_norm_g,
           mla_w_uq, mla_kv_norm_g, mla_w_ukv, gmlp_ln_g, gmlp_ln_b, gmlp_ws, gmlp_b):
    h = jnp.concatenate([x, ctx], axis=1)
    cond = jnp.concatenate([c, c_ctx[None], jnp.zeros((8 - B - 1, D), F32)], axis=0)
    mod = _modulation(cond, mod_w, mod_b).reshape(DEPTH, 8, N_MOD, D)

    wtok, wfeat = _even_weights(ev_w_in[0])
    cosf, sinf, cost, sina, sinb = _rope_tables(16)
    tok, feat = _inproj_even(h, mod[0], wtok, wfeat, (cosf * LOG2E, sinf * LOG2E, cost, sina, sinb))
    mix_a = jnp.concatenate([_diff_attention(tok, feat, diff_lambda[0], diff_subln_g[0], False),
                             _diff_attention(tok, feat, diff_lambda[0], diff_subln_g[0], True)], axis=2)
    mix_b = jnp.concatenate([_na_attention(tok, feat, _na_bias_table(na_rpb[0])),
                             _na_attention_ctx(tok, feat)], axis=2)
    h = _outproj(mix_a, mix_b, ev_w_out[0].astype(BF16), h, mod[0], ln_mix_g[0], ln_mix_b[0],
                 NT_ALL, False)
    h = _ffn(h, mod[0], ffn_w_in[0].astype(BF16), ffn_w_out[0].astype(BF16),
             ln_ffn_g[0], ln_ffn_b[0], NT_ALL)

    w1 = _odd_weights(od_w_in[0], mla_q_norm_g[0], mla_w_uq[0], mla_kv_norm_g[0], mla_w_ukv[0],
                      gmlp_ln_g[0], gmlp_ln_b[0], gmlp_ws[0], gmlp_b[0])
    q_t, k, v_t, gm = _inproj_odd(h, mod[1], w1, _rope_tables(8))
    mix_a = _mla_attention(q_t, k, v_t)
    h = _outproj(mix_a, gm, od_w_out[0].astype(BF16), h, mod[1], ln_mix_g[1], ln_mix_b[1],
                 NT_LAT, True)
    return _ffn(h, mod[1], ffn_w_in[1].astype(BF16), ffn_w_out[1].astype(BF16),
                ln_ffn_g[1], ln_ffn_b[1], NT_LAT)
```

```python
import functools
import math

import jax
import jax.numpy as jnp
import numpy as np
from jax import lax
from jax.experimental import pallas as pl
from jax.experimental.pallas import tpu as pltpu

F32 = jnp.float32
BF16 = jnp.bfloat16

D = 1024
B = 2
S = 8192
L = 256
N = S + L
DEPTH = 2
GRID_W = 64
ROWS = S // GRID_W
ALPHA = (2 * DEPTH) ** 0.25
ROPE_BASE = 10000.0
LN_EPS = 1e-5
RMS_EPS = 1e-6
N_MOD = 6
HEAD_DIM = 64
DIFF_HEADS = 4
DIFF_V = 128
NA_HEADS = 8
NA_ROWS = 8
NA_COLS = 16
MLA_HEADS = 8
MLA_Q_RANK = 384
MLA_KV_RANK = 256
MLA_NOPE = 64
MLA_ROPE = 32
MLA_V = 64
GMLP_GROUPS = 4
GMLP_CH = 128
GMLP_CHUNK = 128
GMLP_WIDTH = 512
FFN_HIDDEN = 2816
LAMBDA_INIT_0 = 0.8 - 0.6 * math.exp(-0.3 * 0)
LOG2E = math.log2(math.e)

LANE = 128
TM = 256
NT_ALL = N // TM
NT_LAT = S // TM
TQ_DIFF, TK_DIFF = 512, 256
TQ_MLA, TK_MLA = 512, 256
TQ_NA = 1024
NA_WIN_ROWS = 10
NA_WIN = NA_WIN_ROWS * GRID_W
NA_PATTERNS = 5
NEG = -1e30
VMEM_LIMIT = 56 * 1024 * 1024


def _cparams(n_axes):
    return pltpu.CompilerParams(dimension_semantics=("parallel",) * n_axes,
                                vmem_limit_bytes=VMEM_LIMIT)


def _mod_row(b, i):
    return jnp.where(i < NT_LAT, b, 2)


def _mod_kernel(cond_ref, w_ref, b_ref, o_ref):
    cnd = cond_ref[...]
    act = cnd * (1.0 / (1.0 + jnp.exp(-cnd)))
    o_ref[0] = jnp.dot(act, w_ref[0], preferred_element_type=F32) + b_ref[0]


def _modulation(cond, mod_w, mod_b):
    tn = 1536
    return pl.pallas_call(
        _mod_kernel,
        out_shape=jax.ShapeDtypeStruct((DEPTH, 8, N_MOD * D), F32),
        grid=(DEPTH, N_MOD * D // tn),
        in_specs=[
            pl.BlockSpec((8, D), lambda l, j: (0, 0)),
            pl.BlockSpec((1, D, tn), lambda l, j: (l, 0, j)),
            pl.BlockSpec((1, 1, tn), lambda l, j: (l, 0, j)),
        ],
        out_specs=pl.BlockSpec((1, 8, tn), lambda l, j: (l, 0, j)),
        compiler_params=_cparams(2),
        name="modulation",
    )(cond, mod_w, mod_b.reshape(DEPTH, 1, N_MOD * D))


def _modulated(h_ref, mv_ref, k):
    mv = mv_ref[0]
    return (h_ref[0] * (1.0 + mv[k + 1:k + 2]) + mv[k:k + 1]).astype(BF16)


def _layer_norm(x, g, b):
    mu = jnp.mean(x, axis=-1, keepdims=True)
    xc = x - mu
    var = jnp.mean(xc * xc, axis=-1, keepdims=True)
    return xc * lax.rsqrt(var + LN_EPS) * g + b


def _inproj_even_kernel(x_ref, c_ref, mv_ref, wtok_ref, wfeat_ref, cosf_ref, sinf_ref,
                        cost_ref, sina_ref, sinb_ref, tok_ref, feat_ref, u_ref):
    i = pl.program_id(1)

    @pl.when(i < NT_LAT)
    def _latent():
        u_ref[...] = _modulated(x_ref, mv_ref, 0)

    @pl.when(i == NT_LAT)
    def _context():
        u_ref[...] = _modulated(c_ref, mv_ref, 0)

    u = u_ref[...]
    tok = jnp.dot(u, wtok_ref[...], preferred_element_type=F32)
    ct, sa, sb = cost_ref[...], sina_ref[...], sinb_ref[...]
    for j in range(4):
        xs = tok[:, j * LANE:(j + 1) * LANE]
        ro = xs * ct + pltpu.roll(xs, LANE - 16, 1) * sa + pltpu.roll(xs, 16, 1) * sb
        tok_ref[0, :, j * LANE:(j + 1) * LANE] = ro.astype(BF16)
    tok_ref[0, :, 512:] = tok[:, 512:].astype(BF16)
    feat = lax.dot_general(wfeat_ref[...], u, (((1,), (1,)), ((), ())),
                           preferred_element_type=F32)
    cf, sf = cosf_ref[...], sinf_ref[...]
    for blk in range(8):
        for half in range(2):
            base = blk * 64 + half * 32
            x1 = feat[base:base + 16]
            x2 = feat[base + 16:base + 32]
            c = cf[half * 16:(half + 1) * 16]
            s = sf[half * 16:(half + 1) * 16]
            feat_ref[0, base:base + 16, :] = (x1 * c - x2 * s).astype(BF16)
            feat_ref[0, base + 16:base + 32, :] = (x2 * c + x1 * s).astype(BF16)
    feat_ref[0, 512:1024, :] = feat[512:1024].astype(BF16)
    feat_ref[0, 1024:1536, :] = (feat[1024:1536] * LOG2E).astype(BF16)
    feat_ref[0, 1536:, :] = feat[1536:].astype(BF16)


def _lat_tile(b, i):
    return (b, jnp.minimum(i, NT_LAT - 1), 0)


def _inproj_even(x, ctx, modv, wtok, wfeat, tabs):
    cosf, sinf, cost, sina, sinb = tabs
    const = lambda b, i: (0, 0)
    return pl.pallas_call(
        _inproj_even_kernel,
        out_shape=(jax.ShapeDtypeStruct((B, N, 1024), BF16),
                   jax.ShapeDtypeStruct((B, 2048, N), BF16)),
        grid=(B, NT_ALL),
        in_specs=[
            pl.BlockSpec((1, TM, D), _lat_tile),
            pl.BlockSpec((1, L, D), lambda b, i: (b, 0, 0)),
            pl.BlockSpec((1, N_MOD, D), lambda b, i: (_mod_row(b, i), 0, 0)),
            pl.BlockSpec((D, 1024), const),
            pl.BlockSpec((2048, D), const),
            pl.BlockSpec((32, TM), lambda b, i: (0, i)),
            pl.BlockSpec((32, TM), lambda b, i: (0, i)),
            pl.BlockSpec((TM, LANE), lambda b, i: (i, 0)),
            pl.BlockSpec((TM, LANE), lambda b, i: (i, 0)),
            pl.BlockSpec((TM, LANE), lambda b, i: (i, 0)),
        ],
        out_specs=(pl.BlockSpec((1, TM, 1024), lambda b, i: (b, i, 0)),
                   pl.BlockSpec((1, 2048, TM), lambda b, i: (b, 0, i))),
        scratch_shapes=[pltpu.VMEM((TM, D), BF16)],
        compiler_params=_cparams(2),
        name="inproj_even",
    )(x, ctx, modv, wtok, wfeat, cosf, sinf, cost, sina, sinb)


def _block_diag2(q_t):
    n = q_t.shape[1]
    z = jnp.zeros((64, n), q_t.dtype)
    left = jnp.concatenate([q_t[:64], z], axis=0)
    right = jnp.concatenate([z, q_t[64:]], axis=0)
    return jnp.concatenate([left, right], axis=1)


ONES_ROWS = 16


def _flash(rhs, k_ref, vt_ref, chunks):
    width = rhs.shape[1]
    dv = vt_ref.shape[1]
    m = jnp.full((1, width), -jnp.inf, F32)
    acc = jnp.zeros((dv + ONES_ROWS, width), F32)

    def scores(idx):
        st, sz = chunks[idx]
        return jnp.dot(k_ref[0, st:st + sz, :], rhs, preferred_element_type=F32)

    s_next = scores(0)
    for idx, (st, sz) in enumerate(chunks):
        s = s_next
        if idx + 1 < len(chunks):
            s_next = scores(idx + 1)
        m_new = jnp.maximum(m, jnp.max(s, axis=0, keepdims=True))
        alpha = jnp.exp2(m - m_new)
        p = jnp.exp2(s - m_new).astype(BF16)
        vt1 = jnp.concatenate([vt_ref[0, :, st:st + sz], jnp.ones((ONES_ROWS, sz), BF16)], axis=0)
        acc = acc * alpha + jnp.dot(vt1, p, preferred_element_type=F32)
        m = m_new
    return acc[:dv], acc[dv:dv + 1]


SCORES_AHEAD = 2
P_LIMIT = 2.0 ** 64


def _flash_fixed_ref(rhs, k_ref, vt_ref, chunks):
    width = rhs.shape[1]

    def scores(idx):
        st, sz = chunks[idx]
        return jnp.dot(k_ref[0, st:st + sz, :], rhs, preferred_element_type=F32)

    def values(idx):
        st, sz = chunks[idx]
        return vt_ref[0, :, st:st + sz]

    n = len(chunks)
    pending = [scores(i) for i in range(min(SCORES_AHEAD, n))]
    s = pending.pop(0)
    m_ref = jnp.max(s, axis=0, keepdims=True)
    if n > SCORES_AHEAD:
        pending.append(scores(SCORES_AHEAD))
    p = jnp.exp2(s - m_ref)
    l8 = jnp.sum(p.reshape(-1, 8, width), axis=0)
    acc = jnp.dot(values(0), p.astype(BF16), preferred_element_type=F32)
    for idx in range(1, n):
        s = pending.pop(0)
        if idx + SCORES_AHEAD < n:
            pending.append(scores(idx + SCORES_AHEAD))
        p = jnp.exp2(s - m_ref)
        l8 = l8 + jnp.sum(p.reshape(-1, 8, width), axis=0)
        acc = acc + jnp.dot(values(idx), p.astype(BF16), preferred_element_type=F32)
    l = jnp.sum(l8, axis=0, keepdims=True)
    return acc, l, jnp.max(l) > P_LIMIT


def _latent_chunks(tk):
    return [(j * tk, tk) for j in range(S // tk)] + [(S, L)]


def _diff_attn_kernel(lam_ref, g_ref, q_ref, k_ref, vt_ref, o_ref, *, chunks):
    tq = q_ref.shape[2]
    rhs = _block_diag2(q_ref[0])

    def finish(acc, l):
        lf = lam_ref[...]
        lam = (jnp.exp(jnp.sum(lf[0:1] * lf[1:2], axis=1, keepdims=True))
               - jnp.exp(jnp.sum(lf[2:3] * lf[3:4], axis=1, keepdims=True)) + LAMBDA_INIT_0)
        o = acc[:, :tq] / l[:, :tq] - lam * (acc[:, tq:] / l[:, tq:])
        ms = jnp.mean(o * o, axis=0, keepdims=True)
        o = o * lax.rsqrt(ms + RMS_EPS) * g_ref[...] * (1.0 - LAMBDA_INIT_0)
        o_ref[0] = o.astype(BF16)

    acc, l, overflow = _flash_fixed_ref(rhs, k_ref, vt_ref, chunks)
    finish(acc, l)

    @pl.when(overflow)
    def _recompute():
        finish(*_flash(rhs, k_ref, vt_ref, chunks))


def _diff_attention(tok, feat, diff_lambda, subln_g, context):
    if context:
        tq, nq, chunks = L, 1, [(0, L)]
        q_spec = pl.BlockSpec((1, 128, L), lambda b, h, i: (b, h, S // L))
        k_spec = pl.BlockSpec((1, L, 128), lambda b, h, i: (b, S // L, h))
        vt_spec = pl.BlockSpec((1, 128, L), lambda b, h, i: (b, 4 + h, S // L))
    else:
        tq, nq, chunks = TQ_DIFF, S // TQ_DIFF, _latent_chunks(TK_DIFF)
        q_spec = pl.BlockSpec((1, 128, tq), lambda b, h, i: (b, h, i))
        k_spec = pl.BlockSpec((1, N, 128), lambda b, h, i: (b, 0, h))
        vt_spec = pl.BlockSpec((1, 128, N), lambda b, h, i: (b, 4 + h, 0))
    return pl.pallas_call(
        functools.partial(_diff_attn_kernel, chunks=chunks),
        out_shape=jax.ShapeDtypeStruct((B, 512, nq * tq), BF16),
        grid=(B, DIFF_HEADS, nq),
        in_specs=[
            pl.BlockSpec((4, HEAD_DIM), lambda b, h, i: (0, 0)),
            pl.BlockSpec((DIFF_V, 1), lambda b, h, i: (0, 0)),
            q_spec, k_spec, vt_spec,
        ],
        out_specs=pl.BlockSpec((1, 128, tq), lambda b, h, i: (b, h, i)),
        compiler_params=_cparams(3),
        name="diff_attention_ctx" if context else "diff_attention",
    )(diff_lambda, subln_g.reshape(DIFF_V, 1), feat, tok, feat)


def _na_softmax_pv(s_c, vt_ctx, s_w=None, vt_win=None):
    m = jnp.max(s_c, axis=0, keepdims=True)
    if s_w is not None:
        m = jnp.maximum(m, jnp.max(s_w, axis=0, keepdims=True))
    p_c = jnp.exp2(s_c - m)
    l = jnp.sum(p_c, axis=0, keepdims=True)
    o = jnp.dot(vt_ctx, p_c.astype(BF16), preferred_element_type=F32)
    if s_w is not None:
        p_w = jnp.exp2(s_w - m)
        l = l + jnp.sum(p_w, axis=0, keepdims=True)
        o = o + jnp.dot(vt_win, p_w.astype(BF16), preferred_element_type=F32)
    o = o / l
    n = s_c.shape[1] // 2
    return jnp.concatenate([o[:64, :n], o[64:, n:]], axis=0).astype(BF16)


def _na_kernel(q_ref, k_ref, vt_ref, bias_ref, o_ref):
    i = pl.program_id(2)
    k_ctx = k_ref[0, S:, :]
    vt_ctx = vt_ref[0, :, S:]
    n_pairs = TQ_NA // LANE

    def scores(jj):
        r = 2 * (i * n_pairs + jj)
        start = jnp.clip(r - NA_ROWS // 2, 0, ROWS - NA_WIN_ROWS)
        pat = jnp.where(r == 0, 0, jnp.where(r == 2, 1, jnp.where(
            r == ROWS - 4, 3, jnp.where(r == ROWS - 2, 4, 2))))
        off = pl.multiple_of(start * GRID_W, LANE)
        rhs = _block_diag2(q_ref[0, :, jj * LANE:(jj + 1) * LANE])
        s_c = jnp.dot(k_ctx, rhs, preferred_element_type=F32)
        s_w = jnp.dot(k_ref[0, pl.ds(off, NA_WIN), :], rhs, preferred_element_type=F32) + bias_ref[0, pat]
        return s_c, s_w, off

    nxt = scores(0)
    for jj in range(n_pairs):
        s_c, s_w, off = nxt
        if jj + 1 < n_pairs:
            nxt = scores(jj + 1)
        o_ref[0, :, jj * LANE:(jj + 1) * LANE] = _na_softmax_pv(
            s_c, vt_ctx, s_w, vt_ref[0, :, pl.ds(off, NA_WIN)])


def _na_ctx_kernel(q_ref, k_ref, vt_ref, o_ref):
    s_c = jnp.dot(k_ref[0], _block_diag2(q_ref[0]), preferred_element_type=F32)
    o_ref[0] = _na_softmax_pv(s_c, vt_ref[0])


def _na_attention(tok, feat, bias):
    return pl.pallas_call(
        _na_kernel,
        out_shape=jax.ShapeDtypeStruct((B, 512, S), BF16),
        grid=(B, NA_HEADS // 2, S // TQ_NA),
        in_specs=[
            pl.BlockSpec((1, 128, TQ_NA), lambda b, h, i: (b, 8 + h, i)),
            pl.BlockSpec((1, N, 128), lambda b, h, i: (b, 0, 4 + h)),
            pl.BlockSpec((1, 128, N), lambda b, h, i: (b, 12 + h, 0)),
            pl.BlockSpec((1, NA_PATTERNS, NA_WIN, 2 * LANE), lambda b, h, i: (h, 0, 0, 0)),
        ],
        out_specs=pl.BlockSpec((1, 128, TQ_NA), lambda b, h, i: (b, h, i)),
        compiler_params=_cparams(3),
        name="na_attention",
    )(feat, tok, feat, bias)


def _na_attention_ctx(tok, feat):
    return pl.pallas_call(
        _na_ctx_kernel,
        out_shape=jax.ShapeDtypeStruct((B, 512, L), BF16),
        grid=(B, NA_HEADS // 2),
        in_specs=[
            pl.BlockSpec((1, 128, L), lambda b, h: (b, 8 + h, S // L)),
            pl.BlockSpec((1, L, 128), lambda b, h: (b, S // L, 4 + h)),
            pl.BlockSpec((1, 128, L), lambda b, h: (b, 12 + h, S // L)),
        ],
        out_specs=pl.BlockSpec((1, 128, L), lambda b, h: (b, h, 0)),
        compiler_params=_cparams(2),
        name="na_attention_ctx",
    )(feat, tok, feat)


def _na_bias_table(rpb):
    pats = [(0, 0), (2, 0), (8, 4), (ROWS - 4, ROWS - NA_WIN_ROWS), (ROWS - 2, ROWS - NA_WIN_ROWS)]
    n_off = 2 * NA_COLS - 1
    kr_rel = np.arange(NA_WIN_ROWS)
    j = np.arange(2)
    drow = np.zeros((NA_PATTERNS, NA_WIN_ROWS, 2), np.int32)
    row_ok = np.zeros((NA_PATTERNS, NA_WIN_ROWS, 2), bool)
    for p, (r, start) in enumerate(pats):
        rq = r + j
        rs = np.clip(rq - NA_ROWS // 2, 0, ROWS - NA_ROWS)
        kr = start + kr_rel
        row_ok[p] = (kr[:, None] >= rs[None, :]) & (kr[:, None] < rs[None, :] + NA_ROWS)
        drow[p] = np.clip(kr[:, None] - rq[None, :] + NA_ROWS - 1, 0, 2 * NA_ROWS - 2)
    kc = np.arange(GRID_W)
    c = np.arange(GRID_W)
    cs = np.clip(c - NA_COLS // 2, 0, GRID_W - NA_COLS)
    col_ok = (kc[:, None] >= cs[None, :]) & (kc[:, None] < cs[None, :] + NA_COLS)
    dcol = kc[:, None] - c[None, :] + NA_COLS - 1
    hit = (dcol[None] == np.arange(n_off)[:, None, None]) & col_ok[None]
    expand = np.zeros((2, 2, n_off, GRID_W, 2, 2, GRID_W), np.float32)
    for hh in range(2):
        for jj in range(2):
            expand[hh, jj, :, :, hh, jj, :] = hit
    expand = expand.reshape(4 * n_off, GRID_W, 2 * LANE)
    rows = rpb[:, drow.reshape(-1), :]
    rows = rows.reshape(NA_HEADS // 2, 2, NA_PATTERNS * NA_WIN_ROWS, 2, n_off)
    rows = rows.transpose(0, 2, 1, 3, 4).reshape(-1, 4 * n_off)
    t = jnp.einsum("gd,dkl->gkl", rows, jnp.asarray(expand), precision=lax.Precision.HIGHEST)
    valid = np.zeros((NA_PATTERNS, NA_WIN_ROWS, GRID_W, 2, 2, GRID_W), bool)
    valid[:] = row_ok[:, :, None, None, :, None] & col_ok[None, None, :, None, None, :]
    valid = valid.reshape(1, NA_PATTERNS, NA_WIN_ROWS, GRID_W, 2 * LANE)
    t = t.reshape(NA_HEADS // 2, NA_PATTERNS, NA_WIN_ROWS, GRID_W, 2 * LANE)
    t = jnp.where(jnp.asarray(valid), t, NEG) * LOG2E
    return t.reshape(NA_HEADS // 2, NA_PATTERNS, NA_WIN, 2 * LANE)


def _outproj_odd_kernel(a_ref, b_ref, w_ref, h_ref, mv_ref, g_ref, beta_ref, o_ref):
    y = (lax.dot_general(a_ref[0], w_ref[:512, :], (((0,), (0,)), ((), ())), preferred_element_type=F32)
         + jnp.dot(b_ref[0], w_ref[512:, :], preferred_element_type=F32))
    gate = mv_ref[0][2:3]
    o_ref[0] = _layer_norm(ALPHA * h_ref[0] + gate * y, g_ref[...], beta_ref[...])


def _outproj_even_kernel(al_ref, ac_ref, bl_ref, bc_ref, w_ref, x_ref, c_ref, mv_ref, g_ref, beta_ref,
                         o_ref):
    i = pl.program_id(1)
    ta = (((0,), (0,)), ((), ()))

    def project(a_ref, b_ref, h_ref):
        y = (lax.dot_general(a_ref[0], w_ref[:512, :], ta, preferred_element_type=F32)
             + lax.dot_general(b_ref[0], w_ref[512:, :], ta, preferred_element_type=F32))
        gate = mv_ref[0][2:3]
        o_ref[0] = _layer_norm(ALPHA * h_ref[0] + gate * y, g_ref[...], beta_ref[...])

    @pl.when(i < NT_LAT)
    def _latent():
        project(al_ref, bl_ref, x_ref)

    @pl.when(i == NT_LAT)
    def _context():
        project(ac_ref, bc_ref, c_ref)


def _outproj_even(a_lat, a_ctx, b_lat, b_ctx, w, x, ctx, modv, g, beta):
    lat_mix = pl.BlockSpec((1, 512, TM), lambda b, i: (b, 0, jnp.minimum(i, NT_LAT - 1)))
    ctx_mix = pl.BlockSpec((1, 512, L), lambda b, i: (b, 0, 0))
    return pl.pallas_call(
        _outproj_even_kernel,
        out_shape=jax.ShapeDtypeStruct((B, N, D), F32),
        grid=(B, NT_ALL),
        in_specs=[
            lat_mix, ctx_mix, lat_mix, ctx_mix,
            pl.BlockSpec((D, D), lambda b, i: (0, 0)),
            pl.BlockSpec((1, TM, D), _lat_tile),
            pl.BlockSpec((1, L, D), lambda b, i: (b, 0, 0)),
            pl.BlockSpec((1, N_MOD, D), lambda b, i: (_mod_row(b, i), 0, 0)),
            pl.BlockSpec((1, D), lambda b, i: (0, 0)),
            pl.BlockSpec((1, D), lambda b, i: (0, 0)),
        ],
        out_specs=pl.BlockSpec((1, TM, D), lambda b, i: (b, i, 0)),
        compiler_params=_cparams(2),
        name="outproj_even_ln",
    )(a_lat, a_ctx, b_lat, b_ctx, w, x, ctx, modv, g.reshape(1, D), beta.reshape(1, D))


def _outproj_odd(mix_a, mix_b, w, h, modv, g, beta):
    return pl.pallas_call(
        _outproj_odd_kernel,
        out_shape=jax.ShapeDtypeStruct((B, S, D), F32),
        grid=(B, NT_LAT),
        in_specs=[
            pl.BlockSpec((1, 512, TM), lambda b, i: (b, 0, i)),
            pl.BlockSpec((1, TM, 512), lambda b, i: (b, i, 0)),
            pl.BlockSpec((D, D), lambda b, i: (0, 0)),
            pl.BlockSpec((1, TM, D), lambda b, i: (b, i, 0)),
            pl.BlockSpec((1, N_MOD, D), lambda b, i: (b, 0, 0)),
            pl.BlockSpec((1, D), lambda b, i: (0, 0)),
            pl.BlockSpec((1, D), lambda b, i: (0, 0)),
        ],
        out_specs=pl.BlockSpec((1, TM, D), lambda b, i: (b, i, 0)),
        compiler_params=_cparams(2),
        name="outproj_odd_ln",
    )(mix_a, mix_b, w, h, modv, g.reshape(1, D), beta.reshape(1, D))


def _ffn_kernel(h_ref, mv_ref, win_ref, wout_ref, g_ref, beta_ref, o_ref):
    u = _modulated(h_ref, mv_ref, 3)
    ga = jnp.dot(u, win_ref[...], preferred_element_type=F32)
    gt, a = ga[:, :FFN_HIDDEN], ga[:, FFN_HIDDEN:]
    act = (gt * (1.0 / (1.0 + jnp.exp(-gt))) * a).astype(BF16)
    y = jnp.dot(act, wout_ref[...], preferred_element_type=F32)
    gate = mv_ref[0][5:6]
    o_ref[0] = _layer_norm(ALPHA * h_ref[0] + gate * y, g_ref[...], beta_ref[...])


def _ffn(h, modv, w_in, w_out, g, beta, n_tiles):
    return pl.pallas_call(
        _ffn_kernel,
        out_shape=jax.ShapeDtypeStruct((B, n_tiles * TM, D), F32),
        grid=(B, n_tiles),
        in_specs=[
            pl.BlockSpec((1, TM, D), lambda b, i: (b, i, 0)),
            pl.BlockSpec((1, N_MOD, D), lambda b, i: (_mod_row(b, i), 0, 0)),
            pl.BlockSpec((D, 2 * FFN_HIDDEN), lambda b, i: (0, 0)),
            pl.BlockSpec((FFN_HIDDEN, D), lambda b, i: (0, 0)),
            pl.BlockSpec((1, D), lambda b, i: (0, 0)),
            pl.BlockSpec((1, D), lambda b, i: (0, 0)),
        ],
        out_specs=pl.BlockSpec((1, TM, D), lambda b, i: (b, i, 0)),
        compiler_params=_cparams(2),
        name="ffn_ln",
    )(h, modv, w_in, w_out, g.reshape(1, D), beta.reshape(1, D))


def _gelu(x):
    return 0.5 * x * (1.0 + lax.erf(x * (1.0 / math.sqrt(2.0))))


def _rms(x, g):
    ms = jnp.mean(x * x, axis=-1, keepdims=True)
    return x * lax.rsqrt(ms + RMS_EPS) * g


def _inproj_odd_kernel(h_ref, mv_ref, win_ref, gq_ref, gkv_ref, wuq_ref, wkn_ref, wkr_ref, wv_ref,
                       cosf_ref, sinf_ref, cost_ref, sina_ref, sinb_ref,
                       lng_ref, lnb_ref, ws_ref, gb_ref,
                       q_ref, k_ref, vt_ref, gm_ref):
    nt = (((1,), (1,)), ((), ()))
    u = _modulated(h_ref, mv_ref, 0)
    p = jnp.dot(u, win_ref[...], preferred_element_type=F32)
    cq = _rms(p[:, :MLA_Q_RANK], gq_ref[...]).astype(BF16)
    ckv = _rms(p[:, MLA_Q_RANK:MLA_Q_RANK + MLA_KV_RANK], gkv_ref[...]).astype(BF16)
    kr = p[:, 640:768]
    qt = lax.dot_general(wuq_ref[...], cq, nt, preferred_element_type=F32)
    qt = qt * ((MLA_NOPE + MLA_ROPE) ** -0.5 * LOG2E)
    cf, sf = cosf_ref[...], sinf_ref[...]
    for hd in range(MLA_HEADS):
        base = hd * 128
        parts = [qt[base:base + 64]]
        for half in range(2):
            x1 = qt[base + 64 + half * 16:base + 72 + half * 16]
            x2 = qt[base + 72 + half * 16:base + 80 + half * 16]
            c = cf[half * 8:(half + 1) * 8]
            s = sf[half * 8:(half + 1) * 8]
            parts += [x1 * c - x2 * s, x2 * c + x1 * s]
        parts.append(qt[base + 96:base + 128])
        q_ref[0, base:base + 128, :] = jnp.concatenate(parts, axis=0).astype(BF16)
    kr = kr * cost_ref[...] + pltpu.roll(kr, LANE - 8, 1) * sina_ref[...] + pltpu.roll(kr, 8, 1) * sinb_ref[...]
    k = (jnp.dot(ckv, wkn_ref[...], preferred_element_type=F32)
         + jnp.dot(kr.astype(BF16), wkr_ref[...], preferred_element_type=F32))
    k_ref[0] = k.astype(BF16)
    vt_ref[0] = lax.dot_general(wv_ref[...], ckv, nt, preferred_element_type=F32).astype(BF16)
    gu = _gelu(p[:, 768:1280])
    gv = _layer_norm(_gelu(p[:, 1280:1792]), lng_ref[...], lnb_ref[...]).astype(BF16)
    for ch in range(TM // GMLP_CHUNK):
        rows = slice(ch * GMLP_CHUNK, (ch + 1) * GMLP_CHUNK)
        for grp in range(GMLP_GROUPS):
            cols = slice(grp * GMLP_CH, (grp + 1) * GMLP_CH)
            mixed = jnp.dot(ws_ref[grp], gv[rows, cols], preferred_element_type=F32) + gb_ref[grp]
            gm_ref[0, rows, cols] = (gu[rows, cols] * mixed).astype(BF16)


def _inproj_odd(h, modv, w, tabs):
    cosf, sinf, cost, sina, sinb = tabs
    const2 = lambda b, i: (0, 0)
    const3 = lambda b, i: (0, 0, 0)
    return pl.pallas_call(
        _inproj_odd_kernel,
        out_shape=(jax.ShapeDtypeStruct((B, 1024, N), BF16),
                   jax.ShapeDtypeStruct((B, N, 1024), BF16),
                   jax.ShapeDtypeStruct((B, 512, N), BF16),
                   jax.ShapeDtypeStruct((B, N, 512), BF16)),
        grid=(B, NT_ALL),
        in_specs=[
            pl.BlockSpec((1, TM, D), lambda b, i: (b, i, 0)),
            pl.BlockSpec((1, N_MOD, D), lambda b, i: (_mod_row(b, i), 0, 0)),
            pl.BlockSpec((D, 1792), const2),
            pl.BlockSpec((1, MLA_Q_RANK), const2),
            pl.BlockSpec((1, MLA_KV_RANK), const2),
            pl.BlockSpec((1024, MLA_Q_RANK), const2),
            pl.BlockSpec((MLA_KV_RANK, 1024), const2),
            pl.BlockSpec((LANE, 1024), const2),
            pl.BlockSpec((512, MLA_KV_RANK), const2),
            pl.BlockSpec((16, TM), lambda b, i: (0, i)),
            pl.BlockSpec((16, TM), lambda b, i: (0, i)),
            pl.BlockSpec((TM, LANE), lambda b, i: (i, 0)),
            pl.BlockSpec((TM, LANE), lambda b, i: (i, 0)),
            pl.BlockSpec((TM, LANE), lambda b, i: (i, 0)),
            pl.BlockSpec((1, GMLP_WIDTH), const2),
            pl.BlockSpec((1, GMLP_WIDTH), const2),
            pl.BlockSpec((GMLP_GROUPS, GMLP_CHUNK, GMLP_CHUNK), const3),
            pl.BlockSpec((GMLP_GROUPS, GMLP_CHUNK, 1), const3),
        ],
        out_specs=(pl.BlockSpec((1, 1024, TM), lambda b, i: (b, 0, i)),
                   pl.BlockSpec((1, TM, 1024), lambda b, i: (b, i, 0)),
                   pl.BlockSpec((1, 512, TM), lambda b, i: (b, 0, i)),
                   pl.BlockSpec((1, TM, 512), lambda b, i: (b, i, 0))),
        compiler_params=_cparams(2),
        name="inproj_odd",
    )(h, modv, w["win"], w["gq"], w["gkv"], w["wuq"], w["wkn"], w["wkr"], w["wv"],
      cosf, sinf, cost, sina, sinb, w["lng"], w["lnb"], w["ws"], w["gb"])


def _mla_attn_kernel(q_ref, k_ref, vt_ref, o_ref):
    chunks = _latent_chunks(TK_MLA)
    acc, l, overflow = _flash_fixed_ref(q_ref[0], k_ref, vt_ref, chunks)
    o_ref[0] = (acc / l).astype(BF16)

    @pl.when(overflow)
    def _recompute():
        acc, l = _flash(q_ref[0], k_ref, vt_ref, chunks)
        o_ref[0] = (acc / l).astype(BF16)


def _mla_attention(q_t, k, v_t):
    return pl.pallas_call(
        _mla_attn_kernel,
        out_shape=jax.ShapeDtypeStruct((B, 512, S), BF16),
        grid=(B, MLA_HEADS, S // TQ_MLA),
        in_specs=[
            pl.BlockSpec((1, 128, TQ_MLA), lambda b, h, i: (b, h, i)),
            pl.BlockSpec((1, N, 128), lambda b, h, i: (b, 0, h)),
            pl.BlockSpec((1, MLA_V, N), lambda b, h, i: (b, h, 0)),
        ],
        out_specs=pl.BlockSpec((1, MLA_V, TQ_MLA), lambda b, h, i: (b, h, i)),
        compiler_params=_cparams(3),
        name="mla_attention",
    )(q_t, k, v_t)


def _rope_tables(n_freq):
    t = jnp.arange(S)
    inv = jnp.power(ROPE_BASE, -jnp.arange(0, 2 * n_freq, 2, dtype=F32) / (2 * n_freq))
    ang_r = (t // GRID_W).astype(F32)[:, None] * inv[None, :]
    ang_c = (t % GRID_W).astype(F32)[:, None] * inv[None, :]
    cos = jnp.concatenate([jnp.cos(ang_r), jnp.cos(ang_c)], axis=1)
    sin = jnp.concatenate([jnp.sin(ang_r), jnp.sin(ang_c)], axis=1)
    cos = jnp.concatenate([cos, jnp.ones((L, 2 * n_freq), F32)], axis=0)
    sin = jnp.concatenate([sin, jnp.zeros((L, 2 * n_freq), F32)], axis=0)
    lane = np.arange(LANE)
    period = 4 * n_freq
    src = ((lane % period) // (2 * n_freq)) * n_freq + lane % n_freq
    is_x1 = (lane % (2 * n_freq)) < n_freq
    cos_t = cos[:, src]
    sin_t = sin[:, src]
    sin_a = jnp.where(jnp.asarray(is_x1)[None, :], -sin_t, 0.0)
    sin_b = jnp.where(jnp.asarray(is_x1)[None, :], 0.0, sin_t)
    return cos.T, sin.T, cos_t, sin_a, sin_b


def _even_weights(ev_w_in):
    aq, ak, av, bq, bk, bv = jnp.split(ev_w_in, [512, 1024, 1536, 2048, 2560], axis=1)
    scale = HEAD_DIM ** -0.5
    wtok = jnp.concatenate([ak, bk], axis=1).astype(BF16)
    wfeat = jnp.concatenate([aq * scale, av, bq * scale, bv], axis=1).T.astype(BF16)
    return wtok, wfeat


def _odd_weights(od_w_in, gq, w_uq, gkv, w_ukv, ln_g, ln_b, ws, gb):
    cq, ckv, kr, gu, gv = jnp.split(od_w_in, [384, 640, 672, 1184], axis=1)
    win = jnp.concatenate([cq, ckv, kr, jnp.zeros((D, LANE - MLA_ROPE), F32), gu, gv], axis=1)
    dq = MLA_NOPE + MLA_ROPE
    wuq = w_uq.reshape(MLA_Q_RANK, MLA_HEADS, dq)
    wuq = jnp.pad(wuq, ((0, 0), (0, 0), (0, LANE - dq))).reshape(MLA_Q_RANK, MLA_HEADS * LANE)
    wukv = w_ukv.reshape(MLA_KV_RANK, MLA_HEADS, MLA_NOPE + MLA_V)
    wkn = jnp.pad(wukv[:, :, :MLA_NOPE], ((0, 0), (0, 0), (0, LANE - MLA_NOPE)))
    wkn = wkn.reshape(MLA_KV_RANK, MLA_HEADS * LANE)
    wv = wukv[:, :, MLA_NOPE:].reshape(MLA_KV_RANK, MLA_HEADS * MLA_V)
    place = np.zeros((LANE, MLA_HEADS * LANE), np.float32)
    for hd in range(MLA_HEADS):
        place[np.arange(MLA_ROPE), hd * LANE + MLA_NOPE + np.arange(MLA_ROPE)] = 1.0
    return {
        "win": win.astype(BF16), "gq": gq.reshape(1, -1), "gkv": gkv.reshape(1, -1),
        "wuq": wuq.T.astype(BF16), "wkn": wkn.astype(BF16), "wkr": jnp.asarray(place, BF16),
        "wv": wv.T.astype(BF16), "lng": ln_g.reshape(1, -1), "lnb": ln_b.reshape(1, -1),
        "ws": ws.astype(BF16), "gb": gb.reshape(GMLP_GROUPS, GMLP_CHUNK, 1),
    }


def kernel(x, c, ctx, c_ctx, mod_w, mod_b, ln_mix_g, ln_mix_b, ln_ffn_g, ln_ffn_b, ffn_w_in, ffn_w_out,
           ev_w_in, ev_w_out, diff_lambda, diff_subln_g, na_rpb, od_w_in, od_w_out, mla_q_norm_g,
           mla_w_uq, mla_kv_norm_g, mla_w_ukv, gmlp_ln_g, gmlp_ln_b, gmlp_ws, gmlp_b):
    cond = jnp.concatenate([c, c_ctx[None], jnp.zeros((8 - B - 1, D), F32)], axis=0)
    mod = _modulation(cond, mod_w, mod_b).reshape(DEPTH, 8, N_MOD, D)

    wtok, wfeat = _even_weights(ev_w_in[0])
    cosf, sinf, cost, sina, sinb = _rope_tables(16)
    tok, feat = _inproj_even(x, ctx, mod[0], wtok, wfeat,
                             (cosf * LOG2E, sinf * LOG2E, cost, sina, sinb))
    h = _outproj_even(_diff_attention(tok, feat, diff_lambda[0], diff_subln_g[0], False),
                      _diff_attention(tok, feat, diff_lambda[0], diff_subln_g[0], True),
                      _na_attention(tok, feat, _na_bias_table(na_rpb[0])),
                      _na_attention_ctx(tok, feat),
                      ev_w_out[0].astype(BF16), x, ctx, mod[0], ln_mix_g[0], ln_mix_b[0])
    h = _ffn(h, mod[0], ffn_w_in[0].astype(BF16), ffn_w_out[0].astype(BF16),
             ln_ffn_g[0], ln_ffn_b[0], NT_ALL)

    w1 = _odd_weights(od_w_in[0], mla_q_norm_g[0], mla_w_uq[0], mla_kv_norm_g[0], mla_w_ukv[0],
                      gmlp_ln_g[0], gmlp_ln_b[0], gmlp_ws[0], gmlp_b[0])
    q_t, k, v_t, gm = _inproj_odd(h, mod[1], w1, _rope_tables(8))
    mix_a = _mla_attention(q_t, k, v_t)
    h = _outproj_odd(mix_a, gm, od_w_out[0].astype(BF16), h, mod[1], ln_mix_g[1], ln_mix_b[1])
    return _ffn(h, mod[1], ffn_w_in[1].astype(BF16), ffn_w_out[1].astype(BF16),
                ln_ffn_g[1], ln_ffn_b[1], NT_LAT)
```

```python
import functools
import math

import jax
import jax.numpy as jnp
import numpy as np
from jax import lax
from jax.experimental import pallas as pl
from jax.experimental.pallas import tpu as pltpu

F32 = jnp.float32
BF16 = jnp.bfloat16

D = 1024
B = 2
S = 8192
L = 256
N = S + L
DEPTH = 2
GRID_W = 64
ROWS = S // GRID_W
ALPHA = (2 * DEPTH) ** 0.25
ROPE_BASE = 10000.0
LN_EPS = 1e-5
RMS_EPS = 1e-6
N_MOD = 6
HEAD_DIM = 64
DIFF_HEADS = 4
DIFF_V = 128
NA_HEADS = 8
NA_ROWS = 8
NA_COLS = 16
MLA_HEADS = 8
MLA_Q_RANK = 384
MLA_KV_RANK = 256
MLA_NOPE = 64
MLA_ROPE = 32
MLA_V = 64
GMLP_GROUPS = 4
GMLP_CH = 128
GMLP_CHUNK = 128
GMLP_WIDTH = 512
FFN_HIDDEN = 2816
LAMBDA_INIT_0 = 0.8 - 0.6 * math.exp(-0.3 * 0)
LOG2E = math.log2(math.e)

LANE = 128
TM = 256
NT_ALL = N // TM
NT_LAT = S // TM
TQ_DIFF, TK_DIFF = 256, 256
TQ_MLA, TK_MLA = 512, 256
TILES_DIFF, TILES_MLA = 4, 4
TQ_NA = 1024
NA_WIN_ROWS = 10
NA_WIN = NA_WIN_ROWS * GRID_W
NA_PATTERNS = 5
NEG = -1e30
VMEM_LIMIT = 56 * 1024 * 1024


def _cparams(n_axes):
    return pltpu.CompilerParams(dimension_semantics=("parallel",) * n_axes,
                                vmem_limit_bytes=VMEM_LIMIT)


def _mod_row(b, i):
    return jnp.where(i < NT_LAT, b, 2)


def _mod_kernel(cond_ref, w_ref, b_ref, o_ref):
    cnd = cond_ref[...]
    act = cnd * (1.0 / (1.0 + jnp.exp(-cnd)))
    o_ref[0] = jnp.dot(act, w_ref[0], preferred_element_type=F32) + b_ref[0]


def _modulation(cond, mod_w, mod_b):
    tn = 1536
    return pl.pallas_call(
        _mod_kernel,
        out_shape=jax.ShapeDtypeStruct((DEPTH, 8, N_MOD * D), F32),
        grid=(DEPTH, N_MOD * D // tn),
        in_specs=[
            pl.BlockSpec((8, D), lambda l, j: (0, 0)),
            pl.BlockSpec((1, D, tn), lambda l, j: (l, 0, j)),
            pl.BlockSpec((1, 1, tn), lambda l, j: (l, 0, j)),
        ],
        out_specs=pl.BlockSpec((1, 8, tn), lambda l, j: (l, 0, j)),
        compiler_params=_cparams(2),
        name="modulation",
    )(cond, mod_w, mod_b.reshape(DEPTH, 1, N_MOD * D))


def _modulated(h_ref, mv_ref, k):
    mv = mv_ref[0]
    return (h_ref[0] * (1.0 + mv[k + 1:k + 2]) + mv[k:k + 1]).astype(BF16)


def _layer_norm(x, g, b):
    mu = jnp.mean(x, axis=-1, keepdims=True)
    xc = x - mu
    var = jnp.mean(xc * xc, axis=-1, keepdims=True)
    return xc * lax.rsqrt(var + LN_EPS) * g + b


def _inproj_even_kernel(x_ref, c_ref, mv_ref, wtok_ref, wfeat_ref, cosf_ref, sinf_ref,
                        cost_ref, sina_ref, sinb_ref, tok_ref, feat_ref, u_ref):
    i = pl.program_id(1)

    @pl.when(i < NT_LAT)
    def _latent():
        u_ref[...] = _modulated(x_ref, mv_ref, 0)

    @pl.when(i == NT_LAT)
    def _context():
        u_ref[...] = _modulated(c_ref, mv_ref, 0)

    u = u_ref[...]
    tok = jnp.dot(u, wtok_ref[...], preferred_element_type=F32)
    ct, sa, sb = cost_ref[...], sina_ref[...], sinb_ref[...]
    for j in range(4):
        xs = tok[:, j * LANE:(j + 1) * LANE]
        ro = xs * ct + pltpu.roll(xs, LANE - 16, 1) * sa + pltpu.roll(xs, 16, 1) * sb
        tok_ref[0, :, j * LANE:(j + 1) * LANE] = ro.astype(BF16)
    tok_ref[0, :, 512:] = tok[:, 512:].astype(BF16)
    feat = lax.dot_general(wfeat_ref[...], u, (((1,), (1,)), ((), ())),
                           preferred_element_type=F32)
    cf, sf = cosf_ref[...], sinf_ref[...]
    for blk in range(8):
        for half in range(2):
            base = blk * 64 + half * 32
            x1 = feat[base:base + 16]
            x2 = feat[base + 16:base + 32]
            c = cf[half * 16:(half + 1) * 16]
            s = sf[half * 16:(half + 1) * 16]
            feat_ref[0, base:base + 16, :] = (x1 * c - x2 * s).astype(BF16)
            feat_ref[0, base + 16:base + 32, :] = (x2 * c + x1 * s).astype(BF16)
    feat_ref[0, 512:1024, :] = feat[512:1024].astype(BF16)
    feat_ref[0, 1024:1536, :] = (feat[1024:1536] * LOG2E).astype(BF16)
    feat_ref[0, 1536:, :] = feat[1536:].astype(BF16)


def _lat_tile(b, i):
    return (b, jnp.minimum(i, NT_LAT - 1), 0)


def _inproj_even(x, ctx, modv, wtok, wfeat, tabs):
    cosf, sinf, cost, sina, sinb = tabs
    const = lambda b, i: (0, 0)
    return pl.pallas_call(
        _inproj_even_kernel,
        out_shape=(jax.ShapeDtypeStruct((B, N, 1024), BF16),
                   jax.ShapeDtypeStruct((B, 2048, N), BF16)),
        grid=(B, NT_ALL),
        in_specs=[
            pl.BlockSpec((1, TM, D), _lat_tile),
            pl.BlockSpec((1, L, D), lambda b, i: (b, 0, 0)),
            pl.BlockSpec((1, N_MOD, D), lambda b, i: (_mod_row(b, i), 0, 0)),
            pl.BlockSpec((D, 1024), const),
            pl.BlockSpec((2048, D), const),
            pl.BlockSpec((32, TM), lambda b, i: (0, i)),
            pl.BlockSpec((32, TM), lambda b, i: (0, i)),
            pl.BlockSpec((TM, LANE), lambda b, i: (i, 0)),
            pl.BlockSpec((TM, LANE), lambda b, i: (i, 0)),
            pl.BlockSpec((TM, LANE), lambda b, i: (i, 0)),
        ],
        out_specs=(pl.BlockSpec((1, TM, 1024), lambda b, i: (b, i, 0)),
                   pl.BlockSpec((1, 2048, TM), lambda b, i: (b, 0, i))),
        scratch_shapes=[pltpu.VMEM((TM, D), BF16)],
        compiler_params=_cparams(2),
        name="inproj_even",
    )(x, ctx, modv, wtok, wfeat, cosf, sinf, cost, sina, sinb)


def _block_diag2(q_t):
    n = q_t.shape[1]
    z = jnp.zeros((64, n), q_t.dtype)
    left = jnp.concatenate([q_t[:64], z], axis=0)
    right = jnp.concatenate([z, q_t[64:]], axis=0)
    return jnp.concatenate([left, right], axis=1)


ONES_ROWS = 16


def _flash(rhs, k_ref, vt_ref, chunks):
    width = rhs.shape[1]
    dv = vt_ref.shape[1]
    m = jnp.full((1, width), -jnp.inf, F32)
    acc = jnp.zeros((dv + ONES_ROWS, width), F32)

    def scores(idx):
        st, sz = chunks[idx]
        return jnp.dot(k_ref[0, st:st + sz, :], rhs, preferred_element_type=F32)

    s_next = scores(0)
    for idx, (st, sz) in enumerate(chunks):
        s = s_next
        if idx + 1 < len(chunks):
            s_next = scores(idx + 1)
        m_new = jnp.maximum(m, jnp.max(s, axis=0, keepdims=True))
        alpha = jnp.exp2(m - m_new)
        p = jnp.exp2(s - m_new).astype(BF16)
        vt1 = jnp.concatenate([vt_ref[0, :, st:st + sz], jnp.ones((ONES_ROWS, sz), BF16)], axis=0)
        acc = acc * alpha + jnp.dot(vt1, p, preferred_element_type=F32)
        m = m_new
    return acc[:dv], acc[dv:dv + 1]


SCORES_AHEAD = 2
P_LIMIT = 2.0 ** 64


def _flash_fixed_ref(rhs_tiles, k_ref, vt_ref, chunks):
    n = len(chunks)
    items = [(t, idx) for t in range(len(rhs_tiles)) for idx in range(n)]

    def scores(item):
        t, idx = item
        st, sz = chunks[idx]
        return jnp.dot(k_ref[0, st:st + sz, :], rhs_tiles[t], preferred_element_type=F32)

    pending = [scores(it) for it in items[:SCORES_AHEAD]]
    results = []
    worst = None
    for j, (t, idx) in enumerate(items):
        s = pending.pop(0)
        if j + SCORES_AHEAD < len(items):
            pending.append(scores(items[j + SCORES_AHEAD]))
        width = s.shape[1]
        if idx == 0:
            m_ref = jnp.max(s, axis=0, keepdims=True)
        p = jnp.exp2(s - m_ref)
        l8_new = jnp.sum(p.reshape(-1, 8, width), axis=0)
        st, sz = chunks[idx]
        pv = jnp.dot(vt_ref[0, :, st:st + sz], p.astype(BF16), preferred_element_type=F32)
        l8, acc = (l8_new, pv) if idx == 0 else (l8 + l8_new, acc + pv)
        if idx == n - 1:
            l = jnp.sum(l8, axis=0, keepdims=True)
            results.append((acc, l))
            l_max = jnp.max(l)
            worst = l_max if worst is None else jnp.maximum(worst, l_max)
    return results, worst > P_LIMIT


def _latent_chunks(tk):
    return [(j * tk, tk) for j in range(S // tk)] + [(S, L)]


def _diff_attn_kernel(lam_ref, g_ref, q_ref, k_ref, vt_ref, o_ref, *, chunks, tq):
    n_tiles = q_ref.shape[2] // tq

    def finish(acc, l):
        lf = lam_ref[...]
        lam = (jnp.exp(jnp.sum(lf[0:1] * lf[1:2], axis=1, keepdims=True))
               - jnp.exp(jnp.sum(lf[2:3] * lf[3:4], axis=1, keepdims=True)) + LAMBDA_INIT_0)
        o = acc[:, :tq] / l[:, :tq] - lam * (acc[:, tq:] / l[:, tq:])
        ms = jnp.mean(o * o, axis=0, keepdims=True)
        o = o * lax.rsqrt(ms + RMS_EPS) * g_ref[...] * (1.0 - LAMBDA_INIT_0)
        return o.astype(BF16)

    rhs = [_block_diag2(q_ref[0, :, t * tq:(t + 1) * tq]) for t in range(n_tiles)]
    results, overflow = _flash_fixed_ref(rhs, k_ref, vt_ref, chunks)
    for t, (acc, l) in enumerate(results):
        o_ref[0, :, t * tq:(t + 1) * tq] = finish(acc, l)

    @pl.when(overflow)
    def _recompute():
        def tile(t, carry):
            cols = pl.ds(pl.multiple_of(t * tq, tq), tq)
            o_ref[0, :, cols] = finish(*_flash(_block_diag2(q_ref[0, :, cols]), k_ref, vt_ref, chunks))
            return carry
        lax.fori_loop(0, n_tiles, tile, 0)


def _diff_attention(tok, feat, diff_lambda, subln_g, context):
    if context:
        tq, bq, nq, chunks = L, L, 1, [(0, L)]
        q_spec = pl.BlockSpec((1, 128, L), lambda b, h, i: (b, h, S // L))
        k_spec = pl.BlockSpec((1, L, 128), lambda b, h, i: (b, S // L, h))
        vt_spec = pl.BlockSpec((1, 128, L), lambda b, h, i: (b, 4 + h, S // L))
    else:
        tq, bq, chunks = TQ_DIFF, TQ_DIFF * TILES_DIFF, _latent_chunks(TK_DIFF)
        nq = S // bq
        q_spec = pl.BlockSpec((1, 128, bq), lambda b, h, i: (b, h, i))
        k_spec = pl.BlockSpec((1, N, 128), lambda b, h, i: (b, 0, h))
        vt_spec = pl.BlockSpec((1, 128, N), lambda b, h, i: (b, 4 + h, 0))
    return pl.pallas_call(
        functools.partial(_diff_attn_kernel, chunks=chunks, tq=tq),
        out_shape=jax.ShapeDtypeStruct((B, 512, nq * bq), BF16),
        grid=(B, DIFF_HEADS, nq),
        in_specs=[
            pl.BlockSpec((4, HEAD_DIM), lambda b, h, i: (0, 0)),
            pl.BlockSpec((DIFF_V, 1), lambda b, h, i: (0, 0)),
            q_spec, k_spec, vt_spec,
        ],
        out_specs=pl.BlockSpec((1, 128, bq), lambda b, h, i: (b, h, i)),
        compiler_params=_cparams(3),
        name="diff_attention_ctx" if context else "diff_attention",
    )(diff_lambda, subln_g.reshape(DIFF_V, 1), feat, tok, feat)


def _na_softmax_pv(s_c, vt_ctx, s_w=None, vt_win=None):
    m = jnp.max(s_c, axis=0, keepdims=True)
    if s_w is not None:
        m = jnp.maximum(m, jnp.max(s_w, axis=0, keepdims=True))
    p_c = jnp.exp2(s_c - m)
    l = jnp.sum(p_c, axis=0, keepdims=True)
    o = jnp.dot(vt_ctx, p_c.astype(BF16), preferred_element_type=F32)
    if s_w is not None:
        p_w = jnp.exp2(s_w - m)
        l = l + jnp.sum(p_w, axis=0, keepdims=True)
        o = o + jnp.dot(vt_win, p_w.astype(BF16), preferred_element_type=F32)
    o = o / l
    n = s_c.shape[1] // 2
    return jnp.concatenate([o[:64, :n], o[64:, n:]], axis=0).astype(BF16)


def _na_kernel(q_ref, k_ref, vt_ref, bias_ref, o_ref):
    i = pl.program_id(2)
    k_ctx = k_ref[0, S:, :]
    vt_ctx = vt_ref[0, :, S:]
    n_pairs = TQ_NA // LANE

    def scores(jj):
        r = 2 * (i * n_pairs + jj)
        start = jnp.clip(r - NA_ROWS // 2, 0, ROWS - NA_WIN_ROWS)
        pat = jnp.where(r == 0, 0, jnp.where(r == 2, 1, jnp.where(
            r == ROWS - 4, 3, jnp.where(r == ROWS - 2, 4, 2))))
        off = pl.multiple_of(start * GRID_W, LANE)
        rhs = _block_diag2(q_ref[0, :, jj * LANE:(jj + 1) * LANE])
        s_c = jnp.dot(k_ctx, rhs, preferred_element_type=F32)
        s_w = jnp.dot(k_ref[0, pl.ds(off, NA_WIN), :], rhs, preferred_element_type=F32) + bias_ref[0, pat]
        return s_c, s_w, off

    pending = [scores(jj) for jj in range(min(SCORES_AHEAD, n_pairs))]
    for jj in range(n_pairs):
        s_c, s_w, off = pending.pop(0)
        if jj + SCORES_AHEAD < n_pairs:
            pending.append(scores(jj + SCORES_AHEAD))
        o_ref[0, :, jj * LANE:(jj + 1) * LANE] = _na_softmax_pv(
            s_c, vt_ctx, s_w, vt_ref[0, :, pl.ds(off, NA_WIN)])


def _na_ctx_kernel(q_ref, k_ref, vt_ref, o_ref):
    s_c = jnp.dot(k_ref[0], _block_diag2(q_ref[0]), preferred_element_type=F32)
    o_ref[0] = _na_softmax_pv(s_c, vt_ref[0])


def _na_attention(tok, feat, bias):
    return pl.pallas_call(
        _na_kernel,
        out_shape=jax.ShapeDtypeStruct((B, 512, S), BF16),
        grid=(B, NA_HEADS // 2, S // TQ_NA),
        in_specs=[
            pl.BlockSpec((1, 128, TQ_NA), lambda b, h, i: (b, 8 + h, i)),
            pl.BlockSpec((1, N, 128), lambda b, h, i: (b, 0, 4 + h)),
            pl.BlockSpec((1, 128, N), lambda b, h, i: (b, 12 + h, 0)),
            pl.BlockSpec((1, NA_PATTERNS, NA_WIN, 2 * LANE), lambda b, h, i: (h, 0, 0, 0)),
        ],
        out_specs=pl.BlockSpec((1, 128, TQ_NA), lambda b, h, i: (b, h, i)),
        compiler_params=_cparams(3),
        name="na_attention",
    )(feat, tok, feat, bias)


def _na_attention_ctx(tok, feat):
    return pl.pallas_call(
        _na_ctx_kernel,
        out_shape=jax.ShapeDtypeStruct((B, 512, L), BF16),
        grid=(B, NA_HEADS // 2),
        in_specs=[
            pl.BlockSpec((1, 128, L), lambda b, h: (b, 8 + h, S // L)),
            pl.BlockSpec((1, L, 128), lambda b, h: (b, S // L, 4 + h)),
            pl.BlockSpec((1, 128, L), lambda b, h: (b, 12 + h, S // L)),
        ],
        out_specs=pl.BlockSpec((1, 128, L), lambda b, h: (b, h, 0)),
        compiler_params=_cparams(2),
        name="na_attention_ctx",
    )(feat, tok, feat)


def _na_bias_table(rpb):
    pats = [(0, 0), (2, 0), (8, 4), (ROWS - 4, ROWS - NA_WIN_ROWS), (ROWS - 2, ROWS - NA_WIN_ROWS)]
    n_off = 2 * NA_COLS - 1
    kr_rel = np.arange(NA_WIN_ROWS)
    j = np.arange(2)
    drow = np.zeros((NA_PATTERNS, NA_WIN_ROWS, 2), np.int32)
    row_ok = np.zeros((NA_PATTERNS, NA_WIN_ROWS, 2), bool)
    for p, (r, start) in enumerate(pats):
        rq = r + j
        rs = np.clip(rq - NA_ROWS // 2, 0, ROWS - NA_ROWS)
        kr = start + kr_rel
        row_ok[p] = (kr[:, None] >= rs[None, :]) & (kr[:, None] < rs[None, :] + NA_ROWS)
        drow[p] = np.clip(kr[:, None] - rq[None, :] + NA_ROWS - 1, 0, 2 * NA_ROWS - 2)
    kc = np.arange(GRID_W)
    c = np.arange(GRID_W)
    cs = np.clip(c - NA_COLS // 2, 0, GRID_W - NA_COLS)
    col_ok = (kc[:, None] >= cs[None, :]) & (kc[:, None] < cs[None, :] + NA_COLS)
    dcol = kc[:, None] - c[None, :] + NA_COLS - 1
    hit = (dcol[None] == np.arange(n_off)[:, None, None]) & col_ok[None]
    expand = np.zeros((2, 2, n_off, GRID_W, 2, 2, GRID_W), np.float32)
    for hh in range(2):
        for jj in range(2):
            expand[hh, jj, :, :, hh, jj, :] = hit
    expand = expand.reshape(4 * n_off, GRID_W, 2 * LANE)
    rows = rpb[:, drow.reshape(-1), :]
    rows = rows.reshape(NA_HEADS // 2, 2, NA_PATTERNS * NA_WIN_ROWS, 2, n_off)
    rows = rows.transpose(0, 2, 1, 3, 4).reshape(-1, 4 * n_off)
    t = jnp.einsum("gd,dkl->gkl", rows, jnp.asarray(expand), precision=lax.Precision.HIGHEST)
    valid = np.zeros((NA_PATTERNS, NA_WIN_ROWS, GRID_W, 2, 2, GRID_W), bool)
    valid[:] = row_ok[:, :, None, None, :, None] & col_ok[None, None, :, None, None, :]
    valid = valid.reshape(1, NA_PATTERNS, NA_WIN_ROWS, GRID_W, 2 * LANE)
    t = t.reshape(NA_HEADS // 2, NA_PATTERNS, NA_WIN_ROWS, GRID_W, 2 * LANE)
    t = jnp.where(jnp.asarray(valid), t, NEG) * LOG2E
    return t.reshape(NA_HEADS // 2, NA_PATTERNS, NA_WIN, 2 * LANE)


def _outproj_odd_kernel(a_ref, b_ref, w_ref, h_ref, mv_ref, g_ref, beta_ref, o_ref):
    y = (lax.dot_general(a_ref[0], w_ref[:512, :], (((0,), (0,)), ((), ())), preferred_element_type=F32)
         + jnp.dot(b_ref[0], w_ref[512:, :], preferred_element_type=F32))
    gate = mv_ref[0][2:3]
    o_ref[0] = _layer_norm(ALPHA * h_ref[0] + gate * y, g_ref[...], beta_ref[...])


def _outproj_even_kernel(al_ref, ac_ref, bl_ref, bc_ref, w_ref, x_ref, c_ref, mv_ref, g_ref, beta_ref,
                         o_ref):
    i = pl.program_id(1)
    ta = (((0,), (0,)), ((), ()))

    def project(a_ref, b_ref, h_ref):
        y = (lax.dot_general(a_ref[0], w_ref[:512, :], ta, preferred_element_type=F32)
             + lax.dot_general(b_ref[0], w_ref[512:, :], ta, preferred_element_type=F32))
        gate = mv_ref[0][2:3]
        o_ref[0] = _layer_norm(ALPHA * h_ref[0] + gate * y, g_ref[...], beta_ref[...])

    @pl.when(i < NT_LAT)
    def _latent():
        project(al_ref, bl_ref, x_ref)

    @pl.when(i == NT_LAT)
    def _context():
        project(ac_ref, bc_ref, c_ref)


def _outproj_even(a_lat, a_ctx, b_lat, b_ctx, w, x, ctx, modv, g, beta):
    lat_mix = pl.BlockSpec((1, 512, TM), lambda b, i: (b, 0, jnp.minimum(i, NT_LAT - 1)))
    ctx_mix = pl.BlockSpec((1, 512, L), lambda b, i: (b, 0, 0))
    return pl.pallas_call(
        _outproj_even_kernel,
        out_shape=jax.ShapeDtypeStruct((B, N, D), F32),
        grid=(B, NT_ALL),
        in_specs=[
            lat_mix, ctx_mix, lat_mix, ctx_mix,
            pl.BlockSpec((D, D), lambda b, i: (0, 0)),
            pl.BlockSpec((1, TM, D), _lat_tile),
            pl.BlockSpec((1, L, D), lambda b, i: (b, 0, 0)),
            pl.BlockSpec((1, N_MOD, D), lambda b, i: (_mod_row(b, i), 0, 0)),
            pl.BlockSpec((1, D), lambda b, i: (0, 0)),
            pl.BlockSpec((1, D), lambda b, i: (0, 0)),
        ],
        out_specs=pl.BlockSpec((1, TM, D), lambda b, i: (b, i, 0)),
        compiler_params=_cparams(2),
        name="outproj_even_ln",
    )(a_lat, a_ctx, b_lat, b_ctx, w, x, ctx, modv, g.reshape(1, D), beta.reshape(1, D))


def _outproj_odd(mix_a, mix_b, w, h, modv, g, beta):
    return pl.pallas_call(
        _outproj_odd_kernel,
        out_shape=jax.ShapeDtypeStruct((B, S, D), F32),
        grid=(B, NT_LAT),
        in_specs=[
            pl.BlockSpec((1, 512, TM), lambda b, i: (b, 0, i)),
            pl.BlockSpec((1, TM, 512), lambda b, i: (b, i, 0)),
            pl.BlockSpec((D, D), lambda b, i: (0, 0)),
            pl.BlockSpec((1, TM, D), lambda b, i: (b, i, 0)),
            pl.BlockSpec((1, N_MOD, D), lambda b, i: (b, 0, 0)),
            pl.BlockSpec((1, D), lambda b, i: (0, 0)),
            pl.BlockSpec((1, D), lambda b, i: (0, 0)),
        ],
        out_specs=pl.BlockSpec((1, TM, D), lambda b, i: (b, i, 0)),
        compiler_params=_cparams(2),
        name="outproj_odd_ln",
    )(mix_a, mix_b, w, h, modv, g.reshape(1, D), beta.reshape(1, D))


def _ffn_kernel(h_ref, mv_ref, win_ref, wout_ref, g_ref, beta_ref, o_ref):
    u = _modulated(h_ref, mv_ref, 3)
    ga = jnp.dot(u, win_ref[...], preferred_element_type=F32)
    gt, a = ga[:, :FFN_HIDDEN], ga[:, FFN_HIDDEN:]
    act = (gt * (1.0 / (1.0 + jnp.exp(-gt))) * a).astype(BF16)
    y = jnp.dot(act, wout_ref[...], preferred_element_type=F32)
    gate = mv_ref[0][5:6]
    o_ref[0] = _layer_norm(ALPHA * h_ref[0] + gate * y, g_ref[...], beta_ref[...])


def _ffn(h, modv, w_in, w_out, g, beta, n_tiles):
    return pl.pallas_call(
        _ffn_kernel,
        out_shape=jax.ShapeDtypeStruct((B, n_tiles * TM, D), F32),
        grid=(B, n_tiles),
        in_specs=[
            pl.BlockSpec((1, TM, D), lambda b, i: (b, i, 0)),
            pl.BlockSpec((1, N_MOD, D), lambda b, i: (_mod_row(b, i), 0, 0)),
            pl.BlockSpec((D, 2 * FFN_HIDDEN), lambda b, i: (0, 0)),
            pl.BlockSpec((FFN_HIDDEN, D), lambda b, i: (0, 0)),
            pl.BlockSpec((1, D), lambda b, i: (0, 0)),
            pl.BlockSpec((1, D), lambda b, i: (0, 0)),
        ],
        out_specs=pl.BlockSpec((1, TM, D), lambda b, i: (b, i, 0)),
        compiler_params=_cparams(2),
        name="ffn_ln",
    )(h, modv, w_in, w_out, g.reshape(1, D), beta.reshape(1, D))


def _gelu(x):
    return 0.5 * x * (1.0 + lax.erf(x * (1.0 / math.sqrt(2.0))))


def _rms(x, g):
    ms = jnp.mean(x * x, axis=-1, keepdims=True)
    return x * lax.rsqrt(ms + RMS_EPS) * g


def _inproj_odd_kernel(h_ref, mv_ref, win_ref, gq_ref, gkv_ref, wuq_ref, wkn_ref, wkr_ref, wv_ref,
                       cosf_ref, sinf_ref, cost_ref, sina_ref, sinb_ref,
                       lng_ref, lnb_ref, ws_ref, gb_ref,
                       q_ref, k_ref, vt_ref, gm_ref):
    nt = (((1,), (1,)), ((), ()))
    u = _modulated(h_ref, mv_ref, 0)
    p = jnp.dot(u, win_ref[...], preferred_element_type=F32)
    cq = _rms(p[:, :MLA_Q_RANK], gq_ref[...]).astype(BF16)
    ckv = _rms(p[:, MLA_Q_RANK:MLA_Q_RANK + MLA_KV_RANK], gkv_ref[...]).astype(BF16)
    kr = p[:, 640:768]
    qt = lax.dot_general(wuq_ref[...], cq, nt, preferred_element_type=F32)
    qt = qt * ((MLA_NOPE + MLA_ROPE) ** -0.5 * LOG2E)
    cf, sf = cosf_ref[...], sinf_ref[...]
    for hd in range(MLA_HEADS):
        base = hd * 128
        parts = [qt[base:base + 64]]
        for half in range(2):
            x1 = qt[base + 64 + half * 16:base + 72 + half * 16]
            x2 = qt[base + 72 + half * 16:base + 80 + half * 16]
            c = cf[half * 8:(half + 1) * 8]
            s = sf[half * 8:(half + 1) * 8]
            parts += [x1 * c - x2 * s, x2 * c + x1 * s]
        parts.append(qt[base + 96:base + 128])
        q_ref[0, base:base + 128, :] = jnp.concatenate(parts, axis=0).astype(BF16)
    kr = kr * cost_ref[...] + pltpu.roll(kr, LANE - 8, 1) * sina_ref[...] + pltpu.roll(kr, 8, 1) * sinb_ref[...]
    k = (jnp.dot(ckv, wkn_ref[...], preferred_element_type=F32)
         + jnp.dot(kr.astype(BF16), wkr_ref[...], preferred_element_type=F32))
    k_ref[0] = k.astype(BF16)
    vt_ref[0] = lax.dot_general(wv_ref[...], ckv, nt, preferred_element_type=F32).astype(BF16)
    gu = _gelu(p[:, 768:1280])
    gv = _layer_norm(_gelu(p[:, 1280:1792]), lng_ref[...], lnb_ref[...]).astype(BF16)
    for ch in range(TM // GMLP_CHUNK):
        rows = slice(ch * GMLP_CHUNK, (ch + 1) * GMLP_CHUNK)
        for grp in range(GMLP_GROUPS):
            cols = slice(grp * GMLP_CH, (grp + 1) * GMLP_CH)
            mixed = jnp.dot(ws_ref[grp], gv[rows, cols], preferred_element_type=F32) + gb_ref[grp]
            gm_ref[0, rows, cols] = (gu[rows, cols] * mixed).astype(BF16)


def _inproj_odd(h, modv, w, tabs):
    cosf, sinf, cost, sina, sinb = tabs
    const2 = lambda b, i: (0, 0)
    const3 = lambda b, i: (0, 0, 0)
    return pl.pallas_call(
        _inproj_odd_kernel,
        out_shape=(jax.ShapeDtypeStruct((B, 1024, N), BF16),
                   jax.ShapeDtypeStruct((B, N, 1024), BF16),
                   jax.ShapeDtypeStruct((B, 512, N), BF16),
                   jax.ShapeDtypeStruct((B, N, 512), BF16)),
        grid=(B, NT_ALL),
        in_specs=[
            pl.BlockSpec((1, TM, D), lambda b, i: (b, i, 0)),
            pl.BlockSpec((1, N_MOD, D), lambda b, i: (_mod_row(b, i), 0, 0)),
            pl.BlockSpec((D, 1792), const2),
            pl.BlockSpec((1, MLA_Q_RANK), const2),
            pl.BlockSpec((1, MLA_KV_RANK), const2),
            pl.BlockSpec((1024, MLA_Q_RANK), const2),
            pl.BlockSpec((MLA_KV_RANK, 1024), const2),
            pl.BlockSpec((LANE, 1024), const2),
            pl.BlockSpec((512, MLA_KV_RANK), const2),
            pl.BlockSpec((16, TM), lambda b, i: (0, i)),
            pl.BlockSpec((16, TM), lambda b, i: (0, i)),
            pl.BlockSpec((TM, LANE), lambda b, i: (i, 0)),
            pl.BlockSpec((TM, LANE), lambda b, i: (i, 0)),
            pl.BlockSpec((TM, LANE), lambda b, i: (i, 0)),
            pl.BlockSpec((1, GMLP_WIDTH), const2),
            pl.BlockSpec((1, GMLP_WIDTH), const2),
            pl.BlockSpec((GMLP_GROUPS, GMLP_CHUNK, GMLP_CHUNK), const3),
            pl.BlockSpec((GMLP_GROUPS, GMLP_CHUNK, 1), const3),
        ],
        out_specs=(pl.BlockSpec((1, 1024, TM), lambda b, i: (b, 0, i)),
                   pl.BlockSpec((1, TM, 1024), lambda b, i: (b, i, 0)),
                   pl.BlockSpec((1, 512, TM), lambda b, i: (b, 0, i)),
                   pl.BlockSpec((1, TM, 512), lambda b, i: (b, i, 0))),
        compiler_params=_cparams(2),
        name="inproj_odd",
    )(h, modv, w["win"], w["gq"], w["gkv"], w["wuq"], w["wkn"], w["wkr"], w["wv"],
      cosf, sinf, cost, sina, sinb, w["lng"], w["lnb"], w["ws"], w["gb"])


def _mla_attn_kernel(q_ref, k_ref, vt_ref, o_ref):
    chunks = _latent_chunks(TK_MLA)
    rhs = [q_ref[0, :, t * TQ_MLA:(t + 1) * TQ_MLA] for t in range(TILES_MLA)]
    results, overflow = _flash_fixed_ref(rhs, k_ref, vt_ref, chunks)
    for t, (acc, l) in enumerate(results):
        o_ref[0, :, t * TQ_MLA:(t + 1) * TQ_MLA] = (acc / l).astype(BF16)

    @pl.when(overflow)
    def _recompute():
        def tile(t, carry):
            cols = pl.ds(pl.multiple_of(t * TQ_MLA, TQ_MLA), TQ_MLA)
            acc, l = _flash(q_ref[0, :, cols], k_ref, vt_ref, chunks)
            o_ref[0, :, cols] = (acc / l).astype(BF16)
            return carry
        lax.fori_loop(0, TILES_MLA, tile, 0)


def _mla_attention(q_t, k, v_t):
    bq = TQ_MLA * TILES_MLA
    return pl.pallas_call(
        _mla_attn_kernel,
        out_shape=jax.ShapeDtypeStruct((B, 512, S), BF16),
        grid=(B, MLA_HEADS, S // bq),
        in_specs=[
            pl.BlockSpec((1, 128, bq), lambda b, h, i: (b, h, i)),
            pl.BlockSpec((1, N, 128), lambda b, h, i: (b, 0, h)),
            pl.BlockSpec((1, MLA_V, N), lambda b, h, i: (b, h, 0)),
        ],
        out_specs=pl.BlockSpec((1, MLA_V, bq), lambda b, h, i: (b, h, i)),
        compiler_params=_cparams(3),
        name="mla_attention",
    )(q_t, k, v_t)


def _rope_tables(n_freq):
    t = jnp.arange(S)
    inv = jnp.power(ROPE_BASE, -jnp.arange(0, 2 * n_freq, 2, dtype=F32) / (2 * n_freq))
    ang_r = (t // GRID_W).astype(F32)[:, None] * inv[None, :]
    ang_c = (t % GRID_W).astype(F32)[:, None] * inv[None, :]
    cos = jnp.concatenate([jnp.cos(ang_r), jnp.cos(ang_c)], axis=1)
    sin = jnp.concatenate([jnp.sin(ang_r), jnp.sin(ang_c)], axis=1)
    cos = jnp.concatenate([cos, jnp.ones((L, 2 * n_freq), F32)], axis=0)
    sin = jnp.concatenate([sin, jnp.zeros((L, 2 * n_freq), F32)], axis=0)
    lane = np.arange(LANE)
    period = 4 * n_freq
    src = ((lane % period) // (2 * n_freq)) * n_freq + lane % n_freq
    is_x1 = (lane % (2 * n_freq)) < n_freq
    cos_t = cos[:, src]
    sin_t = sin[:, src]
    sin_a = jnp.where(jnp.asarray(is_x1)[None, :], -sin_t, 0.0)
    sin_b = jnp.where(jnp.asarray(is_x1)[None, :], 0.0, sin_t)
    return cos.T, sin.T, cos_t, sin_a, sin_b


def _even_weights(ev_w_in):
    aq, ak, av, bq, bk, bv = jnp.split(ev_w_in, [512, 1024, 1536, 2048, 2560], axis=1)
    scale = HEAD_DIM ** -0.5
    wtok = jnp.concatenate([ak, bk], axis=1).astype(BF16)
    wfeat = jnp.concatenate([aq * scale, av, bq * scale, bv], axis=1).T.astype(BF16)
    return wtok, wfeat


def _odd_weights(od_w_in, gq, w_uq, gkv, w_ukv, ln_g, ln_b, ws, gb):
    cq, ckv, kr, gu, gv = jnp.split(od_w_in, [384, 640, 672, 1184], axis=1)
    win = jnp.concatenate([cq, ckv, kr, jnp.zeros((D, LANE - MLA_ROPE), F32), gu, gv], axis=1)
    dq = MLA_NOPE + MLA_ROPE
    wuq = w_uq.reshape(MLA_Q_RANK, MLA_HEADS, dq)
    wuq = jnp.pad(wuq, ((0, 0), (0, 0), (0, LANE - dq))).reshape(MLA_Q_RANK, MLA_HEADS * LANE)
    wukv = w_ukv.reshape(MLA_KV_RANK, MLA_HEADS, MLA_NOPE + MLA_V)
    wkn = jnp.pad(wukv[:, :, :MLA_NOPE], ((0, 0), (0, 0), (0, LANE - MLA_NOPE)))
    wkn = wkn.reshape(MLA_KV_RANK, MLA_HEADS * LANE)
    wv = wukv[:, :, MLA_NOPE:].reshape(MLA_KV_RANK, MLA_HEADS * MLA_V)
    place = np.zeros((LANE, MLA_HEADS * LANE), np.float32)
    for hd in range(MLA_HEADS):
        place[np.arange(MLA_ROPE), hd * LANE + MLA_NOPE + np.arange(MLA_ROPE)] = 1.0
    return {
        "win": win.astype(BF16), "gq": gq.reshape(1, -1), "gkv": gkv.reshape(1, -1),
        "wuq": wuq.T.astype(BF16), "wkn": wkn.astype(BF16), "wkr": jnp.asarray(place, BF16),
        "wv": wv.T.astype(BF16), "lng": ln_g.reshape(1, -1), "lnb": ln_b.reshape(1, -1),
        "ws": ws.astype(BF16), "gb": gb.reshape(GMLP_GROUPS, GMLP_CHUNK, 1),
    }


def kernel(x, c, ctx, c_ctx, mod_w, mod_b, ln_mix_g, ln_mix_b, ln_ffn_g, ln_ffn_b, ffn_w_in, ffn_w_out,
           ev_w_in, ev_w_out, diff_lambda, diff_subln_g, na_rpb, od_w_in, od_w_out, mla_q_norm_g,
           mla_w_uq, mla_kv_norm_g, mla_w_ukv, gmlp_ln_g, gmlp_ln_b, gmlp_ws, gmlp_b):
    cond = jnp.concatenate([c, c_ctx[None], jnp.zeros((8 - B - 1, D), F32)], axis=0)
    mod = _modulation(cond, mod_w, mod_b).reshape(DEPTH, 8, N_MOD, D)

    wtok, wfeat = _even_weights(ev_w_in[0])
    cosf, sinf, cost, sina, sinb = _rope_tables(16)
    tok, feat = _inproj_even(x, ctx, mod[0], wtok, wfeat,
                             (cosf * LOG2E, sinf * LOG2E, cost, sina, sinb))
    h = _outproj_even(_diff_attention(tok, feat, diff_lambda[0], diff_subln_g[0], False),
                      _diff_attention(tok, feat, diff_lambda[0], diff_subln_g[0], True),
                      _na_attention(tok, feat, _na_bias_table(na_rpb[0])),
                      _na_attention_ctx(tok, feat),
                      ev_w_out[0].astype(BF16), x, ctx, mod[0], ln_mix_g[0], ln_mix_b[0])
    h = _ffn(h, mod[0], ffn_w_in[0].astype(BF16), ffn_w_out[0].astype(BF16),
             ln_ffn_g[0], ln_ffn_b[0], NT_ALL)

    w1 = _odd_weights(od_w_in[0], mla_q_norm_g[0], mla_w_uq[0], mla_kv_norm_g[0], mla_w_ukv[0],
                      gmlp_ln_g[0], gmlp_ln_b[0], gmlp_ws[0], gmlp_b[0])
    q_t, k, v_t, gm = _inproj_odd(h, mod[1], w1, _rope_tables(8))
    mix_a = _mla_attention(q_t, k, v_t)
    h = _outproj_odd(mix_a, gm, od_w_out[0].astype(BF16), h, mod[1], ln_mix_g[1], ln_mix_b[1])
    return _ffn(h, mod[1], ffn_w_in[1].astype(BF16), ffn_w_out[1].astype(BF16),
                ln_ffn_g[1], ln_ffn_b[1], NT_LAT)
```

```python
import functools
import math

import jax
import jax.numpy as jnp
import numpy as np
from jax import lax
from jax.experimental import pallas as pl
from jax.experimental.pallas import tpu as pltpu

F32 = jnp.float32
BF16 = jnp.bfloat16

D = 1024
B = 2
S = 8192
L = 256
N = S + L
DEPTH = 2
GRID_W = 64
ROWS = S // GRID_W
ALPHA = (2 * DEPTH) ** 0.25
ROPE_BASE = 10000.0
LN_EPS = 1e-5
RMS_EPS = 1e-6
N_MOD = 6
HEAD_DIM = 64
DIFF_HEADS = 4
DIFF_V = 128
NA_HEADS = 8
NA_ROWS = 8
NA_COLS = 16
MLA_HEADS = 8
MLA_Q_RANK = 384
MLA_KV_RANK = 256
MLA_NOPE = 64
MLA_ROPE = 32
MLA_V = 64
GMLP_GROUPS = 4
GMLP_CH = 128
GMLP_CHUNK = 128
GMLP_WIDTH = 512
FFN_HIDDEN = 2816
LAMBDA_INIT_0 = 0.8 - 0.6 * math.exp(-0.3 * 0)
LOG2E = math.log2(math.e)

LANE = 128
TM = 256
TM_ALL = 3 * TM
TM_LAT = 2 * TM
NT_ALL = N // TM
NT_LAT = S // TM
TQ_DIFF, TK_DIFF = 256, 256
TQ_MLA, TK_MLA = 512, 256
TILES_DIFF, TILES_MLA = 4, 4
TQ_NA = 1024
NA_WIN_ROWS = 10
NA_WIN = NA_WIN_ROWS * GRID_W
NA_PATTERNS = 5
NEG = -1e30
VMEM_LIMIT = 56 * 1024 * 1024


def _cparams(n_axes):
    return pltpu.CompilerParams(dimension_semantics=("parallel",) * n_axes,
                                vmem_limit_bytes=VMEM_LIMIT)


def _mod_row(b, i):
    return jnp.where(i < NT_LAT, b, 2)


def _mod_kernel(cond_ref, w_ref, b_ref, o_ref):
    cnd = cond_ref[...]
    act = cnd * (1.0 / (1.0 + jnp.exp(-cnd)))
    o_ref[0] = jnp.dot(act, w_ref[0], preferred_element_type=F32) + b_ref[0]


def _modulation(cond, mod_w, mod_b):
    tn = 1536
    return pl.pallas_call(
        _mod_kernel,
        out_shape=jax.ShapeDtypeStruct((DEPTH, 8, N_MOD * D), F32),
        grid=(DEPTH, N_MOD * D // tn),
        in_specs=[
            pl.BlockSpec((8, D), lambda l, j: (0, 0)),
            pl.BlockSpec((1, D, tn), lambda l, j: (l, 0, j)),
            pl.BlockSpec((1, 1, tn), lambda l, j: (l, 0, j)),
        ],
        out_specs=pl.BlockSpec((1, 8, tn), lambda l, j: (l, 0, j)),
        compiler_params=_cparams(2),
        name="modulation",
    )(cond, mod_w, mod_b.reshape(DEPTH, 1, N_MOD * D))


def _modulated(h_ref, mv_ref, k):
    mv = mv_ref[0]
    return (h_ref[0] * (1.0 + mv[k + 1:k + 2]) + mv[k:k + 1]).astype(BF16)


def _tile_mod(mvb_ref, mvc_ref, j, n_sub):
    if mvc_ref is None or j != n_sub - 1:
        return mvb_ref[0]
    is_ctx = pl.program_id(1) == pl.num_programs(1) - 1
    return jnp.where(is_ctx, mvc_ref[0], mvb_ref[0])


def _layer_norm(x, g, b):
    mu = jnp.mean(x, axis=-1, keepdims=True)
    xc = x - mu
    var = jnp.mean(xc * xc, axis=-1, keepdims=True)
    return xc * lax.rsqrt(var + LN_EPS) * g + b


def _inproj_even_kernel(x_ref, c_ref, mv_ref, wtok_ref, wfeat_ref, cosf_ref, sinf_ref,
                        cost_ref, sina_ref, sinb_ref, tok_ref, feat_ref, u_ref):
    i = pl.program_id(1)

    @pl.when(i < NT_LAT)
    def _latent():
        u_ref[...] = _modulated(x_ref, mv_ref, 0)

    @pl.when(i == NT_LAT)
    def _context():
        u_ref[...] = _modulated(c_ref, mv_ref, 0)

    u = u_ref[...]
    tok = jnp.dot(u, wtok_ref[...], preferred_element_type=F32)
    ct, sa, sb = cost_ref[...], sina_ref[...], sinb_ref[...]
    for j in range(4):
        xs = tok[:, j * LANE:(j + 1) * LANE]
        ro = xs * ct + pltpu.roll(xs, LANE - 16, 1) * sa + pltpu.roll(xs, 16, 1) * sb
        tok_ref[0, :, j * LANE:(j + 1) * LANE] = ro.astype(BF16)
    tok_ref[0, :, 512:] = tok[:, 512:].astype(BF16)
    feat = lax.dot_general(wfeat_ref[...], u, (((1,), (1,)), ((), ())),
                           preferred_element_type=F32)
    cf, sf = cosf_ref[...], sinf_ref[...]
    for blk in range(8):
        for half in range(2):
            base = blk * 64 + half * 32
            x1 = feat[base:base + 16]
            x2 = feat[base + 16:base + 32]
            c = cf[half * 16:(half + 1) * 16]
            s = sf[half * 16:(half + 1) * 16]
            feat_ref[0, base:base + 16, :] = (x1 * c - x2 * s).astype(BF16)
            feat_ref[0, base + 16:base + 32, :] = (x2 * c + x1 * s).astype(BF16)
    feat_ref[0, 512:1024, :] = feat[512:1024].astype(BF16)
    feat_ref[0, 1024:1536, :] = (feat[1024:1536] * LOG2E).astype(BF16)
    feat_ref[0, 1536:, :] = feat[1536:].astype(BF16)


def _lat_tile(b, i):
    return (b, jnp.minimum(i, NT_LAT - 1), 0)


def _inproj_even(x, ctx, modv, wtok, wfeat, tabs):
    cosf, sinf, cost, sina, sinb = tabs
    const = lambda b, i: (0, 0)
    return pl.pallas_call(
        _inproj_even_kernel,
        out_shape=(jax.ShapeDtypeStruct((B, N, 1024), BF16),
                   jax.ShapeDtypeStruct((B, 2048, N), BF16)),
        grid=(B, NT_ALL),
        in_specs=[
            pl.BlockSpec((1, TM, D), _lat_tile),
            pl.BlockSpec((1, L, D), lambda b, i: (b, 0, 0)),
            pl.BlockSpec((1, N_MOD, D), lambda b, i: (_mod_row(b, i), 0, 0)),
            pl.BlockSpec((D, 1024), const),
            pl.BlockSpec((2048, D), const),
            pl.BlockSpec((32, TM), lambda b, i: (0, i)),
            pl.BlockSpec((32, TM), lambda b, i: (0, i)),
            pl.BlockSpec((TM, LANE), lambda b, i: (i, 0)),
            pl.BlockSpec((TM, LANE), lambda b, i: (i, 0)),
            pl.BlockSpec((TM, LANE), lambda b, i: (i, 0)),
        ],
        out_specs=(pl.BlockSpec((1, TM, 1024), lambda b, i: (b, i, 0)),
                   pl.BlockSpec((1, 2048, TM), lambda b, i: (b, 0, i))),
        scratch_shapes=[pltpu.VMEM((TM, D), BF16)],
        compiler_params=_cparams(2),
        name="inproj_even",
    )(x, ctx, modv, wtok, wfeat, cosf, sinf, cost, sina, sinb)


def _block_diag2(q_t):
    n = q_t.shape[1]
    z = jnp.zeros((64, n), q_t.dtype)
    left = jnp.concatenate([q_t[:64], z], axis=0)
    right = jnp.concatenate([z, q_t[64:]], axis=0)
    return jnp.concatenate([left, right], axis=1)


ONES_ROWS = 16


def _flash(rhs, k_ref, vt_ref, chunks):
    width = rhs.shape[1]
    dv = vt_ref.shape[1]
    m = jnp.full((1, width), -jnp.inf, F32)
    acc = jnp.zeros((dv + ONES_ROWS, width), F32)

    def scores(idx):
        st, sz = chunks[idx]
        return jnp.dot(k_ref[0, st:st + sz, :], rhs, preferred_element_type=F32)

    s_next = scores(0)
    for idx, (st, sz) in enumerate(chunks):
        s = s_next
        if idx + 1 < len(chunks):
            s_next = scores(idx + 1)
        m_new = jnp.maximum(m, jnp.max(s, axis=0, keepdims=True))
        alpha = jnp.exp2(m - m_new)
        p = jnp.exp2(s - m_new).astype(BF16)
        vt1 = jnp.concatenate([vt_ref[0, :, st:st + sz], jnp.ones((ONES_ROWS, sz), BF16)], axis=0)
        acc = acc * alpha + jnp.dot(vt1, p, preferred_element_type=F32)
        m = m_new
    return acc[:dv], acc[dv:dv + 1]


SCORES_AHEAD = 2
P_LIMIT = 2.0 ** 64


def _flash_fixed_ref(rhs_tiles, k_ref, vt_ref, chunks):
    n = len(chunks)
    items = [(t, idx) for t in range(len(rhs_tiles)) for idx in range(n)]

    def scores(item):
        t, idx = item
        st, sz = chunks[idx]
        return jnp.dot(k_ref[0, st:st + sz, :], rhs_tiles[t], preferred_element_type=F32)

    pending = [scores(it) for it in items[:SCORES_AHEAD]]
    results = []
    worst = None
    for j, (t, idx) in enumerate(items):
        s = pending.pop(0)
        if j + SCORES_AHEAD < len(items):
            pending.append(scores(items[j + SCORES_AHEAD]))
        width = s.shape[1]
        if idx == 0:
            m_ref = jnp.max(s, axis=0, keepdims=True)
        p = jnp.exp2(s - m_ref)
        l8_new = jnp.sum(p.reshape(-1, 8, width), axis=0)
        st, sz = chunks[idx]
        pv = jnp.dot(vt_ref[0, :, st:st + sz], p.astype(BF16), preferred_element_type=F32)
        l8, acc = (l8_new, pv) if idx == 0 else (l8 + l8_new, acc + pv)
        if idx == n - 1:
            l = jnp.sum(l8, axis=0, keepdims=True)
            results.append((acc, l))
            l_max = jnp.max(l)
            worst = l_max if worst is None else jnp.maximum(worst, l_max)
    return results, worst > P_LIMIT


def _latent_chunks(tk):
    return [(j * tk, tk) for j in range(S // tk)] + [(S, L)]


def _diff_attn_kernel(lam_ref, g_ref, q_ref, k_ref, vt_ref, o_ref, *, chunks, tq):
    n_tiles = q_ref.shape[2] // tq

    def finish(acc, l):
        lf = lam_ref[...]
        lam = (jnp.exp(jnp.sum(lf[0:1] * lf[1:2], axis=1, keepdims=True))
               - jnp.exp(jnp.sum(lf[2:3] * lf[3:4], axis=1, keepdims=True)) + LAMBDA_INIT_0)
        o = acc[:, :tq] / l[:, :tq] - lam * (acc[:, tq:] / l[:, tq:])
        ms = jnp.mean(o * o, axis=0, keepdims=True)
        o = o * lax.rsqrt(ms + RMS_EPS) * g_ref[...] * (1.0 - LAMBDA_INIT_0)
        return o.astype(BF16)

    rhs = [_block_diag2(q_ref[0, :, t * tq:(t + 1) * tq]) for t in range(n_tiles)]
    results, overflow = _flash_fixed_ref(rhs, k_ref, vt_ref, chunks)
    for t, (acc, l) in enumerate(results):
        o_ref[0, :, t * tq:(t + 1) * tq] = finish(acc, l)

    @pl.when(overflow)
    def _recompute():
        def tile(t, carry):
            cols = pl.ds(pl.multiple_of(t * tq, tq), tq)
            o_ref[0, :, cols] = finish(*_flash(_block_diag2(q_ref[0, :, cols]), k_ref, vt_ref, chunks))
            return carry
        lax.fori_loop(0, n_tiles, tile, 0)


def _diff_attention(tok, feat, diff_lambda, subln_g, context):
    if context:
        tq, bq, nq, chunks = L, L, 1, [(0, L)]
        q_spec = pl.BlockSpec((1, 128, L), lambda b, h, i: (b, h, S // L))
        k_spec = pl.BlockSpec((1, L, 128), lambda b, h, i: (b, S // L, h))
        vt_spec = pl.BlockSpec((1, 128, L), lambda b, h, i: (b, 4 + h, S // L))
    else:
        tq, bq, chunks = TQ_DIFF, TQ_DIFF * TILES_DIFF, _latent_chunks(TK_DIFF)
        nq = S // bq
        q_spec = pl.BlockSpec((1, 128, bq), lambda b, h, i: (b, h, i))
        k_spec = pl.BlockSpec((1, N, 128), lambda b, h, i: (b, 0, h))
        vt_spec = pl.BlockSpec((1, 128, N), lambda b, h, i: (b, 4 + h, 0))
    return pl.pallas_call(
        functools.partial(_diff_attn_kernel, chunks=chunks, tq=tq),
        out_shape=jax.ShapeDtypeStruct((B, 512, nq * bq), BF16),
        grid=(B, DIFF_HEADS, nq),
        in_specs=[
            pl.BlockSpec((4, HEAD_DIM), lambda b, h, i: (0, 0)),
            pl.BlockSpec((DIFF_V, 1), lambda b, h, i: (0, 0)),
            q_spec, k_spec, vt_spec,
        ],
        out_specs=pl.BlockSpec((1, 128, bq), lambda b, h, i: (b, h, i)),
        compiler_params=_cparams(3),
        name="diff_attention_ctx" if context else "diff_attention",
    )(diff_lambda, subln_g.reshape(DIFF_V, 1), feat, tok, feat)


def _na_softmax_pv(s_c, vt_ctx, s_w=None, vt_win=None):
    m = jnp.max(s_c, axis=0, keepdims=True)
    if s_w is not None:
        m = jnp.maximum(m, jnp.max(s_w, axis=0, keepdims=True))
    p_c = jnp.exp2(s_c - m)
    l = jnp.sum(p_c, axis=0, keepdims=True)
    o = jnp.dot(vt_ctx, p_c.astype(BF16), preferred_element_type=F32)
    if s_w is not None:
        p_w = jnp.exp2(s_w - m)
        l = l + jnp.sum(p_w, axis=0, keepdims=True)
        o = o + jnp.dot(vt_win, p_w.astype(BF16), preferred_element_type=F32)
    o = o / l
    n = s_c.shape[1] // 2
    return jnp.concatenate([o[:64, :n], o[64:, n:]], axis=0).astype(BF16)


def _na_kernel(q_ref, k_ref, vt_ref, bias_ref, o_ref):
    i = pl.program_id(2)
    k_ctx = k_ref[0, S:, :]
    vt_ctx = vt_ref[0, :, S:]
    n_pairs = TQ_NA // LANE

    def scores(jj):
        r = 2 * (i * n_pairs + jj)
        start = jnp.clip(r - NA_ROWS // 2, 0, ROWS - NA_WIN_ROWS)
        pat = jnp.where(r == 0, 0, jnp.where(r == 2, 1, jnp.where(
            r == ROWS - 4, 3, jnp.where(r == ROWS - 2, 4, 2))))
        off = pl.multiple_of(start * GRID_W, LANE)
        rhs = _block_diag2(q_ref[0, :, jj * LANE:(jj + 1) * LANE])
        s_c = jnp.dot(k_ctx, rhs, preferred_element_type=F32)
        s_w = jnp.dot(k_ref[0, pl.ds(off, NA_WIN), :], rhs, preferred_element_type=F32) + bias_ref[0, pat]
        return s_c, s_w, off

    pending = [scores(jj) for jj in range(min(SCORES_AHEAD, n_pairs))]
    for jj in range(n_pairs):
        s_c, s_w, off = pending.pop(0)
        if jj + SCORES_AHEAD < n_pairs:
            pending.append(scores(jj + SCORES_AHEAD))
        o_ref[0, :, jj * LANE:(jj + 1) * LANE] = _na_softmax_pv(
            s_c, vt_ctx, s_w, vt_ref[0, :, pl.ds(off, NA_WIN)])


def _na_ctx_kernel(q_ref, k_ref, vt_ref, o_ref):
    s_c = jnp.dot(k_ref[0], _block_diag2(q_ref[0]), preferred_element_type=F32)
    o_ref[0] = _na_softmax_pv(s_c, vt_ref[0])


def _na_attention(tok, feat, bias):
    return pl.pallas_call(
        _na_kernel,
        out_shape=jax.ShapeDtypeStruct((B, 512, S), BF16),
        grid=(B, NA_HEADS // 2, S // TQ_NA),
        in_specs=[
            pl.BlockSpec((1, 128, TQ_NA), lambda b, h, i: (b, 8 + h, i)),
            pl.BlockSpec((1, N, 128), lambda b, h, i: (b, 0, 4 + h)),
            pl.BlockSpec((1, 128, N), lambda b, h, i: (b, 12 + h, 0)),
            pl.BlockSpec((1, NA_PATTERNS, NA_WIN, 2 * LANE), lambda b, h, i: (h, 0, 0, 0)),
        ],
        out_specs=pl.BlockSpec((1, 128, TQ_NA), lambda b, h, i: (b, h, i)),
        compiler_params=_cparams(3),
        name="na_attention",
    )(feat, tok, feat, bias)


def _na_attention_ctx(tok, feat):
    return pl.pallas_call(
        _na_ctx_kernel,
        out_shape=jax.ShapeDtypeStruct((B, 512, L), BF16),
        grid=(B, NA_HEADS // 2),
        in_specs=[
            pl.BlockSpec((1, 128, L), lambda b, h: (b, 8 + h, S // L)),
            pl.BlockSpec((1, L, 128), lambda b, h: (b, S // L, 4 + h)),
            pl.BlockSpec((1, 128, L), lambda b, h: (b, 12 + h, S // L)),
        ],
        out_specs=pl.BlockSpec((1, 128, L), lambda b, h: (b, h, 0)),
        compiler_params=_cparams(2),
        name="na_attention_ctx",
    )(feat, tok, feat)


def _na_bias_table(rpb):
    pats = [(0, 0), (2, 0), (8, 4), (ROWS - 4, ROWS - NA_WIN_ROWS), (ROWS - 2, ROWS - NA_WIN_ROWS)]
    n_off = 2 * NA_COLS - 1
    kr_rel = np.arange(NA_WIN_ROWS)
    j = np.arange(2)
    drow = np.zeros((NA_PATTERNS, NA_WIN_ROWS, 2), np.int32)
    row_ok = np.zeros((NA_PATTERNS, NA_WIN_ROWS, 2), bool)
    for p, (r, start) in enumerate(pats):
        rq = r + j
        rs = np.clip(rq - NA_ROWS // 2, 0, ROWS - NA_ROWS)
        kr = start + kr_rel
        row_ok[p] = (kr[:, None] >= rs[None, :]) & (kr[:, None] < rs[None, :] + NA_ROWS)
        drow[p] = np.clip(kr[:, None] - rq[None, :] + NA_ROWS - 1, 0, 2 * NA_ROWS - 2)
    kc = np.arange(GRID_W)
    c = np.arange(GRID_W)
    cs = np.clip(c - NA_COLS // 2, 0, GRID_W - NA_COLS)
    col_ok = (kc[:, None] >= cs[None, :]) & (kc[:, None] < cs[None, :] + NA_COLS)
    dcol = kc[:, None] - c[None, :] + NA_COLS - 1
    hit = (dcol[None] == np.arange(n_off)[:, None, None]) & col_ok[None]
    expand = np.zeros((2, 2, n_off, GRID_W, 2, 2, GRID_W), np.float32)
    for hh in range(2):
        for jj in range(2):
            expand[hh, jj, :, :, hh, jj, :] = hit
    expand = expand.reshape(4 * n_off, GRID_W, 2 * LANE)
    rows = rpb[:, drow.reshape(-1), :]
    rows = rows.reshape(NA_HEADS // 2, 2, NA_PATTERNS * NA_WIN_ROWS, 2, n_off)
    rows = rows.transpose(0, 2, 1, 3, 4).reshape(-1, 4 * n_off)
    t = jnp.einsum("gd,dkl->gkl", rows, jnp.asarray(expand), precision=lax.Precision.HIGHEST)
    valid = np.zeros((NA_PATTERNS, NA_WIN_ROWS, GRID_W, 2, 2, GRID_W), bool)
    valid[:] = row_ok[:, :, None, None, :, None] & col_ok[None, None, :, None, None, :]
    valid = valid.reshape(1, NA_PATTERNS, NA_WIN_ROWS, GRID_W, 2 * LANE)
    t = t.reshape(NA_HEADS // 2, NA_PATTERNS, NA_WIN_ROWS, GRID_W, 2 * LANE)
    t = jnp.where(jnp.asarray(valid), t, NEG) * LOG2E
    return t.reshape(NA_HEADS // 2, NA_PATTERNS, NA_WIN, 2 * LANE)


def _outproj_odd_kernel(a_ref, b_ref, w_ref, h_ref, mv_ref, g_ref, beta_ref, o_ref):
    gate = mv_ref[0][2:3]
    for j in range(TM_LAT // TM):
        rows = slice(j * TM, (j + 1) * TM)
        y = (lax.dot_general(a_ref[0, :, rows], w_ref[:512, :], (((0,), (0,)), ((), ())),
                             preferred_element_type=F32)
             + jnp.dot(b_ref[0, rows, :], w_ref[512:, :], preferred_element_type=F32))
        o_ref[0, rows, :] = _layer_norm(ALPHA * h_ref[0, rows, :] + gate * y, g_ref[...], beta_ref[...])


def _outproj_even_kernel(al_ref, ac_ref, bl_ref, bc_ref, w_ref, x_ref, c_ref, mv_ref, g_ref, beta_ref,
                         o_ref):
    i = pl.program_id(1)
    ta = (((0,), (0,)), ((), ()))

    def project(a_ref, b_ref, h_ref):
        y = (lax.dot_general(a_ref[0], w_ref[:512, :], ta, preferred_element_type=F32)
             + lax.dot_general(b_ref[0], w_ref[512:, :], ta, preferred_element_type=F32))
        gate = mv_ref[0][2:3]
        o_ref[0] = _layer_norm(ALPHA * h_ref[0] + gate * y, g_ref[...], beta_ref[...])

    @pl.when(i < NT_LAT)
    def _latent():
        project(al_ref, bl_ref, x_ref)

    @pl.when(i == NT_LAT)
    def _context():
        project(ac_ref, bc_ref, c_ref)


def _outproj_even(a_lat, a_ctx, b_lat, b_ctx, w, x, ctx, modv, g, beta):
    lat_mix = pl.BlockSpec((1, 512, TM), lambda b, i: (b, 0, jnp.minimum(i, NT_LAT - 1)))
    ctx_mix = pl.BlockSpec((1, 512, L), lambda b, i: (b, 0, 0))
    return pl.pallas_call(
        _outproj_even_kernel,
        out_shape=jax.ShapeDtypeStruct((B, N, D), F32),
        grid=(B, NT_ALL),
        in_specs=[
            lat_mix, ctx_mix, lat_mix, ctx_mix,
            pl.BlockSpec((D, D), lambda b, i: (0, 0)),
            pl.BlockSpec((1, TM, D), _lat_tile),
            pl.BlockSpec((1, L, D), lambda b, i: (b, 0, 0)),
            pl.BlockSpec((1, N_MOD, D), lambda b, i: (_mod_row(b, i), 0, 0)),
            pl.BlockSpec((1, D), lambda b, i: (0, 0)),
            pl.BlockSpec((1, D), lambda b, i: (0, 0)),
        ],
        out_specs=pl.BlockSpec((1, TM, D), lambda b, i: (b, i, 0)),
        compiler_params=_cparams(2),
        name="outproj_even_ln",
    )(a_lat, a_ctx, b_lat, b_ctx, w, x, ctx, modv, g.reshape(1, D), beta.reshape(1, D))


def _outproj_odd(mix_a, mix_b, w, h, modv, g, beta):
    return pl.pallas_call(
        _outproj_odd_kernel,
        out_shape=jax.ShapeDtypeStruct((B, S, D), F32),
        grid=(B, S // TM_LAT),
        in_specs=[
            pl.BlockSpec((1, 512, TM_LAT), lambda b, i: (b, 0, i)),
            pl.BlockSpec((1, TM_LAT, 512), lambda b, i: (b, i, 0)),
            pl.BlockSpec((D, D), lambda b, i: (0, 0)),
            pl.BlockSpec((1, TM_LAT, D), lambda b, i: (b, i, 0)),
            pl.BlockSpec((1, N_MOD, D), lambda b, i: (b, 0, 0)),
            pl.BlockSpec((1, D), lambda b, i: (0, 0)),
            pl.BlockSpec((1, D), lambda b, i: (0, 0)),
        ],
        out_specs=pl.BlockSpec((1, TM_LAT, D), lambda b, i: (b, i, 0)),
        compiler_params=_cparams(2),
        name="outproj_odd_ln",
    )(mix_a, mix_b, w, h, modv, g.reshape(1, D), beta.reshape(1, D))


def _ffn_kernel(*refs, n_sub, has_ctx):
    if has_ctx:
        h_ref, mvb_ref, mvc_ref, win_ref, wout_ref, g_ref, beta_ref, o_ref = refs
    else:
        h_ref, mvb_ref, win_ref, wout_ref, g_ref, beta_ref, o_ref = refs
        mvc_ref = None
    for j in range(n_sub):
        rows = slice(j * TM, (j + 1) * TM)
        mv = _tile_mod(mvb_ref, mvc_ref, j, n_sub)
        h = h_ref[0, rows, :]
        u = (h * (1.0 + mv[4:5]) + mv[3:4]).astype(BF16)
        ga = jnp.dot(u, win_ref[...], preferred_element_type=F32)
        gt, a = ga[:, :FFN_HIDDEN], ga[:, FFN_HIDDEN:]
        act = (gt * (1.0 / (1.0 + jnp.exp(-gt))) * a).astype(BF16)
        y = jnp.dot(act, wout_ref[...], preferred_element_type=F32)
        o_ref[0, rows, :] = _layer_norm(ALPHA * h + mv[5:6] * y, g_ref[...], beta_ref[...])


def _ffn(h, modv, w_in, w_out, g, beta, n_rows, tm):
    has_ctx = n_rows == N
    mod_specs = [pl.BlockSpec((1, N_MOD, D), lambda b, i: (b, 0, 0))]
    if has_ctx:
        mod_specs.append(pl.BlockSpec((1, N_MOD, D), lambda b, i: (2, 0, 0)))
    weights_once = dict(pipeline_mode=pl.Buffered(1))
    return pl.pallas_call(
        functools.partial(_ffn_kernel, n_sub=tm // TM, has_ctx=has_ctx),
        out_shape=jax.ShapeDtypeStruct((B, n_rows, D), F32),
        grid=(B, n_rows // tm),
        in_specs=[pl.BlockSpec((1, tm, D), lambda b, i: (b, i, 0))] + mod_specs + [
            pl.BlockSpec((D, 2 * FFN_HIDDEN), lambda b, i: (0, 0), **weights_once),
            pl.BlockSpec((FFN_HIDDEN, D), lambda b, i: (0, 0), **weights_once),
            pl.BlockSpec((1, D), lambda b, i: (0, 0)),
            pl.BlockSpec((1, D), lambda b, i: (0, 0)),
        ],
        out_specs=pl.BlockSpec((1, tm, D), lambda b, i: (b, i, 0)),
        compiler_params=_cparams(2),
        name="ffn_ln",
    )(*([h] + [modv] * len(mod_specs) + [w_in, w_out, g.reshape(1, D), beta.reshape(1, D)]))


def _gelu(x):
    return 0.5 * x * (1.0 + lax.erf(x * (1.0 / math.sqrt(2.0))))


def _rms(x, g):
    ms = jnp.mean(x * x, axis=-1, keepdims=True)
    return x * lax.rsqrt(ms + RMS_EPS) * g


def _inproj_odd_kernel(h_ref, mvb_ref, mvc_ref, win_ref, gq_ref, gkv_ref, wuq_ref, wkn_ref, wkr_ref, wv_ref,
                       cosf_ref, sinf_ref, cost_ref, sina_ref, sinb_ref,
                       lng_ref, lnb_ref, ws_ref, gb_ref,
                       q_ref, k_ref, vt_ref, gm_ref):
    nt = (((1,), (1,)), ((), ()))
    n_sub = TM_ALL // TM
    for j in range(n_sub):
        tile = slice(j * TM, (j + 1) * TM)
        mv = _tile_mod(mvb_ref, mvc_ref, j, n_sub)
        u = (h_ref[0, tile, :] * (1.0 + mv[1:2]) + mv[0:1]).astype(BF16)
        p = jnp.dot(u, win_ref[...], preferred_element_type=F32)
        cq = _rms(p[:, :MLA_Q_RANK], gq_ref[...]).astype(BF16)
        ckv = _rms(p[:, MLA_Q_RANK:MLA_Q_RANK + MLA_KV_RANK], gkv_ref[...]).astype(BF16)
        kr = p[:, 640:768]
        qt = lax.dot_general(wuq_ref[...], cq, nt, preferred_element_type=F32)
        qt = qt * ((MLA_NOPE + MLA_ROPE) ** -0.5 * LOG2E)
        cf, sf = cosf_ref[:, tile], sinf_ref[:, tile]
        for hd in range(MLA_HEADS):
            base = hd * 128
            parts = [qt[base:base + 64]]
            for half in range(2):
                x1 = qt[base + 64 + half * 16:base + 72 + half * 16]
                x2 = qt[base + 72 + half * 16:base + 80 + half * 16]
                c = cf[half * 8:(half + 1) * 8]
                sn = sf[half * 8:(half + 1) * 8]
                parts += [x1 * c - x2 * sn, x2 * c + x1 * sn]
            parts.append(qt[base + 96:base + 128])
            q_ref[0, base:base + 128, tile] = jnp.concatenate(parts, axis=0).astype(BF16)
        kr = (kr * cost_ref[tile, :] + pltpu.roll(kr, LANE - 8, 1) * sina_ref[tile, :]
              + pltpu.roll(kr, 8, 1) * sinb_ref[tile, :])
        k = (jnp.dot(ckv, wkn_ref[...], preferred_element_type=F32)
             + jnp.dot(kr.astype(BF16), wkr_ref[...], preferred_element_type=F32))
        k_ref[0, tile, :] = k.astype(BF16)
        vt_ref[0, :, tile] = lax.dot_general(wv_ref[...], ckv, nt, preferred_element_type=F32).astype(BF16)
        gu = _gelu(p[:, 768:1280])
        gv = _layer_norm(_gelu(p[:, 1280:1792]), lng_ref[...], lnb_ref[...]).astype(BF16)
        for ch in range(TM // GMLP_CHUNK):
            rows = slice(ch * GMLP_CHUNK, (ch + 1) * GMLP_CHUNK)
            out_rows = slice(j * TM + ch * GMLP_CHUNK, j * TM + (ch + 1) * GMLP_CHUNK)
            for grp in range(GMLP_GROUPS):
                cols = slice(grp * GMLP_CH, (grp + 1) * GMLP_CH)
                mixed = jnp.dot(ws_ref[grp], gv[rows, cols], preferred_element_type=F32) + gb_ref[grp]
                gm_ref[0, out_rows, cols] = (gu[rows, cols] * mixed).astype(BF16)


def _inproj_odd(h, modv, w, tabs):
    cosf, sinf, cost, sina, sinb = tabs
    const2 = lambda b, i: (0, 0)
    const3 = lambda b, i: (0, 0, 0)
    tm = TM_ALL
    return pl.pallas_call(
        _inproj_odd_kernel,
        out_shape=(jax.ShapeDtypeStruct((B, 1024, N), BF16),
                   jax.ShapeDtypeStruct((B, N, 1024), BF16),
                   jax.ShapeDtypeStruct((B, 512, N), BF16),
                   jax.ShapeDtypeStruct((B, N, 512), BF16)),
        grid=(B, N // tm),
        in_specs=[
            pl.BlockSpec((1, tm, D), lambda b, i: (b, i, 0)),
            pl.BlockSpec((1, N_MOD, D), lambda b, i: (b, 0, 0)),
            pl.BlockSpec((1, N_MOD, D), lambda b, i: (2, 0, 0)),
            pl.BlockSpec((D, 1792), const2),
            pl.BlockSpec((1, MLA_Q_RANK), const2),
            pl.BlockSpec((1, MLA_KV_RANK), const2),
            pl.BlockSpec((1024, MLA_Q_RANK), const2),
            pl.BlockSpec((MLA_KV_RANK, 1024), const2),
            pl.BlockSpec((LANE, 1024), const2),
            pl.BlockSpec((512, MLA_KV_RANK), const2),
            pl.BlockSpec((16, tm), lambda b, i: (0, i)),
            pl.BlockSpec((16, tm), lambda b, i: (0, i)),
            pl.BlockSpec((tm, LANE), lambda b, i: (i, 0)),
            pl.BlockSpec((tm, LANE), lambda b, i: (i, 0)),
            pl.BlockSpec((tm, LANE), lambda b, i: (i, 0)),
            pl.BlockSpec((1, GMLP_WIDTH), const2),
            pl.BlockSpec((1, GMLP_WIDTH), const2),
            pl.BlockSpec((GMLP_GROUPS, GMLP_CHUNK, GMLP_CHUNK), const3),
            pl.BlockSpec((GMLP_GROUPS, GMLP_CHUNK, 1), const3),
        ],
        out_specs=(pl.BlockSpec((1, 1024, tm), lambda b, i: (b, 0, i)),
                   pl.BlockSpec((1, tm, 1024), lambda b, i: (b, i, 0)),
                   pl.BlockSpec((1, 512, tm), lambda b, i: (b, 0, i)),
                   pl.BlockSpec((1, tm, 512), lambda b, i: (b, i, 0))),
        compiler_params=_cparams(2),
        name="inproj_odd",
    )(h, modv, modv, w["win"], w["gq"], w["gkv"], w["wuq"], w["wkn"], w["wkr"], w["wv"],
      cosf, sinf, cost, sina, sinb, w["lng"], w["lnb"], w["ws"], w["gb"])


def _mla_attn_kernel(q_ref, k_ref, vt_ref, o_ref):
    chunks = _latent_chunks(TK_MLA)
    rhs = [q_ref[0, :, t * TQ_MLA:(t + 1) * TQ_MLA] for t in range(TILES_MLA)]
    results, overflow = _flash_fixed_ref(rhs, k_ref, vt_ref, chunks)
    for t, (acc, l) in enumerate(results):
        o_ref[0, :, t * TQ_MLA:(t + 1) * TQ_MLA] = (acc / l).astype(BF16)

    @pl.when(overflow)
    def _recompute():
        def tile(t, carry):
            cols = pl.ds(pl.multiple_of(t * TQ_MLA, TQ_MLA), TQ_MLA)
            acc, l = _flash(q_ref[0, :, cols], k_ref, vt_ref, chunks)
            o_ref[0, :, cols] = (acc / l).astype(BF16)
            return carry
        lax.fori_loop(0, TILES_MLA, tile, 0)


def _mla_attention(q_t, k, v_t):
    bq = TQ_MLA * TILES_MLA
    return pl.pallas_call(
        _mla_attn_kernel,
        out_shape=jax.ShapeDtypeStruct((B, 512, S), BF16),
        grid=(B, MLA_HEADS, S // bq),
        in_specs=[
            pl.BlockSpec((1, 128, bq), lambda b, h, i: (b, h, i)),
            pl.BlockSpec((1, N, 128), lambda b, h, i: (b, 0, h)),
            pl.BlockSpec((1, MLA_V, N), lambda b, h, i: (b, h, 0)),
        ],
        out_specs=pl.BlockSpec((1, MLA_V, bq), lambda b, h, i: (b, h, i)),
        compiler_params=_cparams(3),
        name="mla_attention",
    )(q_t, k, v_t)


def _rope_tables(n_freq):
    t = jnp.arange(S)
    inv = jnp.power(ROPE_BASE, -jnp.arange(0, 2 * n_freq, 2, dtype=F32) / (2 * n_freq))
    ang_r = (t // GRID_W).astype(F32)[:, None] * inv[None, :]
    ang_c = (t % GRID_W).astype(F32)[:, None] * inv[None, :]
    cos = jnp.concatenate([jnp.cos(ang_r), jnp.cos(ang_c)], axis=1)
    sin = jnp.concatenate([jnp.sin(ang_r), jnp.sin(ang_c)], axis=1)
    cos = jnp.concatenate([cos, jnp.ones((L, 2 * n_freq), F32)], axis=0)
    sin = jnp.concatenate([sin, jnp.zeros((L, 2 * n_freq), F32)], axis=0)
    lane = np.arange(LANE)
    period = 4 * n_freq
    src = ((lane % period) // (2 * n_freq)) * n_freq + lane % n_freq
    is_x1 = (lane % (2 * n_freq)) < n_freq
    cos_t = cos[:, src]
    sin_t = sin[:, src]
    sin_a = jnp.where(jnp.asarray(is_x1)[None, :], -sin_t, 0.0)
    sin_b = jnp.where(jnp.asarray(is_x1)[None, :], 0.0, sin_t)
    return cos.T, sin.T, cos_t, sin_a, sin_b


def _even_weights(ev_w_in):
    aq, ak, av, bq, bk, bv = jnp.split(ev_w_in, [512, 1024, 1536, 2048, 2560], axis=1)
    scale = HEAD_DIM ** -0.5
    wtok = jnp.concatenate([ak, bk], axis=1).astype(BF16)
    wfeat = jnp.concatenate([aq * scale, av, bq * scale, bv], axis=1).T.astype(BF16)
    return wtok, wfeat


def _odd_weights(od_w_in, gq, w_uq, gkv, w_ukv, ln_g, ln_b, ws, gb):
    cq, ckv, kr, gu, gv = jnp.split(od_w_in, [384, 640, 672, 1184], axis=1)
    win = jnp.concatenate([cq, ckv, kr, jnp.zeros((D, LANE - MLA_ROPE), F32), gu, gv], axis=1)
    dq = MLA_NOPE + MLA_ROPE
    wuq = w_uq.reshape(MLA_Q_RANK, MLA_HEADS, dq)
    wuq = jnp.pad(wuq, ((0, 0), (0, 0), (0, LANE - dq))).reshape(MLA_Q_RANK, MLA_HEADS * LANE)
    wukv = w_ukv.reshape(MLA_KV_RANK, MLA_HEADS, MLA_NOPE + MLA_V)
    wkn = jnp.pad(wukv[:, :, :MLA_NOPE], ((0, 0), (0, 0), (0, LANE - MLA_NOPE)))
    wkn = wkn.reshape(MLA_KV_RANK, MLA_HEADS * LANE)
    wv = wukv[:, :, MLA_NOPE:].reshape(MLA_KV_RANK, MLA_HEADS * MLA_V)
    place = np.zeros((LANE, MLA_HEADS * LANE), np.float32)
    for hd in range(MLA_HEADS):
        place[np.arange(MLA_ROPE), hd * LANE + MLA_NOPE + np.arange(MLA_ROPE)] = 1.0
    return {
        "win": win.astype(BF16), "gq": gq.reshape(1, -1), "gkv": gkv.reshape(1, -1),
        "wuq": wuq.T.astype(BF16), "wkn": wkn.astype(BF16), "wkr": jnp.asarray(place, BF16),
        "wv": wv.T.astype(BF16), "lng": ln_g.reshape(1, -1), "lnb": ln_b.reshape(1, -1),
        "ws": ws.astype(BF16), "gb": gb.reshape(GMLP_GROUPS, GMLP_CHUNK, 1),
    }


def kernel(x, c, ctx, c_ctx, mod_w, mod_b, ln_mix_g, ln_mix_b, ln_ffn_g, ln_ffn_b, ffn_w_in, ffn_w_out,
           ev_w_in, ev_w_out, diff_lambda, diff_subln_g, na_rpb, od_w_in, od_w_out, mla_q_norm_g,
           mla_w_uq, mla_kv_norm_g, mla_w_ukv, gmlp_ln_g, gmlp_ln_b, gmlp_ws, gmlp_b):
    cond = jnp.concatenate([c, c_ctx[None], jnp.zeros((8 - B - 1, D), F32)], axis=0)
    mod = _modulation(cond, mod_w, mod_b).reshape(DEPTH, 8, N_MOD, D)

    wtok, wfeat = _even_weights(ev_w_in[0])
    cosf, sinf, cost, sina, sinb = _rope_tables(16)
    tok, feat = _inproj_even(x, ctx, mod[0], wtok, wfeat,
                             (cosf * LOG2E, sinf * LOG2E, cost, sina, sinb))
    h = _outproj_even(_diff_attention(tok, feat, diff_lambda[0], diff_subln_g[0], False),
                      _diff_attention(tok, feat, diff_lambda[0], diff_subln_g[0], True),
                      _na_attention(tok, feat, _na_bias_table(na_rpb[0])),
                      _na_attention_ctx(tok, feat),
                      ev_w_out[0].astype(BF16), x, ctx, mod[0], ln_mix_g[0], ln_mix_b[0])
    h = _ffn(h, mod[0], ffn_w_in[0].astype(BF16), ffn_w_out[0].astype(BF16),
             ln_ffn_g[0], ln_ffn_b[0], N, TM_ALL)

    w1 = _odd_weights(od_w_in[0], mla_q_norm_g[0], mla_w_uq[0], mla_kv_norm_g[0], mla_w_ukv[0],
                      gmlp_ln_g[0], gmlp_ln_b[0], gmlp_ws[0], gmlp_b[0])
    q_t, k, v_t, gm = _inproj_odd(h, mod[1], w1, _rope_tables(8))
    mix_a = _mla_attention(q_t, k, v_t)
    h = _outproj_odd(mix_a, gm, od_w_out[0].astype(BF16), h, mod[1], ln_mix_g[1], ln_mix_b[1])
    return _ffn(h, mod[1], ffn_w_in[1].astype(BF16), ffn_w_out[1].astype(BF16),
                ln_ffn_g[1], ln_ffn_b[1], S, TM_LAT)
```

```python
import functools
import math

import jax
import jax.numpy as jnp
import numpy as np
from jax import lax
from jax.experimental import pallas as pl
from jax.experimental.pallas import tpu as pltpu

F32 = jnp.float32
BF16 = jnp.bfloat16

D = 1024
B = 2
S = 8192
L = 256
N = S + L
DEPTH = 2
GRID_W = 64
ROWS = S // GRID_W
ALPHA = (2 * DEPTH) ** 0.25
ROPE_BASE = 10000.0
LN_EPS = 1e-5
RMS_EPS = 1e-6
N_MOD = 6
HEAD_DIM = 64
DIFF_HEADS = 4
DIFF_V = 128
NA_HEADS = 8
NA_ROWS = 8
NA_COLS = 16
MLA_HEADS = 8
MLA_Q_RANK = 384
MLA_KV_RANK = 256
MLA_NOPE = 64
MLA_ROPE = 32
MLA_V = 64
GMLP_GROUPS = 4
GMLP_CH = 128
GMLP_CHUNK = 128
GMLP_WIDTH = 512
FFN_HIDDEN = 2816
LAMBDA_INIT_0 = 0.8 - 0.6 * math.exp(-0.3 * 0)
LOG2E = math.log2(math.e)

LANE = 128
TM = 256
TM_ALL = 3 * TM
TM_LAT = 2 * TM
NT_ALL = N // TM
NT_LAT = S // TM
TQ_DIFF, TK_DIFF = 256, 256
TQ_MLA, TK_MLA = 512, 256
TILES_DIFF, TILES_MLA = 4, 4
TQ_NA = 1024
NA_WIN_ROWS = 10
NA_WIN = NA_WIN_ROWS * GRID_W
NA_PATTERNS = 5
NEG = -1e30
VMEM_LIMIT = 56 * 1024 * 1024


def _cparams(n_axes):
    return pltpu.CompilerParams(dimension_semantics=("parallel",) * n_axes,
                                vmem_limit_bytes=VMEM_LIMIT)


def _mod_row(b, i):
    return jnp.where(i < NT_LAT, b, 2)


def _mod_kernel(cond_ref, w_ref, b_ref, o_ref):
    cnd = cond_ref[...]
    act = cnd * (1.0 / (1.0 + jnp.exp(-cnd)))
    o_ref[0] = jnp.dot(act, w_ref[0], preferred_element_type=F32) + b_ref[0]


def _modulation(cond, mod_w, mod_b):
    tn = 1536
    return pl.pallas_call(
        _mod_kernel,
        out_shape=jax.ShapeDtypeStruct((DEPTH, 8, N_MOD * D), F32),
        grid=(DEPTH, N_MOD * D // tn),
        in_specs=[
            pl.BlockSpec((8, D), lambda l, j: (0, 0)),
            pl.BlockSpec((1, D, tn), lambda l, j: (l, 0, j)),
            pl.BlockSpec((1, 1, tn), lambda l, j: (l, 0, j)),
        ],
        out_specs=pl.BlockSpec((1, 8, tn), lambda l, j: (l, 0, j)),
        compiler_params=_cparams(2),
        name="modulation",
    )(cond, mod_w, mod_b.reshape(DEPTH, 1, N_MOD * D))


def _modulated(h_ref, mv_ref, k):
    mv = mv_ref[0]
    return (h_ref[0] * (1.0 + mv[k + 1:k + 2]) + mv[k:k + 1]).astype(BF16)


def _tile_mod(mvb_ref, mvc_ref, j, n_sub):
    if mvc_ref is None or j != n_sub - 1:
        return mvb_ref[0]
    is_ctx = pl.program_id(1) == pl.num_programs(1) - 1
    return jnp.where(is_ctx, mvc_ref[0], mvb_ref[0])


def _layer_norm(x, g, b):
    mu = jnp.mean(x, axis=-1, keepdims=True)
    xc = x - mu
    var = jnp.mean(xc * xc, axis=-1, keepdims=True)
    return xc * lax.rsqrt(var + LN_EPS) * g + b


def _inproj_even_kernel(x_ref, c_ref, mv_ref, wtok_ref, wfeat_ref, cosf_ref, sinf_ref,
                        cost_ref, sina_ref, sinb_ref, tok_ref, feat_ref, u_ref):
    i = pl.program_id(1)

    @pl.when(i < NT_LAT)
    def _latent():
        u_ref[...] = _modulated(x_ref, mv_ref, 0)

    @pl.when(i == NT_LAT)
    def _context():
        u_ref[...] = _modulated(c_ref, mv_ref, 0)

    u = u_ref[...]
    tok = jnp.dot(u, wtok_ref[...], preferred_element_type=F32)
    ct, sa, sb = cost_ref[...], sina_ref[...], sinb_ref[...]
    for j in range(4):
        xs = tok[:, j * LANE:(j + 1) * LANE]
        ro = xs * ct + pltpu.roll(xs, LANE - 16, 1) * sa + pltpu.roll(xs, 16, 1) * sb
        tok_ref[0, :, j * LANE:(j + 1) * LANE] = ro.astype(BF16)
    tok_ref[0, :, 512:] = tok[:, 512:].astype(BF16)
    feat = lax.dot_general(wfeat_ref[...], u, (((1,), (1,)), ((), ())),
                           preferred_element_type=F32)
    cf, sf = cosf_ref[...], sinf_ref[...]
    for blk in range(8):
        for half in range(2):
            base = blk * 64 + half * 32
            x1 = feat[base:base + 16]
            x2 = feat[base + 16:base + 32]
            c = cf[half * 16:(half + 1) * 16]
            s = sf[half * 16:(half + 1) * 16]
            feat_ref[0, base:base + 16, :] = (x1 * c - x2 * s).astype(BF16)
            feat_ref[0, base + 16:base + 32, :] = (x2 * c + x1 * s).astype(BF16)
    feat_ref[0, 512:1024, :] = feat[512:1024].astype(BF16)
    feat_ref[0, 1024:1536, :] = (feat[1024:1536] * LOG2E).astype(BF16)
    feat_ref[0, 1536:, :] = feat[1536:].astype(BF16)


def _lat_tile(b, i):
    return (b, jnp.minimum(i, NT_LAT - 1), 0)


def _inproj_even(x, ctx, modv, wtok, wfeat, tabs):
    cosf, sinf, cost, sina, sinb = tabs
    const = lambda b, i: (0, 0)
    return pl.pallas_call(
        _inproj_even_kernel,
        out_shape=(jax.ShapeDtypeStruct((B, N, 1024), BF16),
                   jax.ShapeDtypeStruct((B, 2048, N), BF16)),
        grid=(B, NT_ALL),
        in_specs=[
            pl.BlockSpec((1, TM, D), _lat_tile),
            pl.BlockSpec((1, L, D), lambda b, i: (b, 0, 0)),
            pl.BlockSpec((1, N_MOD, D), lambda b, i: (_mod_row(b, i), 0, 0)),
            pl.BlockSpec((D, 1024), const),
            pl.BlockSpec((2048, D), const),
            pl.BlockSpec((32, TM), lambda b, i: (0, i)),
            pl.BlockSpec((32, TM), lambda b, i: (0, i)),
            pl.BlockSpec((TM, LANE), lambda b, i: (i, 0)),
            pl.BlockSpec((TM, LANE), lambda b, i: (i, 0)),
            pl.BlockSpec((TM, LANE), lambda b, i: (i, 0)),
        ],
        out_specs=(pl.BlockSpec((1, TM, 1024), lambda b, i: (b, i, 0)),
                   pl.BlockSpec((1, 2048, TM), lambda b, i: (b, 0, i))),
        scratch_shapes=[pltpu.VMEM((TM, D), BF16)],
        compiler_params=_cparams(2),
        name="inproj_even",
    )(x, ctx, modv, wtok, wfeat, cosf, sinf, cost, sina, sinb)


def _block_diag2(q_t):
    n = q_t.shape[1]
    z = jnp.zeros((64, n), q_t.dtype)
    left = jnp.concatenate([q_t[:64], z], axis=0)
    right = jnp.concatenate([z, q_t[64:]], axis=0)
    return jnp.concatenate([left, right], axis=1)


ONES_ROWS = 16


def _flash(rhs, k_ref, vt_ref, chunks):
    width = rhs.shape[1]
    dv = vt_ref.shape[1]
    m = jnp.full((1, width), -jnp.inf, F32)
    acc = jnp.zeros((dv + ONES_ROWS, width), F32)

    def scores(idx):
        st, sz = chunks[idx]
        return jnp.dot(k_ref[0, st:st + sz, :], rhs, preferred_element_type=F32)

    s_next = scores(0)
    for idx, (st, sz) in enumerate(chunks):
        s = s_next
        if idx + 1 < len(chunks):
            s_next = scores(idx + 1)
        m_new = jnp.maximum(m, jnp.max(s, axis=0, keepdims=True))
        alpha = jnp.exp2(m - m_new)
        p = jnp.exp2(s - m_new).astype(BF16)
        vt1 = jnp.concatenate([vt_ref[0, :, st:st + sz], jnp.ones((ONES_ROWS, sz), BF16)], axis=0)
        acc = acc * alpha + jnp.dot(vt1, p, preferred_element_type=F32)
        m = m_new
    return acc[:dv], acc[dv:dv + 1]


SCORES_AHEAD = 2
P_LIMIT = 2.0 ** 64


def _flash_fixed_ref(rhs_tiles, k_ref, vt_ref, chunks):
    n = len(chunks)
    items = [(t, idx) for t in range(len(rhs_tiles)) for idx in range(n)]

    def scores(item):
        t, idx = item
        st, sz = chunks[idx]
        return jnp.dot(k_ref[0, st:st + sz, :], rhs_tiles[t], preferred_element_type=F32)

    pending = [scores(it) for it in items[:SCORES_AHEAD]]
    results = []
    worst = None
    for j, (t, idx) in enumerate(items):
        s = pending.pop(0)
        if j + SCORES_AHEAD < len(items):
            pending.append(scores(items[j + SCORES_AHEAD]))
        width = s.shape[1]
        if idx == 0:
            m_ref = jnp.max(s, axis=0, keepdims=True)
        p = jnp.exp2(s - m_ref)
        l8_new = jnp.sum(p.reshape(-1, 8, width), axis=0)
        st, sz = chunks[idx]
        pv = jnp.dot(vt_ref[0, :, st:st + sz], p.astype(BF16), preferred_element_type=F32)
        l8, acc = (l8_new, pv) if idx == 0 else (l8 + l8_new, acc + pv)
        if idx == n - 1:
            l = jnp.sum(l8, axis=0, keepdims=True)
            results.append((acc, l))
            l_max = jnp.max(l)
            worst = l_max if worst is None else jnp.maximum(worst, l_max)
    return results, worst > P_LIMIT


def _latent_chunks(tk):
    return [(j * tk, tk) for j in range(S // tk)] + [(S, L)]


def _diff_attn_kernel(lam_ref, g_ref, q_ref, k_ref, vt_ref, o_ref, *, chunks, tq):
    n_tiles = q_ref.shape[2] // tq

    def finish(acc, l):
        lf = lam_ref[...]
        lam = (jnp.exp(jnp.sum(lf[0:1] * lf[1:2], axis=1, keepdims=True))
               - jnp.exp(jnp.sum(lf[2:3] * lf[3:4], axis=1, keepdims=True)) + LAMBDA_INIT_0)
        o = acc[:, :tq] / l[:, :tq] - lam * (acc[:, tq:] / l[:, tq:])
        ms = jnp.mean(o * o, axis=0, keepdims=True)
        o = o * lax.rsqrt(ms + RMS_EPS) * g_ref[...] * (1.0 - LAMBDA_INIT_0)
        return o.astype(BF16)

    rhs = [_block_diag2(q_ref[0, :, t * tq:(t + 1) * tq]) for t in range(n_tiles)]
    results, overflow = _flash_fixed_ref(rhs, k_ref, vt_ref, chunks)
    for t, (acc, l) in enumerate(results):
        o_ref[0, :, t * tq:(t + 1) * tq] = finish(acc, l)

    @pl.when(overflow)
    def _recompute():
        def tile(t, carry):
            cols = pl.ds(pl.multiple_of(t * tq, tq), tq)
            o_ref[0, :, cols] = finish(*_flash(_block_diag2(q_ref[0, :, cols]), k_ref, vt_ref, chunks))
            return carry
        lax.fori_loop(0, n_tiles, tile, 0)


def _diff_attention(tok, feat, diff_lambda, subln_g, context):
    if context:
        tq, bq, nq, chunks = L, L, 1, [(0, L)]
        q_spec = pl.BlockSpec((1, 128, L), lambda b, h, i: (b, h, S // L))
        k_spec = pl.BlockSpec((1, L, 128), lambda b, h, i: (b, S // L, h))
        vt_spec = pl.BlockSpec((1, 128, L), lambda b, h, i: (b, 4 + h, S // L))
    else:
        tq, bq, chunks = TQ_DIFF, TQ_DIFF * TILES_DIFF, _latent_chunks(TK_DIFF)
        nq = S // bq
        q_spec = pl.BlockSpec((1, 128, bq), lambda b, h, i: (b, h, i))
        k_spec = pl.BlockSpec((1, N, 128), lambda b, h, i: (b, 0, h))
        vt_spec = pl.BlockSpec((1, 128, N), lambda b, h, i: (b, 4 + h, 0))
    return pl.pallas_call(
        functools.partial(_diff_attn_kernel, chunks=chunks, tq=tq),
        out_shape=jax.ShapeDtypeStruct((B, 512, nq * bq), BF16),
        grid=(B, DIFF_HEADS, nq),
        in_specs=[
            pl.BlockSpec((4, HEAD_DIM), lambda b, h, i: (0, 0)),
            pl.BlockSpec((DIFF_V, 1), lambda b, h, i: (0, 0)),
            q_spec, k_spec, vt_spec,
        ],
        out_specs=pl.BlockSpec((1, 128, bq), lambda b, h, i: (b, h, i)),
        compiler_params=_cparams(3),
        name="diff_attention_ctx" if context else "diff_attention",
    )(diff_lambda, subln_g.reshape(DIFF_V, 1), feat, tok, feat)


def _na_softmax_pv(s_c, vt_ctx, s_w=None, vt_win=None):
    m = jnp.max(s_c, axis=0, keepdims=True)
    if s_w is not None:
        m = jnp.maximum(m, jnp.max(s_w, axis=0, keepdims=True))
    p_c = jnp.exp2(s_c - m)
    l = jnp.sum(p_c, axis=0, keepdims=True)
    o = jnp.dot(vt_ctx, p_c.astype(BF16), preferred_element_type=F32)
    if s_w is not None:
        p_w = jnp.exp2(s_w - m)
        l = l + jnp.sum(p_w, axis=0, keepdims=True)
        o = o + jnp.dot(vt_win, p_w.astype(BF16), preferred_element_type=F32)
    o = o / l
    n = s_c.shape[1] // 2
    return jnp.concatenate([o[:64, :n], o[64:, n:]], axis=0).astype(BF16)


def _na_kernel(q_ref, k_ref, vt_ref, bias_ref, o_ref):
    i = pl.program_id(2)
    k_ctx = k_ref[0, S:, :]
    vt_ctx = vt_ref[0, :, S:]
    n_pairs = TQ_NA // LANE

    def scores(jj):
        r = 2 * (i * n_pairs + jj)
        start = jnp.clip(r - NA_ROWS // 2, 0, ROWS - NA_WIN_ROWS)
        pat = jnp.where(r == 0, 0, jnp.where(r == 2, 1, jnp.where(
            r == ROWS - 4, 3, jnp.where(r == ROWS - 2, 4, 2))))
        off = pl.multiple_of(start * GRID_W, LANE)
        rhs = _block_diag2(q_ref[0, :, jj * LANE:(jj + 1) * LANE])
        s_c = jnp.dot(k_ctx, rhs, preferred_element_type=F32)
        s_w = jnp.dot(k_ref[0, pl.ds(off, NA_WIN), :], rhs, preferred_element_type=F32) + bias_ref[0, pat]
        return s_c, s_w, off

    pending = [scores(jj) for jj in range(min(SCORES_AHEAD, n_pairs))]
    for jj in range(n_pairs):
        s_c, s_w, off = pending.pop(0)
        if jj + SCORES_AHEAD < n_pairs:
            pending.append(scores(jj + SCORES_AHEAD))
        o_ref[0, :, jj * LANE:(jj + 1) * LANE] = _na_softmax_pv(
            s_c, vt_ctx, s_w, vt_ref[0, :, pl.ds(off, NA_WIN)])


def _na_ctx_kernel(q_ref, k_ref, vt_ref, o_ref):
    s_c = jnp.dot(k_ref[0], _block_diag2(q_ref[0]), preferred_element_type=F32)
    o_ref[0] = _na_softmax_pv(s_c, vt_ref[0])


def _na_attention(tok, feat, bias):
    return pl.pallas_call(
        _na_kernel,
        out_shape=jax.ShapeDtypeStruct((B, 512, S), BF16),
        grid=(B, NA_HEADS // 2, S // TQ_NA),
        in_specs=[
            pl.BlockSpec((1, 128, TQ_NA), lambda b, h, i: (b, 8 + h, i)),
            pl.BlockSpec((1, N, 128), lambda b, h, i: (b, 0, 4 + h)),
            pl.BlockSpec((1, 128, N), lambda b, h, i: (b, 12 + h, 0)),
            pl.BlockSpec((1, NA_PATTERNS, NA_WIN, 2 * LANE), lambda b, h, i: (h, 0, 0, 0)),
        ],
        out_specs=pl.BlockSpec((1, 128, TQ_NA), lambda b, h, i: (b, h, i)),
        compiler_params=_cparams(3),
        name="na_attention",
    )(feat, tok, feat, bias)


def _na_attention_ctx(tok, feat):
    return pl.pallas_call(
        _na_ctx_kernel,
        out_shape=jax.ShapeDtypeStruct((B, 512, L), BF16),
        grid=(B, NA_HEADS // 2),
        in_specs=[
            pl.BlockSpec((1, 128, L), lambda b, h: (b, 8 + h, S // L)),
            pl.BlockSpec((1, L, 128), lambda b, h: (b, S // L, 4 + h)),
            pl.BlockSpec((1, 128, L), lambda b, h: (b, 12 + h, S // L)),
        ],
        out_specs=pl.BlockSpec((1, 128, L), lambda b, h: (b, h, 0)),
        compiler_params=_cparams(2),
        name="na_attention_ctx",
    )(feat, tok, feat)


def _na_bias_table(rpb):
    pats = [(0, 0), (2, 0), (8, 4), (ROWS - 4, ROWS - NA_WIN_ROWS), (ROWS - 2, ROWS - NA_WIN_ROWS)]
    n_off = 2 * NA_COLS - 1
    kr_rel = np.arange(NA_WIN_ROWS)
    j = np.arange(2)
    drow = np.zeros((NA_PATTERNS, NA_WIN_ROWS, 2), np.int32)
    row_ok = np.zeros((NA_PATTERNS, NA_WIN_ROWS, 2), bool)
    for p, (r, start) in enumerate(pats):
        rq = r + j
        rs = np.clip(rq - NA_ROWS // 2, 0, ROWS - NA_ROWS)
        kr = start + kr_rel
        row_ok[p] = (kr[:, None] >= rs[None, :]) & (kr[:, None] < rs[None, :] + NA_ROWS)
        drow[p] = np.clip(kr[:, None] - rq[None, :] + NA_ROWS - 1, 0, 2 * NA_ROWS - 2)
    kc = np.arange(GRID_W)
    c = np.arange(GRID_W)
    cs = np.clip(c - NA_COLS // 2, 0, GRID_W - NA_COLS)
    col_ok = (kc[:, None] >= cs[None, :]) & (kc[:, None] < cs[None, :] + NA_COLS)
    dcol = kc[:, None] - c[None, :] + NA_COLS - 1
    hit = (dcol[None] == np.arange(n_off)[:, None, None]) & col_ok[None]
    expand = np.zeros((4 * n_off + 3, GRID_W, 2, 2, GRID_W), np.float32)
    for hh in range(2):
        for jj in range(2):
            expand[(hh * 2 + jj) * n_off:(hh * 2 + jj + 1) * n_off, :, hh, jj, :] = hit
    for jj in range(2):
        expand[4 * n_off + jj, :, :, jj, :] = NEG
    expand[4 * n_off + 2] = np.where(col_ok, 0.0, NEG)[:, None, None, :]
    expand = expand.reshape(4 * n_off + 3, GRID_W, 2 * LANE)
    rows = rpb[:, drow.reshape(-1), :] * LOG2E
    rows = rows.reshape(NA_HEADS // 2, 2, NA_PATTERNS * NA_WIN_ROWS, 2, n_off)
    rows = rows.transpose(0, 2, 1, 3, 4).reshape(NA_HEADS // 2, -1, 4 * n_off)
    flags = np.concatenate([(~row_ok).reshape(-1, 2), np.ones((NA_PATTERNS * NA_WIN_ROWS, 1), bool)], 1)
    flags = jnp.broadcast_to(jnp.asarray(flags, F32)[None], (NA_HEADS // 2,) + flags.shape)
    rows = jnp.concatenate([rows, flags], axis=2).reshape(-1, 4 * n_off + 3)
    t = jnp.einsum("gd,dkl->gkl", rows, jnp.asarray(expand), precision=lax.Precision.HIGHEST)
    return t.reshape(NA_HEADS // 2, NA_PATTERNS, NA_WIN, 2 * LANE)


def _outproj_odd_kernel(a_ref, b_ref, w_ref, h_ref, mv_ref, g_ref, beta_ref, o_ref):
    gate = mv_ref[0][2:3]
    for j in range(TM_LAT // TM):
        rows = slice(j * TM, (j + 1) * TM)
        y = (lax.dot_general(a_ref[0, :, rows], w_ref[:512, :], (((0,), (0,)), ((), ())),
                             preferred_element_type=F32)
             + jnp.dot(b_ref[0, rows, :], w_ref[512:, :], preferred_element_type=F32))
        o_ref[0, rows, :] = _layer_norm(ALPHA * h_ref[0, rows, :] + gate * y, g_ref[...], beta_ref[...])


def _outproj_even_kernel(al_ref, ac_ref, bl_ref, bc_ref, w_ref, x_ref, c_ref, mv_ref, g_ref, beta_ref,
                         o_ref):
    i = pl.program_id(1)
    ta = (((0,), (0,)), ((), ()))

    def project(a_ref, b_ref, h_ref):
        y = (lax.dot_general(a_ref[0], w_ref[:512, :], ta, preferred_element_type=F32)
             + lax.dot_general(b_ref[0], w_ref[512:, :], ta, preferred_element_type=F32))
        gate = mv_ref[0][2:3]
        o_ref[0] = _layer_norm(ALPHA * h_ref[0] + gate * y, g_ref[...], beta_ref[...])

    @pl.when(i < NT_LAT)
    def _latent():
        project(al_ref, bl_ref, x_ref)

    @pl.when(i == NT_LAT)
    def _context():
        project(ac_ref, bc_ref, c_ref)


def _outproj_even(a_lat, a_ctx, b_lat, b_ctx, w, x, ctx, modv, g, beta):
    lat_mix = pl.BlockSpec((1, 512, TM), lambda b, i: (b, 0, jnp.minimum(i, NT_LAT - 1)))
    ctx_mix = pl.BlockSpec((1, 512, L), lambda b, i: (b, 0, 0))
    return pl.pallas_call(
        _outproj_even_kernel,
        out_shape=jax.ShapeDtypeStruct((B, N, D), F32),
        grid=(B, NT_ALL),
        in_specs=[
            lat_mix, ctx_mix, lat_mix, ctx_mix,
            pl.BlockSpec((D, D), lambda b, i: (0, 0)),
            pl.BlockSpec((1, TM, D), _lat_tile),
            pl.BlockSpec((1, L, D), lambda b, i: (b, 0, 0)),
            pl.BlockSpec((1, N_MOD, D), lambda b, i: (_mod_row(b, i), 0, 0)),
            pl.BlockSpec((1, D), lambda b, i: (0, 0)),
            pl.BlockSpec((1, D), lambda b, i: (0, 0)),
        ],
        out_specs=pl.BlockSpec((1, TM, D), lambda b, i: (b, i, 0)),
        compiler_params=_cparams(2),
        name="outproj_even_ln",
    )(a_lat, a_ctx, b_lat, b_ctx, w, x, ctx, modv, g.reshape(1, D), beta.reshape(1, D))


def _outproj_odd(mix_a, mix_b, w, h, modv, g, beta):
    return pl.pallas_call(
        _outproj_odd_kernel,
        out_shape=jax.ShapeDtypeStruct((B, S, D), F32),
        grid=(B, S // TM_LAT),
        in_specs=[
            pl.BlockSpec((1, 512, TM_LAT), lambda b, i: (b, 0, i)),
            pl.BlockSpec((1, TM_LAT, 512), lambda b, i: (b, i, 0)),
            pl.BlockSpec((D, D), lambda b, i: (0, 0)),
            pl.BlockSpec((1, TM_LAT, D), lambda b, i: (b, i, 0)),
            pl.BlockSpec((1, N_MOD, D), lambda b, i: (b, 0, 0)),
            pl.BlockSpec((1, D), lambda b, i: (0, 0)),
            pl.BlockSpec((1, D), lambda b, i: (0, 0)),
        ],
        out_specs=pl.BlockSpec((1, TM_LAT, D), lambda b, i: (b, i, 0)),
        compiler_params=_cparams(2),
        name="outproj_odd_ln",
    )(mix_a, mix_b, w, h, modv, g.reshape(1, D), beta.reshape(1, D))


def _ffn_kernel(*refs, n_sub, has_ctx):
    if has_ctx:
        h_ref, mvb_ref, mvc_ref, win_ref, wout_ref, g_ref, beta_ref, o_ref = refs
    else:
        h_ref, mvb_ref, win_ref, wout_ref, g_ref, beta_ref, o_ref = refs
        mvc_ref = None
    for j in range(n_sub):
        rows = slice(j * TM, (j + 1) * TM)
        mv = _tile_mod(mvb_ref, mvc_ref, j, n_sub)
        h = h_ref[0, rows, :]
        u = (h * (1.0 + mv[4:5]) + mv[3:4]).astype(BF16)
        ga = jnp.dot(u, win_ref[...], preferred_element_type=F32)
        gt, a = ga[:, :FFN_HIDDEN], ga[:, FFN_HIDDEN:]
        act = (gt * (1.0 / (1.0 + jnp.exp(-gt))) * a).astype(BF16)
        y = jnp.dot(act, wout_ref[...], preferred_element_type=F32)
        o_ref[0, rows, :] = _layer_norm(ALPHA * h + mv[5:6] * y, g_ref[...], beta_ref[...])


def _ffn(h, modv, w_in, w_out, layer, g, beta, n_rows, tm):
    has_ctx = n_rows == N
    mod_specs = [pl.BlockSpec((1, N_MOD, D), lambda b, i: (b, 0, 0))]
    if has_ctx:
        mod_specs.append(pl.BlockSpec((1, N_MOD, D), lambda b, i: (2, 0, 0)))
    weights_once = dict(pipeline_mode=pl.Buffered(1))
    return pl.pallas_call(
        functools.partial(_ffn_kernel, n_sub=tm // TM, has_ctx=has_ctx),
        out_shape=jax.ShapeDtypeStruct((B, n_rows, D), F32),
        grid=(B, n_rows // tm),
        in_specs=[pl.BlockSpec((1, tm, D), lambda b, i: (b, i, 0))] + mod_specs + [
            pl.BlockSpec((None, D, 2 * FFN_HIDDEN), lambda b, i: (layer, 0, 0), **weights_once),
            pl.BlockSpec((None, FFN_HIDDEN, D), lambda b, i: (layer, 0, 0), **weights_once),
            pl.BlockSpec((1, D), lambda b, i: (0, 0)),
            pl.BlockSpec((1, D), lambda b, i: (0, 0)),
        ],
        out_specs=pl.BlockSpec((1, tm, D), lambda b, i: (b, i, 0)),
        compiler_params=_cparams(2),
        name="ffn_ln",
    )(*([h] + [modv] * len(mod_specs) + [w_in, w_out, g.reshape(1, D), beta.reshape(1, D)]))


def _gelu(x):
    return 0.5 * x * (1.0 + lax.erf(x * (1.0 / math.sqrt(2.0))))


def _rms(x, g):
    ms = jnp.mean(x * x, axis=-1, keepdims=True)
    return x * lax.rsqrt(ms + RMS_EPS) * g


def _inproj_odd_kernel(h_ref, mvb_ref, mvc_ref, win_ref, gq_ref, gkv_ref, wuq_ref, wkn_ref, wkr_ref, wv_ref,
                       cosf_ref, sinf_ref, cost_ref, sina_ref, sinb_ref,
                       lng_ref, lnb_ref, ws_ref, gb_ref,
                       q_ref, k_ref, vt_ref, gm_ref):
    nt = (((1,), (1,)), ((), ()))
    n_sub = TM_ALL // TM
    for j in range(n_sub):
        tile = slice(j * TM, (j + 1) * TM)
        mv = _tile_mod(mvb_ref, mvc_ref, j, n_sub)
        u = (h_ref[0, tile, :] * (1.0 + mv[1:2]) + mv[0:1]).astype(BF16)
        p = jnp.dot(u, win_ref[...], preferred_element_type=F32)
        cq = _rms(p[:, :MLA_Q_RANK], gq_ref[...]).astype(BF16)
        ckv = _rms(p[:, MLA_Q_RANK:MLA_Q_RANK + MLA_KV_RANK], gkv_ref[...]).astype(BF16)
        kr = p[:, 640:768]
        qt = lax.dot_general(wuq_ref[...], cq, nt, preferred_element_type=F32)
        qt = qt * ((MLA_NOPE + MLA_ROPE) ** -0.5 * LOG2E)
        cf, sf = cosf_ref[:, tile], sinf_ref[:, tile]
        for hd in range(MLA_HEADS):
            base = hd * 128
            parts = [qt[base:base + 64]]
            for half in range(2):
                x1 = qt[base + 64 + half * 16:base + 72 + half * 16]
                x2 = qt[base + 72 + half * 16:base + 80 + half * 16]
                c = cf[half * 8:(half + 1) * 8]
                sn = sf[half * 8:(half + 1) * 8]
                parts += [x1 * c - x2 * sn, x2 * c + x1 * sn]
            parts.append(qt[base + 96:base + 128])
            q_ref[0, base:base + 128, tile] = jnp.concatenate(parts, axis=0).astype(BF16)
        kr = (kr * cost_ref[tile, :] + pltpu.roll(kr, LANE - 8, 1) * sina_ref[tile, :]
              + pltpu.roll(kr, 8, 1) * sinb_ref[tile, :])
        k = (jnp.dot(ckv, wkn_ref[...], preferred_element_type=F32)
             + jnp.dot(kr.astype(BF16), wkr_ref[...], preferred_element_type=F32))
        k_ref[0, tile, :] = k.astype(BF16)
        vt_ref[0, :, tile] = lax.dot_general(wv_ref[...], ckv, nt, preferred_element_type=F32).astype(BF16)
        gu = _gelu(p[:, 768:1280])
        gv = _layer_norm(_gelu(p[:, 1280:1792]), lng_ref[...], lnb_ref[...]).astype(BF16)
        for ch in range(TM // GMLP_CHUNK):
            rows = slice(ch * GMLP_CHUNK, (ch + 1) * GMLP_CHUNK)
            out_rows = slice(j * TM + ch * GMLP_CHUNK, j * TM + (ch + 1) * GMLP_CHUNK)
            for grp in range(GMLP_GROUPS):
                cols = slice(grp * GMLP_CH, (grp + 1) * GMLP_CH)
                mixed = jnp.dot(ws_ref[grp], gv[rows, cols], preferred_element_type=F32) + gb_ref[grp]
                gm_ref[0, out_rows, cols] = (gu[rows, cols] * mixed).astype(BF16)


def _inproj_odd(h, modv, w, tabs):
    cosf, sinf, cost, sina, sinb = tabs
    const2 = lambda b, i: (0, 0)
    const3 = lambda b, i: (0, 0, 0)
    tm = TM_ALL
    return pl.pallas_call(
        _inproj_odd_kernel,
        out_shape=(jax.ShapeDtypeStruct((B, 1024, N), BF16),
                   jax.ShapeDtypeStruct((B, N, 1024), BF16),
                   jax.ShapeDtypeStruct((B, 512, N), BF16),
                   jax.ShapeDtypeStruct((B, N, 512), BF16)),
        grid=(B, N // tm),
        in_specs=[
            pl.BlockSpec((1, tm, D), lambda b, i: (b, i, 0)),
            pl.BlockSpec((1, N_MOD, D), lambda b, i: (b, 0, 0)),
            pl.BlockSpec((1, N_MOD, D), lambda b, i: (2, 0, 0)),
            pl.BlockSpec((D, 1792), const2),
            pl.BlockSpec((1, MLA_Q_RANK), const2),
            pl.BlockSpec((1, MLA_KV_RANK), const2),
            pl.BlockSpec((1024, MLA_Q_RANK), const2),
            pl.BlockSpec((MLA_KV_RANK, 1024), const2),
            pl.BlockSpec((LANE, 1024), const2),
            pl.BlockSpec((512, MLA_KV_RANK), const2),
            pl.BlockSpec((16, tm), lambda b, i: (0, i)),
            pl.BlockSpec((16, tm), lambda b, i: (0, i)),
            pl.BlockSpec((tm, LANE), lambda b, i: (i, 0)),
            pl.BlockSpec((tm, LANE), lambda b, i: (i, 0)),
            pl.BlockSpec((tm, LANE), lambda b, i: (i, 0)),
            pl.BlockSpec((1, GMLP_WIDTH), const2),
            pl.BlockSpec((1, GMLP_WIDTH), const2),
            pl.BlockSpec((GMLP_GROUPS, GMLP_CHUNK, GMLP_CHUNK), const3),
            pl.BlockSpec((GMLP_GROUPS, GMLP_CHUNK, 1), const3),
        ],
        out_specs=(pl.BlockSpec((1, 1024, tm), lambda b, i: (b, 0, i)),
                   pl.BlockSpec((1, tm, 1024), lambda b, i: (b, i, 0)),
                   pl.BlockSpec((1, 512, tm), lambda b, i: (b, 0, i)),
                   pl.BlockSpec((1, tm, 512), lambda b, i: (b, i, 0))),
        compiler_params=_cparams(2),
        name="inproj_odd",
    )(h, modv, modv, w["win"], w["gq"], w["gkv"], w["wuq"], w["wkn"], w["wkr"], w["wv"],
      cosf, sinf, cost, sina, sinb, w["lng"], w["lnb"], w["ws"], w["gb"])


def _mla_attn_kernel(q_ref, k_ref, vt_ref, o_ref):
    chunks = _latent_chunks(TK_MLA)
    rhs = [q_ref[0, :, t * TQ_MLA:(t + 1) * TQ_MLA] for t in range(TILES_MLA)]
    results, overflow = _flash_fixed_ref(rhs, k_ref, vt_ref, chunks)
    for t, (acc, l) in enumerate(results):
        o_ref[0, :, t * TQ_MLA:(t + 1) * TQ_MLA] = (acc / l).astype(BF16)

    @pl.when(overflow)
    def _recompute():
        def tile(t, carry):
            cols = pl.ds(pl.multiple_of(t * TQ_MLA, TQ_MLA), TQ_MLA)
            acc, l = _flash(q_ref[0, :, cols], k_ref, vt_ref, chunks)
            o_ref[0, :, cols] = (acc / l).astype(BF16)
            return carry
        lax.fori_loop(0, TILES_MLA, tile, 0)


def _mla_attention(q_t, k, v_t):
    bq = TQ_MLA * TILES_MLA
    return pl.pallas_call(
        _mla_attn_kernel,
        out_shape=jax.ShapeDtypeStruct((B, 512, S), BF16),
        grid=(B, MLA_HEADS, S // bq),
        in_specs=[
            pl.BlockSpec((1, 128, bq), lambda b, h, i: (b, h, i)),
            pl.BlockSpec((1, N, 128), lambda b, h, i: (b, 0, h)),
            pl.BlockSpec((1, MLA_V, N), lambda b, h, i: (b, h, 0)),
        ],
        out_specs=pl.BlockSpec((1, MLA_V, bq), lambda b, h, i: (b, h, i)),
        compiler_params=_cparams(3),
        name="mla_attention",
    )(q_t, k, v_t)


def _rope_tables(n_freq):
    t = np.arange(S)
    inv = np.power(np.float32(ROPE_BASE), -np.arange(0, 2 * n_freq, 2, dtype=np.float32) / np.float32(2 * n_freq))
    ang_r = (t // GRID_W).astype(np.float32)[:, None] * inv[None, :]
    ang_c = (t % GRID_W).astype(np.float32)[:, None] * inv[None, :]
    cos = np.concatenate([np.cos(ang_r), np.cos(ang_c)], axis=1)
    sin = np.concatenate([np.sin(ang_r), np.sin(ang_c)], axis=1)
    cos = np.concatenate([cos, np.ones((L, 2 * n_freq), np.float32)], axis=0).astype(np.float32)
    sin = np.concatenate([sin, np.zeros((L, 2 * n_freq), np.float32)], axis=0).astype(np.float32)
    lane = np.arange(LANE)
    period = 4 * n_freq
    src = ((lane % period) // (2 * n_freq)) * n_freq + lane % n_freq
    is_x1 = (lane % (2 * n_freq)) < n_freq
    cos_t = cos[:, src]
    sin_t = sin[:, src]
    sin_a = np.where(is_x1[None, :], -sin_t, 0.0).astype(np.float32)
    sin_b = np.where(is_x1[None, :], 0.0, sin_t).astype(np.float32)
    return np.ascontiguousarray(cos.T), np.ascontiguousarray(sin.T), cos_t, sin_a, sin_b


def _even_weights(ev_w_in):
    aq, ak, av, bq, bk, bv = jnp.split(ev_w_in, [512, 1024, 1536, 2048, 2560], axis=1)
    scale = HEAD_DIM ** -0.5
    wtok = jnp.concatenate([ak, bk], axis=1).astype(BF16)
    wfeat = jnp.concatenate([aq * scale, av, bq * scale, bv], axis=1).T.astype(BF16)
    return wtok, wfeat


def _odd_weights(od_w_in, gq, w_uq, gkv, w_ukv, ln_g, ln_b, ws, gb):
    cq, ckv, kr, gu, gv = jnp.split(od_w_in, [384, 640, 672, 1184], axis=1)
    win = jnp.concatenate([cq, ckv, kr, jnp.zeros((D, LANE - MLA_ROPE), F32), gu, gv], axis=1)
    dq = MLA_NOPE + MLA_ROPE
    wuq = w_uq.reshape(MLA_Q_RANK, MLA_HEADS, dq)
    wuq = jnp.pad(wuq, ((0, 0), (0, 0), (0, LANE - dq))).reshape(MLA_Q_RANK, MLA_HEADS * LANE)
    wukv = w_ukv.reshape(MLA_KV_RANK, MLA_HEADS, MLA_NOPE + MLA_V)
    wkn = jnp.pad(wukv[:, :, :MLA_NOPE], ((0, 0), (0, 0), (0, LANE - MLA_NOPE)))
    wkn = wkn.reshape(MLA_KV_RANK, MLA_HEADS * LANE)
    wv = wukv[:, :, MLA_NOPE:].reshape(MLA_KV_RANK, MLA_HEADS * MLA_V)
    place = np.zeros((LANE, MLA_HEADS * LANE), np.float32)
    for hd in range(MLA_HEADS):
        place[np.arange(MLA_ROPE), hd * LANE + MLA_NOPE + np.arange(MLA_ROPE)] = 1.0
    return {
        "win": win.astype(BF16), "gq": gq.reshape(1, -1), "gkv": gkv.reshape(1, -1),
        "wuq": wuq.T.astype(BF16), "wkn": wkn.astype(BF16), "wkr": jnp.asarray(place, BF16),
        "wv": wv.T.astype(BF16), "lng": ln_g.reshape(1, -1), "lnb": ln_b.reshape(1, -1),
        "ws": ws.astype(BF16), "gb": gb.reshape(GMLP_GROUPS, GMLP_CHUNK, 1),
    }


def kernel(x, c, ctx, c_ctx, mod_w, mod_b, ln_mix_g, ln_mix_b, ln_ffn_g, ln_ffn_b, ffn_w_in, ffn_w_out,
           ev_w_in, ev_w_out, diff_lambda, diff_subln_g, na_rpb, od_w_in, od_w_out, mla_q_norm_g,
           mla_w_uq, mla_kv_norm_g, mla_w_ukv, gmlp_ln_g, gmlp_ln_b, gmlp_ws, gmlp_b):
    cond = jnp.concatenate([c, c_ctx[None], jnp.zeros((8 - B - 1, D), F32)], axis=0)
    mod = _modulation(cond, mod_w, mod_b).reshape(DEPTH, 8, N_MOD, D)

    wtok, wfeat = _even_weights(ev_w_in[0])
    cosf, sinf, cost, sina, sinb = _rope_tables(16)
    tok, feat = _inproj_even(x, ctx, mod[0], wtok, wfeat,
                             (cosf * np.float32(LOG2E), sinf * np.float32(LOG2E), cost, sina, sinb))
    h = _outproj_even(_diff_attention(tok, feat, diff_lambda[0], diff_subln_g[0], False),
                      _diff_attention(tok, feat, diff_lambda[0], diff_subln_g[0], True),
                      _na_attention(tok, feat, _na_bias_table(na_rpb[0])),
                      _na_attention_ctx(tok, feat),
                      ev_w_out[0].astype(BF16), x, ctx, mod[0], ln_mix_g[0], ln_mix_b[0])
    ffn_w_in, ffn_w_out = ffn_w_in.astype(BF16), ffn_w_out.astype(BF16)
    h = _ffn(h, mod[0], ffn_w_in, ffn_w_out, 0, ln_ffn_g[0], ln_ffn_b[0], N, TM_ALL)

    w1 = _odd_weights(od_w_in[0], mla_q_norm_g[0], mla_w_uq[0], mla_kv_norm_g[0], mla_w_ukv[0],
                      gmlp_ln_g[0], gmlp_ln_b[0], gmlp_ws[0], gmlp_b[0])
    q_t, k, v_t, gm = _inproj_odd(h, mod[1], w1, _rope_tables(8))
    mix_a = _mla_attention(q_t, k, v_t)
    h = _outproj_odd(mix_a, gm, od_w_out[0].astype(BF16), h, mod[1], ln_mix_g[1], ln_mix_b[1])
    return _ffn(h, mod[1], ffn_w_in, ffn_w_out, 1, ln_ffn_g[1], ln_ffn_b[1], S, TM_LAT)
```

```python
import functools
import math

import jax
import jax.numpy as jnp
import numpy as np
from jax import lax
from jax.experimental import pallas as pl
from jax.experimental.pallas import tpu as pltpu

F32 = jnp.float32
BF16 = jnp.bfloat16

D = 1024
B = 2
S = 8192
L = 256
N = S + L
DEPTH = 2
GRID_W = 64
ROWS = S // GRID_W
ALPHA = (2 * DEPTH) ** 0.25
ROPE_BASE = 10000.0
LN_EPS = 1e-5
RMS_EPS = 1e-6
N_MOD = 6
HEAD_DIM = 64
DIFF_HEADS = 4
DIFF_V = 128
NA_HEADS = 8
NA_ROWS = 8
NA_COLS = 16
MLA_HEADS = 8
MLA_Q_RANK = 384
MLA_KV_RANK = 256
MLA_NOPE = 64
MLA_ROPE = 32
MLA_V = 64
GMLP_GROUPS = 4
GMLP_CH = 128
GMLP_CHUNK = 128
GMLP_WIDTH = 512
FFN_HIDDEN = 2816
LAMBDA_INIT_0 = 0.8 - 0.6 * math.exp(-0.3 * 0)
LOG2E = math.log2(math.e)

LANE = 128
TM = 256
TM_ALL = 3 * TM
TM_LAT = 4 * TM
NT_ALL = N // TM
NT_LAT = S // TM
TQ_DIFF, TK_DIFF = 256, 256
TQ_MLA, TK_MLA = 512, 256
TILES_DIFF, TILES_MLA = 4, 4
TQ_NA = 2048
NA_WIN_ROWS = 10
NA_WIN = NA_WIN_ROWS * GRID_W
NA_PATTERNS = 5
NEG = -1e30
VMEM_LIMIT = 56 * 1024 * 1024


def _cparams(n_axes):
    return pltpu.CompilerParams(dimension_semantics=("parallel",) * n_axes,
                                vmem_limit_bytes=VMEM_LIMIT)


def _mod_row(b, i):
    return jnp.where(i < NT_LAT, b, 2)


def _mod_kernel(cond_ref, w_ref, b_ref, o_ref):
    cnd = cond_ref[...]
    act = cnd * (1.0 / (1.0 + jnp.exp(-cnd)))
    o_ref[0] = jnp.dot(act, w_ref[0], preferred_element_type=F32) + b_ref[0]


def _modulation(cond, mod_w, mod_b):
    tn = 1536
    return pl.pallas_call(
        _mod_kernel,
        out_shape=jax.ShapeDtypeStruct((DEPTH, 8, N_MOD * D), F32),
        grid=(DEPTH, N_MOD * D // tn),
        in_specs=[
            pl.BlockSpec((8, D), lambda l, j: (0, 0)),
            pl.BlockSpec((1, D, tn), lambda l, j: (l, 0, j)),
            pl.BlockSpec((1, 1, tn), lambda l, j: (l, 0, j)),
        ],
        out_specs=pl.BlockSpec((1, 8, tn), lambda l, j: (l, 0, j)),
        compiler_params=_cparams(2),
        name="modulation",
    )(cond, mod_w, mod_b.reshape(DEPTH, 1, N_MOD * D))


def _modulated(h_ref, mv_ref, k):
    mv = mv_ref[0]
    return (h_ref[0] * (1.0 + mv[k + 1:k + 2]) + mv[k:k + 1]).astype(BF16)


def _tile_mod(mvb_ref, mvc_ref, j, n_sub):
    if mvc_ref is None or j != n_sub - 1:
        return mvb_ref[0]
    is_ctx = pl.program_id(1) == pl.num_programs(1) - 1
    return jnp.where(is_ctx, mvc_ref[0], mvb_ref[0])


def _layer_norm(x, g, b):
    mu = jnp.mean(x, axis=-1, keepdims=True)
    xc = x - mu
    var = jnp.mean(xc * xc, axis=-1, keepdims=True)
    return xc * lax.rsqrt(var + LN_EPS) * g + b


def _inproj_even_kernel(x_ref, c_ref, mv_ref, wtok_ref, wfeat_ref, cosf_ref, sinf_ref,
                        cost_ref, sina_ref, sinb_ref, tok_ref, feat_ref, u_ref):
    i = pl.program_id(1)

    @pl.when(i < NT_LAT)
    def _latent():
        u_ref[...] = _modulated(x_ref, mv_ref, 0)

    @pl.when(i == NT_LAT)
    def _context():
        u_ref[...] = _modulated(c_ref, mv_ref, 0)

    u = u_ref[...]
    tok = jnp.dot(u, wtok_ref[...], preferred_element_type=F32)
    ct, sa, sb = cost_ref[...], sina_ref[...], sinb_ref[...]
    for j in range(4):
        xs = tok[:, j * LANE:(j + 1) * LANE]
        ro = xs * ct + pltpu.roll(xs, LANE - 16, 1) * sa + pltpu.roll(xs, 16, 1) * sb
        tok_ref[0, :, j * LANE:(j + 1) * LANE] = ro.astype(BF16)
    tok_ref[0, :, 512:] = tok[:, 512:].astype(BF16)
    feat = lax.dot_general(wfeat_ref[...], u, (((1,), (1,)), ((), ())),
                           preferred_element_type=F32)
    cf, sf = cosf_ref[...], sinf_ref[...]
    for blk in range(8):
        for half in range(2):
            base = blk * 64 + half * 32
            x1 = feat[base:base + 16]
            x2 = feat[base + 16:base + 32]
            c = cf[half * 16:(half + 1) * 16]
            s = sf[half * 16:(half + 1) * 16]
            feat_ref[0, base:base + 16, :] = (x1 * c - x2 * s).astype(BF16)
            feat_ref[0, base + 16:base + 32, :] = (x2 * c + x1 * s).astype(BF16)
    feat_ref[0, 512:1024, :] = feat[512:1024].astype(BF16)
    feat_ref[0, 1024:1536, :] = (feat[1024:1536] * LOG2E).astype(BF16)
    feat_ref[0, 1536:, :] = feat[1536:].astype(BF16)


def _lat_tile(b, i):
    return (b, jnp.minimum(i, NT_LAT - 1), 0)


def _inproj_even(x, ctx, modv, wtok, wfeat, tabs):
    cosf, sinf, cost, sina, sinb = tabs
    const = lambda b, i: (0, 0)
    return pl.pallas_call(
        _inproj_even_kernel,
        out_shape=(jax.ShapeDtypeStruct((B, N, 1024), BF16),
                   jax.ShapeDtypeStruct((B, 2048, N), BF16)),
        grid=(B, NT_ALL),
        in_specs=[
            pl.BlockSpec((1, TM, D), _lat_tile),
            pl.BlockSpec((1, L, D), lambda b, i: (b, 0, 0)),
            pl.BlockSpec((1, N_MOD, D), lambda b, i: (_mod_row(b, i), 0, 0)),
            pl.BlockSpec((D, 1024), const),
            pl.BlockSpec((2048, D), const),
            pl.BlockSpec((32, TM), lambda b, i: (0, i)),
            pl.BlockSpec((32, TM), lambda b, i: (0, i)),
            pl.BlockSpec((TM, LANE), lambda b, i: (i, 0)),
            pl.BlockSpec((TM, LANE), lambda b, i: (i, 0)),
            pl.BlockSpec((TM, LANE), lambda b, i: (i, 0)),
        ],
        out_specs=(pl.BlockSpec((1, TM, 1024), lambda b, i: (b, i, 0)),
                   pl.BlockSpec((1, 2048, TM), lambda b, i: (b, 0, i))),
        scratch_shapes=[pltpu.VMEM((TM, D), BF16)],
        compiler_params=_cparams(2),
        name="inproj_even",
    )(x, ctx, modv, wtok, wfeat, cosf, sinf, cost, sina, sinb)


def _block_diag2(q_t):
    n = q_t.shape[1]
    z = jnp.zeros((64, n), q_t.dtype)
    left = jnp.concatenate([q_t[:64], z], axis=0)
    right = jnp.concatenate([z, q_t[64:]], axis=0)
    return jnp.concatenate([left, right], axis=1)


ONES_ROWS = 16


def _flash(rhs, k_ref, vt_ref, chunks):
    width = rhs.shape[1]
    dv = vt_ref.shape[1]
    m = jnp.full((1, width), -jnp.inf, F32)
    acc = jnp.zeros((dv + ONES_ROWS, width), F32)

    def scores(idx):
        st, sz = chunks[idx]
        return jnp.dot(k_ref[0, st:st + sz, :], rhs, preferred_element_type=F32)

    s_next = scores(0)
    for idx, (st, sz) in enumerate(chunks):
        s = s_next
        if idx + 1 < len(chunks):
            s_next = scores(idx + 1)
        m_new = jnp.maximum(m, jnp.max(s, axis=0, keepdims=True))
        alpha = jnp.exp2(m - m_new)
        p = jnp.exp2(s - m_new).astype(BF16)
        vt1 = jnp.concatenate([vt_ref[0, :, st:st + sz], jnp.ones((ONES_ROWS, sz), BF16)], axis=0)
        acc = acc * alpha + jnp.dot(vt1, p, preferred_element_type=F32)
        m = m_new
    return acc[:dv], acc[dv:dv + 1]


SCORES_AHEAD = 2
P_LIMIT = 2.0 ** 64


def _flash_fixed_ref(rhs_tiles, k_ref, vt_ref, chunks):
    n = len(chunks)
    items = [(t, idx) for t in range(len(rhs_tiles)) for idx in range(n)]

    def scores(item):
        t, idx = item
        st, sz = chunks[idx]
        return jnp.dot(k_ref[0, st:st + sz, :], rhs_tiles[t], preferred_element_type=F32)

    pending = [scores(it) for it in items[:SCORES_AHEAD]]
    results = []
    worst = None
    for j, (t, idx) in enumerate(items):
        s = pending.pop(0)
        if j + SCORES_AHEAD < len(items):
            pending.append(scores(items[j + SCORES_AHEAD]))
        width = s.shape[1]
        if idx == 0:
            m_ref = jnp.max(s, axis=0, keepdims=True)
        p = jnp.exp2(s - m_ref)
        l8_new = jnp.sum(p.reshape(-1, 8, width), axis=0)
        st, sz = chunks[idx]
        pv = jnp.dot(vt_ref[0, :, st:st + sz], p.astype(BF16), preferred_element_type=F32)
        l8, acc = (l8_new, pv) if idx == 0 else (l8 + l8_new, acc + pv)
        if idx == n - 1:
            l = jnp.sum(l8, axis=0, keepdims=True)
            results.append((acc, l))
            l_max = jnp.max(l)
            worst = l_max if worst is None else jnp.maximum(worst, l_max)
    return results, worst > P_LIMIT


def _latent_chunks(tk):
    return [(j * tk, tk) for j in range(S // tk)] + [(S, L)]


def _diff_attn_kernel(lam_ref, g_ref, q_ref, k_ref, vt_ref, o_ref, *, chunks, tq):
    n_tiles = q_ref.shape[2] // tq

    def finish(acc, l):
        lf = lam_ref[...]
        lam = (jnp.exp(jnp.sum(lf[0:1] * lf[1:2], axis=1, keepdims=True))
               - jnp.exp(jnp.sum(lf[2:3] * lf[3:4], axis=1, keepdims=True)) + LAMBDA_INIT_0)
        o = acc[:, :tq] / l[:, :tq] - lam * (acc[:, tq:] / l[:, tq:])
        ms = jnp.mean(o * o, axis=0, keepdims=True)
        o = o * lax.rsqrt(ms + RMS_EPS) * g_ref[...] * (1.0 - LAMBDA_INIT_0)
        return o.astype(BF16)

    rhs = [_block_diag2(q_ref[0, :, t * tq:(t + 1) * tq]) for t in range(n_tiles)]
    results, overflow = _flash_fixed_ref(rhs, k_ref, vt_ref, chunks)
    for t, (acc, l) in enumerate(results):
        o_ref[0, :, t * tq:(t + 1) * tq] = finish(acc, l)

    @pl.when(overflow)
    def _recompute():
        def tile(t, carry):
            cols = pl.ds(pl.multiple_of(t * tq, tq), tq)
            o_ref[0, :, cols] = finish(*_flash(_block_diag2(q_ref[0, :, cols]), k_ref, vt_ref, chunks))
            return carry
        lax.fori_loop(0, n_tiles, tile, 0)


def _diff_attention(tok, feat, diff_lambda, subln_g, context):
    if context:
        tq, bq, nq, chunks = L, L, 1, [(0, L)]
        q_spec = pl.BlockSpec((1, 128, L), lambda b, h, i: (b, h, S // L))
        k_spec = pl.BlockSpec((1, L, 128), lambda b, h, i: (b, S // L, h))
        vt_spec = pl.BlockSpec((1, 128, L), lambda b, h, i: (b, 4 + h, S // L))
    else:
        tq, bq, chunks = TQ_DIFF, TQ_DIFF * TILES_DIFF, _latent_chunks(TK_DIFF)
        nq = S // bq
        q_spec = pl.BlockSpec((1, 128, bq), lambda b, h, i: (b, h, i))
        k_spec = pl.BlockSpec((1, N, 128), lambda b, h, i: (b, 0, h))
        vt_spec = pl.BlockSpec((1, 128, N), lambda b, h, i: (b, 4 + h, 0))
    return pl.pallas_call(
        functools.partial(_diff_attn_kernel, chunks=chunks, tq=tq),
        out_shape=jax.ShapeDtypeStruct((B, 512, nq * bq), BF16),
        grid=(B, DIFF_HEADS, nq),
        in_specs=[
            pl.BlockSpec((4, HEAD_DIM), lambda b, h, i: (0, 0)),
            pl.BlockSpec((DIFF_V, 1), lambda b, h, i: (0, 0)),
            q_spec, k_spec, vt_spec,
        ],
        out_specs=pl.BlockSpec((1, 128, bq), lambda b, h, i: (b, h, i)),
        compiler_params=_cparams(3),
        name="diff_attention_ctx" if context else "diff_attention",
    )(diff_lambda, subln_g.reshape(DIFF_V, 1), feat, tok, feat)


def _na_softmax_pv(s_c, vt_ctx, s_w=None, vt_win=None):
    m = jnp.max(s_c, axis=0, keepdims=True)
    if s_w is not None:
        m = jnp.maximum(m, jnp.max(s_w, axis=0, keepdims=True))
    p_c = jnp.exp2(s_c - m)
    l = jnp.sum(p_c, axis=0, keepdims=True)
    o = jnp.dot(vt_ctx, p_c.astype(BF16), preferred_element_type=F32)
    if s_w is not None:
        p_w = jnp.exp2(s_w - m)
        l = l + jnp.sum(p_w, axis=0, keepdims=True)
        o = o + jnp.dot(vt_win, p_w.astype(BF16), preferred_element_type=F32)
    o = o / l
    n = s_c.shape[1] // 2
    return jnp.concatenate([o[:64, :n], o[64:, n:]], axis=0).astype(BF16)


def _na_kernel(q_ref, k_ref, vt_ref, bias_ref, o_ref):
    i = pl.program_id(2)
    k_ctx = k_ref[0, S:, :]
    vt_ctx = vt_ref[0, :, S:]
    n_pairs = TQ_NA // LANE

    def scores(jj):
        r = 2 * (i * n_pairs + jj)
        start = jnp.clip(r - NA_ROWS // 2, 0, ROWS - NA_WIN_ROWS)
        pat = jnp.where(r == 0, 0, jnp.where(r == 2, 1, jnp.where(
            r == ROWS - 4, 3, jnp.where(r == ROWS - 2, 4, 2))))
        off = pl.multiple_of(start * GRID_W, LANE)
        rhs = _block_diag2(q_ref[0, :, jj * LANE:(jj + 1) * LANE])
        s_c = jnp.dot(k_ctx, rhs, preferred_element_type=F32)
        s_w = jnp.dot(k_ref[0, pl.ds(off, NA_WIN), :], rhs, preferred_element_type=F32) + bias_ref[0, pat]
        return s_c, s_w, off

    pending = [scores(jj) for jj in range(min(SCORES_AHEAD, n_pairs))]
    for jj in range(n_pairs):
        s_c, s_w, off = pending.pop(0)
        if jj + SCORES_AHEAD < n_pairs:
            pending.append(scores(jj + SCORES_AHEAD))
        o_ref[0, :, jj * LANE:(jj + 1) * LANE] = _na_softmax_pv(
            s_c, vt_ctx, s_w, vt_ref[0, :, pl.ds(off, NA_WIN)])


def _na_ctx_kernel(q_ref, k_ref, vt_ref, o_ref):
    s_c = jnp.dot(k_ref[0], _block_diag2(q_ref[0]), preferred_element_type=F32)
    o_ref[0] = _na_softmax_pv(s_c, vt_ref[0])


def _na_attention(tok, feat, bias):
    return pl.pallas_call(
        _na_kernel,
        out_shape=jax.ShapeDtypeStruct((B, 512, S), BF16),
        grid=(B, NA_HEADS // 2, S // TQ_NA),
        in_specs=[
            pl.BlockSpec((1, 128, TQ_NA), lambda b, h, i: (b, 8 + h, i)),
            pl.BlockSpec((1, N, 128), lambda b, h, i: (b, 0, 4 + h)),
            pl.BlockSpec((1, 128, N), lambda b, h, i: (b, 12 + h, 0)),
            pl.BlockSpec((1, NA_PATTERNS, NA_WIN, 2 * LANE), lambda b, h, i: (h, 0, 0, 0)),
        ],
        out_specs=pl.BlockSpec((1, 128, TQ_NA), lambda b, h, i: (b, h, i)),
        compiler_params=_cparams(3),
        name="na_attention",
    )(feat, tok, feat, bias)


def _na_attention_ctx(tok, feat):
    return pl.pallas_call(
        _na_ctx_kernel,
        out_shape=jax.ShapeDtypeStruct((B, 512, L), BF16),
        grid=(B, NA_HEADS // 2),
        in_specs=[
            pl.BlockSpec((1, 128, L), lambda b, h: (b, 8 + h, S // L)),
            pl.BlockSpec((1, L, 128), lambda b, h: (b, S // L, 4 + h)),
            pl.BlockSpec((1, 128, L), lambda b, h: (b, 12 + h, S // L)),
        ],
        out_specs=pl.BlockSpec((1, 128, L), lambda b, h: (b, h, 0)),
        compiler_params=_cparams(2),
        name="na_attention_ctx",
    )(feat, tok, feat)


def _na_bias_table(rpb):
    pats = [(0, 0), (2, 0), (8, 4), (ROWS - 4, ROWS - NA_WIN_ROWS), (ROWS - 2, ROWS - NA_WIN_ROWS)]
    n_off = 2 * NA_COLS - 1
    kr_rel = np.arange(NA_WIN_ROWS)
    j = np.arange(2)
    drow = np.zeros((NA_PATTERNS, NA_WIN_ROWS, 2), np.int32)
    row_ok = np.zeros((NA_PATTERNS, NA_WIN_ROWS, 2), bool)
    for p, (r, start) in enumerate(pats):
        rq = r + j
        rs = np.clip(rq - NA_ROWS // 2, 0, ROWS - NA_ROWS)
        kr = start + kr_rel
        row_ok[p] = (kr[:, None] >= rs[None, :]) & (kr[:, None] < rs[None, :] + NA_ROWS)
        drow[p] = np.clip(kr[:, None] - rq[None, :] + NA_ROWS - 1, 0, 2 * NA_ROWS - 2)
    kc = np.arange(GRID_W)
    c = np.arange(GRID_W)
    cs = np.clip(c - NA_COLS // 2, 0, GRID_W - NA_COLS)
    col_ok = (kc[:, None] >= cs[None, :]) & (kc[:, None] < cs[None, :] + NA_COLS)
    dcol = kc[:, None] - c[None, :] + NA_COLS - 1
    hit = (dcol[None] == np.arange(n_off)[:, None, None]) & col_ok[None]
    expand = np.zeros((4 * n_off + 3, GRID_W, 2, 2, GRID_W), np.float32)
    for hh in range(2):
        for jj in range(2):
            expand[(hh * 2 + jj) * n_off:(hh * 2 + jj + 1) * n_off, :, hh, jj, :] = hit
    for jj in range(2):
        expand[4 * n_off + jj, :, :, jj, :] = NEG
    expand[4 * n_off + 2] = np.where(col_ok, 0.0, NEG)[:, None, None, :]
    expand = expand.reshape(4 * n_off + 3, GRID_W, 2 * LANE)
    rows = rpb[:, drow.reshape(-1), :] * LOG2E
    rows = rows.reshape(NA_HEADS // 2, 2, NA_PATTERNS * NA_WIN_ROWS, 2, n_off)
    rows = rows.transpose(0, 2, 1, 3, 4).reshape(NA_HEADS // 2, -1, 4 * n_off)
    flags = np.concatenate([(~row_ok).reshape(-1, 2), np.ones((NA_PATTERNS * NA_WIN_ROWS, 1), bool)], 1)
    flags = jnp.broadcast_to(jnp.asarray(flags, F32)[None], (NA_HEADS // 2,) + flags.shape)
    rows = jnp.concatenate([rows, flags], axis=2).reshape(-1, 4 * n_off + 3)
    t = jnp.einsum("gd,dkl->gkl", rows, jnp.asarray(expand), precision=lax.Precision.HIGHEST)
    return t.reshape(NA_HEADS // 2, NA_PATTERNS, NA_WIN, 2 * LANE)


def _outproj_odd_kernel(a_ref, b_ref, w_ref, h_ref, mv_ref, g_ref, beta_ref, o_ref):
    gate = mv_ref[0][2:3]
    for j in range(TM_LAT // TM):
        rows = slice(j * TM, (j + 1) * TM)
        y = (lax.dot_general(a_ref[0, :, rows], w_ref[:512, :], (((0,), (0,)), ((), ())),
                             preferred_element_type=F32)
             + jnp.dot(b_ref[0, rows, :], w_ref[512:, :], preferred_element_type=F32))
        o_ref[0, rows, :] = _layer_norm(ALPHA * h_ref[0, rows, :] + gate * y, g_ref[...], beta_ref[...])


def _outproj_even_kernel(al_ref, ac_ref, bl_ref, bc_ref, w_ref, x_ref, c_ref, mv_ref, g_ref, beta_ref,
                         o_ref):
    i = pl.program_id(1)
    ta = (((0,), (0,)), ((), ()))

    def project(a_ref, b_ref, h_ref):
        y = (lax.dot_general(a_ref[0], w_ref[:512, :], ta, preferred_element_type=F32)
             + lax.dot_general(b_ref[0], w_ref[512:, :], ta, preferred_element_type=F32))
        gate = mv_ref[0][2:3]
        o_ref[0] = _layer_norm(ALPHA * h_ref[0] + gate * y, g_ref[...], beta_ref[...])

    @pl.when(i < NT_LAT)
    def _latent():
        project(al_ref, bl_ref, x_ref)

    @pl.when(i == NT_LAT)
    def _context():
        project(ac_ref, bc_ref, c_ref)


def _outproj_even(a_lat, a_ctx, b_lat, b_ctx, w, x, ctx, modv, g, beta):
    lat_mix = pl.BlockSpec((1, 512, TM), lambda b, i: (b, 0, jnp.minimum(i, NT_LAT - 1)))
    ctx_mix = pl.BlockSpec((1, 512, L), lambda b, i: (b, 0, 0))
    return pl.pallas_call(
        _outproj_even_kernel,
        out_shape=jax.ShapeDtypeStruct((B, N, D), F32),
        grid=(B, NT_ALL),
        in_specs=[
            lat_mix, ctx_mix, lat_mix, ctx_mix,
            pl.BlockSpec((D, D), lambda b, i: (0, 0)),
            pl.BlockSpec((1, TM, D), _lat_tile),
            pl.BlockSpec((1, L, D), lambda b, i: (b, 0, 0)),
            pl.BlockSpec((1, N_MOD, D), lambda b, i: (_mod_row(b, i), 0, 0)),
            pl.BlockSpec((1, D), lambda b, i: (0, 0)),
            pl.BlockSpec((1, D), lambda b, i: (0, 0)),
        ],
        out_specs=pl.BlockSpec((1, TM, D), lambda b, i: (b, i, 0)),
        compiler_params=_cparams(2),
        name="outproj_even_ln",
    )(a_lat, a_ctx, b_lat, b_ctx, w, x, ctx, modv, g.reshape(1, D), beta.reshape(1, D))


def _outproj_odd(mix_a, mix_b, w, h, modv, g, beta):
    return pl.pallas_call(
        _outproj_odd_kernel,
        out_shape=jax.ShapeDtypeStruct((B, S, D), F32),
        grid=(B, S // TM_LAT),
        in_specs=[
            pl.BlockSpec((1, 512, TM_LAT), lambda b, i: (b, 0, i)),
            pl.BlockSpec((1, TM_LAT, 512), lambda b, i: (b, i, 0)),
            pl.BlockSpec((D, D), lambda b, i: (0, 0)),
            pl.BlockSpec((1, TM_LAT, D), lambda b, i: (b, i, 0)),
            pl.BlockSpec((1, N_MOD, D), lambda b, i: (b, 0, 0)),
            pl.BlockSpec((1, D), lambda b, i: (0, 0)),
            pl.BlockSpec((1, D), lambda b, i: (0, 0)),
        ],
        out_specs=pl.BlockSpec((1, TM_LAT, D), lambda b, i: (b, i, 0)),
        compiler_params=_cparams(2),
        name="outproj_odd_ln",
    )(mix_a, mix_b, w, h, modv, g.reshape(1, D), beta.reshape(1, D))


def _ffn_kernel(*refs, n_sub, has_ctx):
    if has_ctx:
        h_ref, mvb_ref, mvc_ref, win_ref, wout_ref, g_ref, beta_ref, o_ref = refs
    else:
        h_ref, mvb_ref, win_ref, wout_ref, g_ref, beta_ref, o_ref = refs
        mvc_ref = None
    for j in range(n_sub):
        rows = slice(j * TM, (j + 1) * TM)
        mv = _tile_mod(mvb_ref, mvc_ref, j, n_sub)
        h = h_ref[0, rows, :]
        u = (h * (1.0 + mv[4:5]) + mv[3:4]).astype(BF16)
        ga = jnp.dot(u, win_ref[...], preferred_element_type=F32)
        gt, a = ga[:, :FFN_HIDDEN], ga[:, FFN_HIDDEN:]
        act = (gt * (1.0 / (1.0 + jnp.exp(-gt))) * a).astype(BF16)
        y = jnp.dot(act, wout_ref[...], preferred_element_type=F32)
        o_ref[0, rows, :] = _layer_norm(ALPHA * h + mv[5:6] * y, g_ref[...], beta_ref[...])


def _ffn(h, modv, w_in, w_out, layer, g, beta, n_rows, tm):
    has_ctx = n_rows == N
    mod_specs = [pl.BlockSpec((1, N_MOD, D), lambda b, i: (b, 0, 0))]
    if has_ctx:
        mod_specs.append(pl.BlockSpec((1, N_MOD, D), lambda b, i: (2, 0, 0)))
    weights_once = dict(pipeline_mode=pl.Buffered(1))
    return pl.pallas_call(
        functools.partial(_ffn_kernel, n_sub=tm // TM, has_ctx=has_ctx),
        out_shape=jax.ShapeDtypeStruct((B, n_rows, D), F32),
        grid=(B, n_rows // tm),
        in_specs=[pl.BlockSpec((1, tm, D), lambda b, i: (b, i, 0))] + mod_specs + [
            pl.BlockSpec((None, D, 2 * FFN_HIDDEN), lambda b, i: (layer, 0, 0), **weights_once),
            pl.BlockSpec((None, FFN_HIDDEN, D), lambda b, i: (layer, 0, 0), **weights_once),
            pl.BlockSpec((1, D), lambda b, i: (0, 0)),
            pl.BlockSpec((1, D), lambda b, i: (0, 0)),
        ],
        out_specs=pl.BlockSpec((1, tm, D), lambda b, i: (b, i, 0)),
        compiler_params=_cparams(2),
        name="ffn_ln",
    )(*([h] + [modv] * len(mod_specs) + [w_in, w_out, g.reshape(1, D), beta.reshape(1, D)]))


def _gelu(x):
    return 0.5 * x * (1.0 + lax.erf(x * (1.0 / math.sqrt(2.0))))


def _rms(x, g):
    ms = jnp.mean(x * x, axis=-1, keepdims=True)
    return x * lax.rsqrt(ms + RMS_EPS) * g


def _inproj_odd_kernel(h_ref, mvb_ref, mvc_ref, win_ref, gq_ref, gkv_ref, wuq_ref, wkn_ref, wkr_ref, wv_ref,
                       cosf_ref, sinf_ref, cost_ref, sina_ref, sinb_ref,
                       lng_ref, lnb_ref, ws_ref, gb_ref,
                       q_ref, k_ref, vt_ref, gm_ref):
    nt = (((1,), (1,)), ((), ()))
    n_sub = TM_ALL // TM
    for j in range(n_sub):
        tile = slice(j * TM, (j + 1) * TM)
        mv = _tile_mod(mvb_ref, mvc_ref, j, n_sub)
        u = (h_ref[0, tile, :] * (1.0 + mv[1:2]) + mv[0:1]).astype(BF16)
        p = jnp.dot(u, win_ref[...], preferred_element_type=F32)
        cq = _rms(p[:, :MLA_Q_RANK], gq_ref[...]).astype(BF16)
        ckv = _rms(p[:, MLA_Q_RANK:MLA_Q_RANK + MLA_KV_RANK], gkv_ref[...]).astype(BF16)
        kr = p[:, 640:768]
        qt = lax.dot_general(wuq_ref[...], cq, nt, preferred_element_type=F32)
        qt = qt * ((MLA_NOPE + MLA_ROPE) ** -0.5 * LOG2E)
        cf, sf = cosf_ref[:, tile], sinf_ref[:, tile]
        for hd in range(MLA_HEADS):
            base = hd * 128
            parts = [qt[base:base + 64]]
            for half in range(2):
                x1 = qt[base + 64 + half * 16:base + 72 + half * 16]
                x2 = qt[base + 72 + half * 16:base + 80 + half * 16]
                c = cf[half * 8:(half + 1) * 8]
                sn = sf[half * 8:(half + 1) * 8]
                parts += [x1 * c - x2 * sn, x2 * c + x1 * sn]
            parts.append(qt[base + 96:base + 128])
            q_ref[0, base:base + 128, tile] = jnp.concatenate(parts, axis=0).astype(BF16)
        kr = (kr * cost_ref[tile, :] + pltpu.roll(kr, LANE - 8, 1) * sina_ref[tile, :]
              + pltpu.roll(kr, 8, 1) * sinb_ref[tile, :])
        k = (jnp.dot(ckv, wkn_ref[...], preferred_element_type=F32)
             + jnp.dot(kr.astype(BF16), wkr_ref[...], preferred_element_type=F32))
        k_ref[0, tile, :] = k.astype(BF16)
        vt_ref[0, :, tile] = lax.dot_general(wv_ref[...], ckv, nt, preferred_element_type=F32).astype(BF16)
        gu = _gelu(p[:, 768:1280])
        gv = _layer_norm(_gelu(p[:, 1280:1792]), lng_ref[...], lnb_ref[...]).astype(BF16)
        for ch in range(TM // GMLP_CHUNK):
            rows = slice(ch * GMLP_CHUNK, (ch + 1) * GMLP_CHUNK)
            out_rows = slice(j * TM + ch * GMLP_CHUNK, j * TM + (ch + 1) * GMLP_CHUNK)
            for grp in range(GMLP_GROUPS):
                cols = slice(grp * GMLP_CH, (grp + 1) * GMLP_CH)
                mixed = jnp.dot(ws_ref[grp], gv[rows, cols], preferred_element_type=F32) + gb_ref[grp]
                gm_ref[0, out_rows, cols] = (gu[rows, cols] * mixed).astype(BF16)


def _inproj_odd(h, modv, w, tabs):
    cosf, sinf, cost, sina, sinb = tabs
    const2 = lambda b, i: (0, 0)
    const3 = lambda b, i: (0, 0, 0)
    tm = TM_ALL
    return pl.pallas_call(
        _inproj_odd_kernel,
        out_shape=(jax.ShapeDtypeStruct((B, 1024, N), BF16),
                   jax.ShapeDtypeStruct((B, N, 1024), BF16),
                   jax.ShapeDtypeStruct((B, 512, N), BF16),
                   jax.ShapeDtypeStruct((B, N, 512), BF16)),
        grid=(B, N // tm),
        in_specs=[
            pl.BlockSpec((1, tm, D), lambda b, i: (b, i, 0)),
            pl.BlockSpec((1, N_MOD, D), lambda b, i: (b, 0, 0)),
            pl.BlockSpec((1, N_MOD, D), lambda b, i: (2, 0, 0)),
            pl.BlockSpec((D, 1792), const2),
            pl.BlockSpec((1, MLA_Q_RANK), const2),
            pl.BlockSpec((1, MLA_KV_RANK), const2),
            pl.BlockSpec((1024, MLA_Q_RANK), const2),
            pl.BlockSpec((MLA_KV_RANK, 1024), const2),
            pl.BlockSpec((LANE, 1024), const2),
            pl.BlockSpec((512, MLA_KV_RANK), const2),
            pl.BlockSpec((16, tm), lambda b, i: (0, i)),
            pl.BlockSpec((16, tm), lambda b, i: (0, i)),
            pl.BlockSpec((tm, LANE), lambda b, i: (i, 0)),
            pl.BlockSpec((tm, LANE), lambda b, i: (i, 0)),
            pl.BlockSpec((tm, LANE), lambda b, i: (i, 0)),
            pl.BlockSpec((1, GMLP_WIDTH), const2),
            pl.BlockSpec((1, GMLP_WIDTH), const2),
            pl.BlockSpec((GMLP_GROUPS, GMLP_CHUNK, GMLP_CHUNK), const3),
            pl.BlockSpec((GMLP_GROUPS, GMLP_CHUNK, 1), const3),
        ],
        out_specs=(pl.BlockSpec((1, 1024, tm), lambda b, i: (b, 0, i)),
                   pl.BlockSpec((1, tm, 1024), lambda b, i: (b, i, 0)),
                   pl.BlockSpec((1, 512, tm), lambda b, i: (b, 0, i)),
                   pl.BlockSpec((1, tm, 512), lambda b, i: (b, i, 0))),
        compiler_params=_cparams(2),
        name="inproj_odd",
    )(h, modv, modv, w["win"], w["gq"], w["gkv"], w["wuq"], w["wkn"], w["wkr"], w["wv"],
      cosf, sinf, cost, sina, sinb, w["lng"], w["lnb"], w["ws"], w["gb"])


def _mla_attn_kernel(q_ref, k_ref, vt_ref, o_ref):
    chunks = _latent_chunks(TK_MLA)
    rhs = [q_ref[0, :, t * TQ_MLA:(t + 1) * TQ_MLA] for t in range(TILES_MLA)]
    results, overflow = _flash_fixed_ref(rhs, k_ref, vt_ref, chunks)
    for t, (acc, l) in enumerate(results):
        o_ref[0, :, t * TQ_MLA:(t + 1) * TQ_MLA] = (acc / l).astype(BF16)

    @pl.when(overflow)
    def _recompute():
        def tile(t, carry):
            cols = pl.ds(pl.multiple_of(t * TQ_MLA, TQ_MLA), TQ_MLA)
            acc, l = _flash(q_ref[0, :, cols], k_ref, vt_ref, chunks)
            o_ref[0, :, cols] = (acc / l).astype(BF16)
            return carry
        lax.fori_loop(0, TILES_MLA, tile, 0)


def _mla_attention(q_t, k, v_t):
    bq = TQ_MLA * TILES_MLA
    return pl.pallas_call(
        _mla_attn_kernel,
        out_shape=jax.ShapeDtypeStruct((B, 512, S), BF16),
        grid=(B, MLA_HEADS, S // bq),
        in_specs=[
            pl.BlockSpec((1, 128, bq), lambda b, h, i: (b, h, i)),
            pl.BlockSpec((1, N, 128), lambda b, h, i: (b, 0, h)),
            pl.BlockSpec((1, MLA_V, N), lambda b, h, i: (b, h, 0)),
        ],
        out_specs=pl.BlockSpec((1, MLA_V, bq), lambda b, h, i: (b, h, i)),
        compiler_params=_cparams(3),
        name="mla_attention",
    )(q_t, k, v_t)


def _rope_tables(n_freq):
    t = np.arange(S)
    inv = np.power(np.float32(ROPE_BASE), -np.arange(0, 2 * n_freq, 2, dtype=np.float32) / np.float32(2 * n_freq))
    ang_r = (t // GRID_W).astype(np.float32)[:, None] * inv[None, :]
    ang_c = (t % GRID_W).astype(np.float32)[:, None] * inv[None, :]
    cos = np.concatenate([np.cos(ang_r), np.cos(ang_c)], axis=1)
    sin = np.concatenate([np.sin(ang_r), np.sin(ang_c)], axis=1)
    cos = np.concatenate([cos, np.ones((L, 2 * n_freq), np.float32)], axis=0).astype(np.float32)
    sin = np.concatenate([sin, np.zeros((L, 2 * n_freq), np.float32)], axis=0).astype(np.float32)
    lane = np.arange(LANE)
    period = 4 * n_freq
    src = ((lane % period) // (2 * n_freq)) * n_freq + lane % n_freq
    is_x1 = (lane % (2 * n_freq)) < n_freq
    cos_t = cos[:, src]
    sin_t = sin[:, src]
    sin_a = np.where(is_x1[None, :], -sin_t, 0.0).astype(np.float32)
    sin_b = np.where(is_x1[None, :], 0.0, sin_t).astype(np.float32)
    return np.ascontiguousarray(cos.T), np.ascontiguousarray(sin.T), cos_t, sin_a, sin_b


def _even_weights(ev_w_in):
    aq, ak, av, bq, bk, bv = jnp.split(ev_w_in, [512, 1024, 1536, 2048, 2560], axis=1)
    scale = HEAD_DIM ** -0.5
    wtok = jnp.concatenate([ak, bk], axis=1).astype(BF16)
    wfeat = jnp.concatenate([aq * scale, av, bq * scale, bv], axis=1).T.astype(BF16)
    return wtok, wfeat


def _odd_weights(od_w_in, gq, w_uq, gkv, w_ukv, ln_g, ln_b, ws, gb):
    cq, ckv, kr, gu, gv = jnp.split(od_w_in, [384, 640, 672, 1184], axis=1)
    win = jnp.concatenate([cq, ckv, kr, jnp.zeros((D, LANE - MLA_ROPE), F32), gu, gv], axis=1)
    dq = MLA_NOPE + MLA_ROPE
    wuq = w_uq.reshape(MLA_Q_RANK, MLA_HEADS, dq)
    wuq = jnp.pad(wuq, ((0, 0), (0, 0), (0, LANE - dq))).reshape(MLA_Q_RANK, MLA_HEADS * LANE)
    wukv = w_ukv.reshape(MLA_KV_RANK, MLA_HEADS, MLA_NOPE + MLA_V)
    wkn = jnp.pad(wukv[:, :, :MLA_NOPE], ((0, 0), (0, 0), (0, LANE - MLA_NOPE)))
    wkn = wkn.reshape(MLA_KV_RANK, MLA_HEADS * LANE)
    wv = wukv[:, :, MLA_NOPE:].reshape(MLA_KV_RANK, MLA_HEADS * MLA_V)
    place = np.zeros((LANE, MLA_HEADS * LANE), np.float32)
    for hd in range(MLA_HEADS):
        place[np.arange(MLA_ROPE), hd * LANE + MLA_NOPE + np.arange(MLA_ROPE)] = 1.0
    return {
        "win": win.astype(BF16), "gq": gq.reshape(1, -1), "gkv": gkv.reshape(1, -1),
        "wuq": wuq.T.astype(BF16), "wkn": wkn.astype(BF16), "wkr": jnp.asarray(place, BF16),
        "wv": wv.T.astype(BF16), "lng": ln_g.reshape(1, -1), "lnb": ln_b.reshape(1, -1),
        "ws": ws.astype(BF16), "gb": gb.reshape(GMLP_GROUPS, GMLP_CHUNK, 1),
    }


def kernel(x, c, ctx, c_ctx, mod_w, mod_b, ln_mix_g, ln_mix_b, ln_ffn_g, ln_ffn_b, ffn_w_in, ffn_w_out,
           ev_w_in, ev_w_out, diff_lambda, diff_subln_g, na_rpb, od_w_in, od_w_out, mla_q_norm_g,
           mla_w_uq, mla_kv_norm_g, mla_w_ukv, gmlp_ln_g, gmlp_ln_b, gmlp_ws, gmlp_b):
    cond = jnp.concatenate([c, c_ctx[None], jnp.zeros((8 - B - 1, D), F32)], axis=0)
    mod = _modulation(cond, mod_w, mod_b).reshape(DEPTH, 8, N_MOD, D)

    wtok, wfeat = _even_weights(ev_w_in[0])
    cosf, sinf, cost, sina, sinb = _rope_tables(16)
    tok, feat = _inproj_even(x, ctx, mod[0], wtok, wfeat,
                             (cosf * np.float32(LOG2E), sinf * np.float32(LOG2E), cost, sina, sinb))
    h = _outproj_even(_diff_attention(tok, feat, diff_lambda[0], diff_subln_g[0], False),
                      _diff_attention(tok, feat, diff_lambda[0], diff_subln_g[0], True),
                      _na_attention(tok, feat, _na_bias_table(na_rpb[0])),
                      _na_attention_ctx(tok, feat),
                      ev_w_out[0].astype(BF16), x, ctx, mod[0], ln_mix_g[0], ln_mix_b[0])
    ffn_w_in, ffn_w_out = ffn_w_in.astype(BF16), ffn_w_out.astype(BF16)
    h = _ffn(h, mod[0], ffn_w_in, ffn_w_out, 0, ln_ffn_g[0], ln_ffn_b[0], N, TM_ALL)

    w1 = _odd_weights(od_w_in[0], mla_q_norm_g[0], mla_w_uq[0], mla_kv_norm_g[0], mla_w_ukv[0],
                      gmlp_ln_g[0], gmlp_ln_b[0], gmlp_ws[0], gmlp_b[0])
    q_t, k, v_t, gm = _inproj_odd(h, mod[1], w1, _rope_tables(8))
    mix_a = _mla_attention(q_t, k, v_t)
    h = _outproj_odd(mix_a, gm, od_w_out[0].astype(BF16), h, mod[1], ln_mix_g[1], ln_mix_b[1])
    return _ffn(h, mod[1], ffn_w_in, ffn_w_out, 1, ln_ffn_g[1], ln_ffn_b[1], S, TM_LAT)
```

```python
import functools
import math

import jax
import jax.numpy as jnp
import numpy as np
from jax import lax
from jax.experimental import pallas as pl
from jax.experimental.pallas import tpu as pltpu

F32 = jnp.float32
BF16 = jnp.bfloat16

D = 1024
B = 2
S = 8192
L = 256
N = S + L
DEPTH = 2
GRID_W = 64
ROWS = S // GRID_W
ALPHA = (2 * DEPTH) ** 0.25
ROPE_BASE = 10000.0
LN_EPS = 1e-5
RMS_EPS = 1e-6
N_MOD = 6
HEAD_DIM = 64
DIFF_HEADS = 4
DIFF_V = 128
NA_HEADS = 8
NA_ROWS = 8
NA_COLS = 16
MLA_HEADS = 8
MLA_Q_RANK = 384
MLA_KV_RANK = 256
MLA_NOPE = 64
MLA_ROPE = 32
MLA_V = 64
GMLP_GROUPS = 4
GMLP_CH = 128
GMLP_CHUNK = 128
GMLP_WIDTH = 512
FFN_HIDDEN = 2816
LAMBDA_INIT_0 = 0.8 - 0.6 * math.exp(-0.3 * 0)
LOG2E = math.log2(math.e)

LANE = 128
TM = 256
TM_ALL = 3 * TM
TM_LAT = 4 * TM
NT_ALL = N // TM
NT_LAT = S // TM
TQ_DIFF, TK_DIFF = 256, 256
TQ_MLA, TK_MLA = 512, 256
TILES_DIFF, TILES_MLA = 8, 8
TQ_NA = 2048
NA_WIN_ROWS = 10
NA_WIN = NA_WIN_ROWS * GRID_W
NA_PATTERNS = 5
NEG = -1e30
VMEM_LIMIT = 56 * 1024 * 1024


def _cparams(n_axes):
    return pltpu.CompilerParams(dimension_semantics=("parallel",) * n_axes,
                                vmem_limit_bytes=VMEM_LIMIT)


def _mod_row(b, i):
    return jnp.where(i < NT_LAT, b, 2)


def _mod_kernel(cond_ref, w_ref, b_ref, o_ref):
    cnd = cond_ref[...]
    act = cnd * (1.0 / (1.0 + jnp.exp(-cnd)))
    o_ref[0] = jnp.dot(act, w_ref[0], preferred_element_type=F32) + b_ref[0]


def _modulation(cond, mod_w, mod_b):
    tn = 1536
    return pl.pallas_call(
        _mod_kernel,
        out_shape=jax.ShapeDtypeStruct((DEPTH, 8, N_MOD * D), F32),
        grid=(DEPTH, N_MOD * D // tn),
        in_specs=[
            pl.BlockSpec((8, D), lambda l, j: (0, 0)),
            pl.BlockSpec((1, D, tn), lambda l, j: (l, 0, j)),
            pl.BlockSpec((1, 1, tn), lambda l, j: (l, 0, j)),
        ],
        out_specs=pl.BlockSpec((1, 8, tn), lambda l, j: (l, 0, j)),
        compiler_params=_cparams(2),
        name="modulation",
    )(cond, mod_w, mod_b.reshape(DEPTH, 1, N_MOD * D))


def _modulated(h_ref, mv_ref, k):
    mv = mv_ref[0]
    return (h_ref[0] * (1.0 + mv[k + 1:k + 2]) + mv[k:k + 1]).astype(BF16)


def _tile_mod(mvb_ref, mvc_ref, j, n_sub):
    if mvc_ref is None or j != n_sub - 1:
        return mvb_ref[0]
    is_ctx = pl.program_id(1) == pl.num_programs(1) - 1
    return jnp.where(is_ctx, mvc_ref[0], mvb_ref[0])


def _layer_norm(x, g, b):
    mu = jnp.mean(x, axis=-1, keepdims=True)
    xc = x - mu
    var = jnp.mean(xc * xc, axis=-1, keepdims=True)
    return xc * lax.rsqrt(var + LN_EPS) * g + b


def _inproj_even_kernel(x_ref, c_ref, mv_ref, wtok_ref, wfeat_ref, cosf_ref, sinf_ref,
                        cost_ref, sina_ref, sinb_ref, tok_ref, feat_ref, u_ref):
    i = pl.program_id(1)

    @pl.when(i < NT_LAT)
    def _latent():
        u_ref[...] = _modulated(x_ref, mv_ref, 0)

    @pl.when(i == NT_LAT)
    def _context():
        u_ref[...] = _modulated(c_ref, mv_ref, 0)

    u = u_ref[...]
    tok = jnp.dot(u, wtok_ref[...], preferred_element_type=F32)
    ct, sa, sb = cost_ref[...], sina_ref[...], sinb_ref[...]
    for j in range(4):
        xs = tok[:, j * LANE:(j + 1) * LANE]
        ro = xs * ct + pltpu.roll(xs, LANE - 16, 1) * sa + pltpu.roll(xs, 16, 1) * sb
        tok_ref[0, :, j * LANE:(j + 1) * LANE] = ro.astype(BF16)
    tok_ref[0, :, 512:] = tok[:, 512:].astype(BF16)
    feat = lax.dot_general(wfeat_ref[...], u, (((1,), (1,)), ((), ())),
                           preferred_element_type=F32)
    cf, sf = cosf_ref[...], sinf_ref[...]
    for blk in range(8):
        for half in range(2):
            base = blk * 64 + half * 32
            x1 = feat[base:base + 16]
            x2 = feat[base + 16:base + 32]
            c = cf[half * 16:(half + 1) * 16]
            s = sf[half * 16:(half + 1) * 16]
            feat_ref[0, base:base + 16, :] = (x1 * c - x2 * s).astype(BF16)
            feat_ref[0, base + 16:base + 32, :] = (x2 * c + x1 * s).astype(BF16)
    feat_ref[0, 512:1024, :] = feat[512:1024].astype(BF16)
    feat_ref[0, 1024:1536, :] = (feat[1024:1536] * LOG2E).astype(BF16)
    feat_ref[0, 1536:, :] = feat[1536:].astype(BF16)


def _lat_tile(b, i):
    return (b, jnp.minimum(i, NT_LAT - 1), 0)


def _inproj_even(x, ctx, modv, wtok, wfeat, tabs):
    cosf, sinf, cost, sina, sinb = tabs
    const = lambda b, i: (0, 0)
    return pl.pallas_call(
        _inproj_even_kernel,
        out_shape=(jax.ShapeDtypeStruct((B, N, 1024), BF16),
                   jax.ShapeDtypeStruct((B, 2048, N), BF16)),
        grid=(B, NT_ALL),
        in_specs=[
            pl.BlockSpec((1, TM, D), _lat_tile),
            pl.BlockSpec((1, L, D), lambda b, i: (b, 0, 0)),
            pl.BlockSpec((1, N_MOD, D), lambda b, i: (_mod_row(b, i), 0, 0)),
            pl.BlockSpec((D, 1024), const),
            pl.BlockSpec((2048, D), const),
            pl.BlockSpec((32, TM), lambda b, i: (0, i)),
            pl.BlockSpec((32, TM), lambda b, i: (0, i)),
            pl.BlockSpec((TM, LANE), lambda b, i: (i, 0)),
            pl.BlockSpec((TM, LANE), lambda b, i: (i, 0)),
            pl.BlockSpec((TM, LANE), lambda b, i: (i, 0)),
        ],
        out_specs=(pl.BlockSpec((1, TM, 1024), lambda b, i: (b, i, 0)),
                   pl.BlockSpec((1, 2048, TM), lambda b, i: (b, 0, i))),
        scratch_shapes=[pltpu.VMEM((TM, D), BF16)],
        compiler_params=_cparams(2),
        name="inproj_even",
    )(x, ctx, modv, wtok, wfeat, cosf, sinf, cost, sina, sinb)


def _block_diag2(q_t):
    n = q_t.shape[1]
    z = jnp.zeros((64, n), q_t.dtype)
    left = jnp.concatenate([q_t[:64], z], axis=0)
    right = jnp.concatenate([z, q_t[64:]], axis=0)
    return jnp.concatenate([left, right], axis=1)


ONES_ROWS = 16


def _flash(rhs, k_ref, vt_ref, chunks):
    width = rhs.shape[1]
    dv = vt_ref.shape[1]
    m = jnp.full((1, width), -jnp.inf, F32)
    acc = jnp.zeros((dv + ONES_ROWS, width), F32)

    def scores(idx):
        st, sz = chunks[idx]
        return jnp.dot(k_ref[0, st:st + sz, :], rhs, preferred_element_type=F32)

    s_next = scores(0)
    for idx, (st, sz) in enumerate(chunks):
        s = s_next
        if idx + 1 < len(chunks):
            s_next = scores(idx + 1)
        m_new = jnp.maximum(m, jnp.max(s, axis=0, keepdims=True))
        alpha = jnp.exp2(m - m_new)
        p = jnp.exp2(s - m_new).astype(BF16)
        vt1 = jnp.concatenate([vt_ref[0, :, st:st + sz], jnp.ones((ONES_ROWS, sz), BF16)], axis=0)
        acc = acc * alpha + jnp.dot(vt1, p, preferred_element_type=F32)
        m = m_new
    return acc[:dv], acc[dv:dv + 1]


SCORES_AHEAD = 2
P_LIMIT = 2.0 ** 64


def _flash_fixed_ref(rhs_tiles, k_ref, vt_ref, chunks):
    n = len(chunks)
    items = [(t, idx) for t in range(len(rhs_tiles)) for idx in range(n)]

    def scores(item):
        t, idx = item
        st, sz = chunks[idx]
        return jnp.dot(k_ref[0, st:st + sz, :], rhs_tiles[t], preferred_element_type=F32)

    pending = [scores(it) for it in items[:SCORES_AHEAD]]
    results = []
    worst = None
    for j, (t, idx) in enumerate(items):
        s = pending.pop(0)
        if j + SCORES_AHEAD < len(items):
            pending.append(scores(items[j + SCORES_AHEAD]))
        width = s.shape[1]
        if idx == 0:
            m_ref = jnp.max(s, axis=0, keepdims=True)
        p = jnp.exp2(s - m_ref)
        l8_new = jnp.sum(p.reshape(-1, 8, width), axis=0)
        st, sz = chunks[idx]
        pv = jnp.dot(vt_ref[0, :, st:st + sz], p.astype(BF16), preferred_element_type=F32)
        l8, acc = (l8_new, pv) if idx == 0 else (l8 + l8_new, acc + pv)
        if idx == n - 1:
            l = jnp.sum(l8, axis=0, keepdims=True)
            results.append((acc, l))
            l_max = jnp.max(l)
            worst = l_max if worst is None else jnp.maximum(worst, l_max)
    return results, worst > P_LIMIT


def _latent_chunks(tk):
    return [(j * tk, tk) for j in range(S // tk)] + [(S, L)]


def _diff_attn_kernel(lam_ref, g_ref, q_ref, k_ref, vt_ref, o_ref, *, chunks, tq):
    n_tiles = q_ref.shape[2] // tq

    def finish(acc, l):
        lf = lam_ref[...]
        lam = (jnp.exp(jnp.sum(lf[0:1] * lf[1:2], axis=1, keepdims=True))
               - jnp.exp(jnp.sum(lf[2:3] * lf[3:4], axis=1, keepdims=True)) + LAMBDA_INIT_0)
        o = acc[:, :tq] / l[:, :tq] - lam * (acc[:, tq:] / l[:, tq:])
        ms = jnp.mean(o * o, axis=0, keepdims=True)
        o = o * lax.rsqrt(ms + RMS_EPS) * g_ref[...] * (1.0 - LAMBDA_INIT_0)
        return o.astype(BF16)

    rhs = [_block_diag2(q_ref[0, :, t * tq:(t + 1) * tq]) for t in range(n_tiles)]
    results, overflow = _flash_fixed_ref(rhs, k_ref, vt_ref, chunks)
    for t, (acc, l) in enumerate(results):
        o_ref[0, :, t * tq:(t + 1) * tq] = finish(acc, l)

    @pl.when(overflow)
    def _recompute():
        def tile(t, carry):
            cols = pl.ds(pl.multiple_of(t * tq, tq), tq)
            o_ref[0, :, cols] = finish(*_flash(_block_diag2(q_ref[0, :, cols]), k_ref, vt_ref, chunks))
            return carry
        lax.fori_loop(0, n_tiles, tile, 0)


def _diff_attention(tok, feat, diff_lambda, subln_g, context):
    if context:
        tq, bq, nq, chunks = L, L, 1, [(0, L)]
        q_spec = pl.BlockSpec((1, 128, L), lambda b, h, i: (b, h, S // L))
        k_spec = pl.BlockSpec((1, L, 128), lambda b, h, i: (b, S // L, h))
        vt_spec = pl.BlockSpec((1, 128, L), lambda b, h, i: (b, 4 + h, S // L))
    else:
        tq, bq, chunks = TQ_DIFF, TQ_DIFF * TILES_DIFF, _latent_chunks(TK_DIFF)
        nq = S // bq
        q_spec = pl.BlockSpec((1, 128, bq), lambda b, h, i: (b, h, i))
        k_spec = pl.BlockSpec((1, N, 128), lambda b, h, i: (b, 0, h))
        vt_spec = pl.BlockSpec((1, 128, N), lambda b, h, i: (b, 4 + h, 0))
    return pl.pallas_call(
        functools.partial(_diff_attn_kernel, chunks=chunks, tq=tq),
        out_shape=jax.ShapeDtypeStruct((B, 512, nq * bq), BF16),
        grid=(B, DIFF_HEADS, nq),
        in_specs=[
            pl.BlockSpec((4, HEAD_DIM), lambda b, h, i: (0, 0)),
            pl.BlockSpec((DIFF_V, 1), lambda b, h, i: (0, 0)),
            q_spec, k_spec, vt_spec,
        ],
        out_specs=pl.BlockSpec((1, 128, bq), lambda b, h, i: (b, h, i)),
        compiler_params=_cparams(3),
        name="diff_attention_ctx" if context else "diff_attention",
    )(diff_lambda, subln_g.reshape(DIFF_V, 1), feat, tok, feat)


def _na_softmax_pv(s_c, vt_ctx, s_w=None, vt_win=None):
    m = jnp.max(s_c, axis=0, keepdims=True)
    if s_w is not None:
        m = jnp.maximum(m, jnp.max(s_w, axis=0, keepdims=True))
    p_c = jnp.exp2(s_c - m)
    l = jnp.sum(p_c, axis=0, keepdims=True)
    o = jnp.dot(vt_ctx, p_c.astype(BF16), preferred_element_type=F32)
    if s_w is not None:
        p_w = jnp.exp2(s_w - m)
        l = l + jnp.sum(p_w, axis=0, keepdims=True)
        o = o + jnp.dot(vt_win, p_w.astype(BF16), preferred_element_type=F32)
    o = o / l
    n = s_c.shape[1] // 2
    return jnp.concatenate([o[:64, :n], o[64:, n:]], axis=0).astype(BF16)


def _na_kernel(q_ref, k_ref, vt_ref, bias_ref, o_ref):
    i = pl.program_id(2)
    k_ctx = k_ref[0, S:, :]
    vt_ctx = vt_ref[0, :, S:]
    n_pairs = TQ_NA // LANE

    def scores(jj):
        r = 2 * (i * n_pairs + jj)
        start = jnp.clip(r - NA_ROWS // 2, 0, ROWS - NA_WIN_ROWS)
        pat = jnp.where(r == 0, 0, jnp.where(r == 2, 1, jnp.where(
            r == ROWS - 4, 3, jnp.where(r == ROWS - 2, 4, 2))))
        off = pl.multiple_of(start * GRID_W, LANE)
        rhs = _block_diag2(q_ref[0, :, jj * LANE:(jj + 1) * LANE])
        s_c = jnp.dot(k_ctx, rhs, preferred_element_type=F32)
        s_w = jnp.dot(k_ref[0, pl.ds(off, NA_WIN), :], rhs, preferred_element_type=F32) + bias_ref[0, pat]
        return s_c, s_w, off

    pending = [scores(jj) for jj in range(min(SCORES_AHEAD, n_pairs))]
    for jj in range(n_pairs):
        s_c, s_w, off = pending.pop(0)
        if jj + SCORES_AHEAD < n_pairs:
            pending.append(scores(jj + SCORES_AHEAD))
        o_ref[0, :, jj * LANE:(jj + 1) * LANE] = _na_softmax_pv(
            s_c, vt_ctx, s_w, vt_ref[0, :, pl.ds(off, NA_WIN)])


def _na_ctx_kernel(q_ref, k_ref, vt_ref, o_ref):
    s_c = jnp.dot(k_ref[0], _block_diag2(q_ref[0]), preferred_element_type=F32)
    o_ref[0] = _na_softmax_pv(s_c, vt_ref[0])


def _na_attention(tok, feat, bias):
    return pl.pallas_call(
        _na_kernel,
        out_shape=jax.ShapeDtypeStruct((B, 512, S), BF16),
        grid=(B, NA_HEADS // 2, S // TQ_NA),
        in_specs=[
            pl.BlockSpec((1, 128, TQ_NA), lambda b, h, i: (b, 8 + h, i)),
            pl.BlockSpec((1, N, 128), lambda b, h, i: (b, 0, 4 + h)),
            pl.BlockSpec((1, 128, N), lambda b, h, i: (b, 12 + h, 0)),
            pl.BlockSpec((1, NA_PATTERNS, NA_WIN, 2 * LANE), lambda b, h, i: (h, 0, 0, 0)),
        ],
        out_specs=pl.BlockSpec((1, 128, TQ_NA), lambda b, h, i: (b, h, i)),
        compiler_params=_cparams(3),
        name="na_attention",
    )(feat, tok, feat, bias)


def _na_attention_ctx(tok, feat):
    return pl.pallas_call(
        _na_ctx_kernel,
        out_shape=jax.ShapeDtypeStruct((B, 512, L), BF16),
        grid=(B, NA_HEADS // 2),
        in_specs=[
            pl.BlockSpec((1, 128, L), lambda b, h: (b, 8 + h, S // L)),
            pl.BlockSpec((1, L, 128), lambda b, h: (b, S // L, 4 + h)),
            pl.BlockSpec((1, 128, L), lambda b, h: (b, 12 + h, S // L)),
        ],
        out_specs=pl.BlockSpec((1, 128, L), lambda b, h: (b, h, 0)),
        compiler_params=_cparams(2),
        name="na_attention_ctx",
    )(feat, tok, feat)


def _na_bias_table(rpb):
    pats = [(0, 0), (2, 0), (8, 4), (ROWS - 4, ROWS - NA_WIN_ROWS), (ROWS - 2, ROWS - NA_WIN_ROWS)]
    n_off = 2 * NA_COLS - 1
    kr_rel = np.arange(NA_WIN_ROWS)
    j = np.arange(2)
    drow = np.zeros((NA_PATTERNS, NA_WIN_ROWS, 2), np.int32)
    row_ok = np.zeros((NA_PATTERNS, NA_WIN_ROWS, 2), bool)
    for p, (r, start) in enumerate(pats):
        rq = r + j
        rs = np.clip(rq - NA_ROWS // 2, 0, ROWS - NA_ROWS)
        kr = start + kr_rel
        row_ok[p] = (kr[:, None] >= rs[None, :]) & (kr[:, None] < rs[None, :] + NA_ROWS)
        drow[p] = np.clip(kr[:, None] - rq[None, :] + NA_ROWS - 1, 0, 2 * NA_ROWS - 2)
    kc = np.arange(GRID_W)
    c = np.arange(GRID_W)
    cs = np.clip(c - NA_COLS // 2, 0, GRID_W - NA_COLS)
    col_ok = (kc[:, None] >= cs[None, :]) & (kc[:, None] < cs[None, :] + NA_COLS)
    dcol = kc[:, None] - c[None, :] + NA_COLS - 1
    hit = (dcol[None] == np.arange(n_off)[:, None, None]) & col_ok[None]
    expand = np.zeros((4 * n_off + 3, GRID_W, 2, 2, GRID_W), np.float32)
    for hh in range(2):
        for jj in range(2):
            expand[(hh * 2 + jj) * n_off:(hh * 2 + jj + 1) * n_off, :, hh, jj, :] = hit
    for jj in range(2):
        expand[4 * n_off + jj, :, :, jj, :] = NEG
    expand[4 * n_off + 2] = np.where(col_ok, 0.0, NEG)[:, None, None, :]
    expand = expand.reshape(4 * n_off + 3, GRID_W, 2 * LANE)
    rows = rpb[:, drow.reshape(-1), :] * LOG2E
    rows = rows.reshape(NA_HEADS // 2, 2, NA_PATTERNS * NA_WIN_ROWS, 2, n_off)
    rows = rows.transpose(0, 2, 1, 3, 4).reshape(NA_HEADS // 2, -1, 4 * n_off)
    flags = np.concatenate([(~row_ok).reshape(-1, 2), np.ones((NA_PATTERNS * NA_WIN_ROWS, 1), bool)], 1)
    flags = jnp.broadcast_to(jnp.asarray(flags, F32)[None], (NA_HEADS // 2,) + flags.shape)
    rows = jnp.concatenate([rows, flags], axis=2).reshape(-1, 4 * n_off + 3)
    t = jnp.einsum("gd,dkl->gkl", rows, jnp.asarray(expand), precision=lax.Precision.HIGHEST)
    return t.reshape(NA_HEADS // 2, NA_PATTERNS, NA_WIN, 2 * LANE)


def _outproj_odd_kernel(a_ref, b_ref, w_ref, h_ref, mv_ref, g_ref, beta_ref, o_ref):
    gate = mv_ref[0][2:3]
    for j in range(TM_LAT // TM):
        rows = slice(j * TM, (j + 1) * TM)
        y = (lax.dot_general(a_ref[0, :, rows], w_ref[:512, :], (((0,), (0,)), ((), ())),
                             preferred_element_type=F32)
             + jnp.dot(b_ref[0, rows, :], w_ref[512:, :], preferred_element_type=F32))
        o_ref[0, rows, :] = _layer_norm(ALPHA * h_ref[0, rows, :] + gate * y, g_ref[...], beta_ref[...])


def _outproj_even_kernel(al_ref, ac_ref, bl_ref, bc_ref, w_ref, x_ref, c_ref, mv_ref, g_ref, beta_ref,
                         o_ref):
    i = pl.program_id(1)
    ta = (((0,), (0,)), ((), ()))

    def project(a_ref, b_ref, h_ref):
        y = (lax.dot_general(a_ref[0], w_ref[:512, :], ta, preferred_element_type=F32)
             + lax.dot_general(b_ref[0], w_ref[512:, :], ta, preferred_element_type=F32))
        gate = mv_ref[0][2:3]
        o_ref[0] = _layer_norm(ALPHA * h_ref[0] + gate * y, g_ref[...], beta_ref[...])

    @pl.when(i < NT_LAT)
    def _latent():
        project(al_ref, bl_ref, x_ref)

    @pl.when(i == NT_LAT)
    def _context():
        project(ac_ref, bc_ref, c_ref)


def _outproj_even(a_lat, a_ctx, b_lat, b_ctx, w, x, ctx, modv, g, beta):
    lat_mix = pl.BlockSpec((1, 512, TM), lambda b, i: (b, 0, jnp.minimum(i, NT_LAT - 1)))
    ctx_mix = pl.BlockSpec((1, 512, L), lambda b, i: (b, 0, 0))
    return pl.pallas_call(
        _outproj_even_kernel,
        out_shape=jax.ShapeDtypeStruct((B, N, D), F32),
        grid=(B, NT_ALL),
        in_specs=[
            lat_mix, ctx_mix, lat_mix, ctx_mix,
            pl.BlockSpec((D, D), lambda b, i: (0, 0)),
            pl.BlockSpec((1, TM, D), _lat_tile),
            pl.BlockSpec((1, L, D), lambda b, i: (b, 0, 0)),
            pl.BlockSpec((1, N_MOD, D), lambda b, i: (_mod_row(b, i), 0, 0)),
            pl.BlockSpec((1, D), lambda b, i: (0, 0)),
            pl.BlockSpec((1, D), lambda b, i: (0, 0)),
        ],
        out_specs=pl.BlockSpec((1, TM, D), lambda b, i: (b, i, 0)),
        compiler_params=_cparams(2),
        name="outproj_even_ln",
    )(a_lat, a_ctx, b_lat, b_ctx, w, x, ctx, modv, g.reshape(1, D), beta.reshape(1, D))


def _outproj_odd(mix_a, mix_b, w, h, modv, g, beta):
    return pl.pallas_call(
        _outproj_odd_kernel,
        out_shape=jax.ShapeDtypeStruct((B, S, D), F32),
        grid=(B, S // TM_LAT),
        in_specs=[
            pl.BlockSpec((1, 512, TM_LAT), lambda b, i: (b, 0, i)),
            pl.BlockSpec((1, TM_LAT, 512), lambda b, i: (b, i, 0)),
            pl.BlockSpec((D, D), lambda b, i: (0, 0)),
            pl.BlockSpec((1, TM_LAT, D), lambda b, i: (b, i, 0)),
            pl.BlockSpec((1, N_MOD, D), lambda b, i: (b, 0, 0)),
            pl.BlockSpec((1, D), lambda b, i: (0, 0)),
            pl.BlockSpec((1, D), lambda b, i: (0, 0)),
        ],
        out_specs=pl.BlockSpec((1, TM_LAT, D), lambda b, i: (b, i, 0)),
        compiler_params=_cparams(2),
        name="outproj_odd_ln",
    )(mix_a, mix_b, w, h, modv, g.reshape(1, D), beta.reshape(1, D))


def _ffn_kernel(*refs, n_sub, has_ctx):
    if has_ctx:
        h_ref, mvb_ref, mvc_ref, win_ref, wout_ref, g_ref, beta_ref, o_ref = refs
    else:
        h_ref, mvb_ref, win_ref, wout_ref, g_ref, beta_ref, o_ref = refs
        mvc_ref = None
    for j in range(n_sub):
        rows = slice(j * TM, (j + 1) * TM)
        mv = _tile_mod(mvb_ref, mvc_ref, j, n_sub)
        h = h_ref[0, rows, :]
        u = (h * (1.0 + mv[4:5]) + mv[3:4]).astype(BF16)
        ga = jnp.dot(u, win_ref[...], preferred_element_type=F32)
        gt, a = ga[:, :FFN_HIDDEN], ga[:, FFN_HIDDEN:]
        act = (gt * (1.0 / (1.0 + jnp.exp(-gt))) * a).astype(BF16)
        y = jnp.dot(act, wout_ref[...], preferred_element_type=F32)
        o_ref[0, rows, :] = _layer_norm(ALPHA * h + mv[5:6] * y, g_ref[...], beta_ref[...])


def _ffn(h, modv, w_in, w_out, layer, g, beta, n_rows, tm):
    has_ctx = n_rows == N
    mod_specs = [pl.BlockSpec((1, N_MOD, D), lambda b, i: (b, 0, 0))]
    if has_ctx:
        mod_specs.append(pl.BlockSpec((1, N_MOD, D), lambda b, i: (2, 0, 0)))
    weights_once = dict(pipeline_mode=pl.Buffered(1))
    return pl.pallas_call(
        functools.partial(_ffn_kernel, n_sub=tm // TM, has_ctx=has_ctx),
        out_shape=jax.ShapeDtypeStruct((B, n_rows, D), F32),
        grid=(B, n_rows // tm),
        in_specs=[pl.BlockSpec((1, tm, D), lambda b, i: (b, i, 0))] + mod_specs + [
            pl.BlockSpec((None, D, 2 * FFN_HIDDEN), lambda b, i: (layer, 0, 0), **weights_once),
            pl.BlockSpec((None, FFN_HIDDEN, D), lambda b, i: (layer, 0, 0), **weights_once),
            pl.BlockSpec((1, D), lambda b, i: (0, 0)),
            pl.BlockSpec((1, D), lambda b, i: (0, 0)),
        ],
        out_specs=pl.BlockSpec((1, tm, D), lambda b, i: (b, i, 0)),
        compiler_params=_cparams(2),
        name="ffn_ln",
    )(*([h] + [modv] * len(mod_specs) + [w_in, w_out, g.reshape(1, D), beta.reshape(1, D)]))


def _gelu(x):
    return 0.5 * x * (1.0 + lax.erf(x * (1.0 / math.sqrt(2.0))))


def _rms(x, g):
    ms = jnp.mean(x * x, axis=-1, keepdims=True)
    return x * lax.rsqrt(ms + RMS_EPS) * g


def _inproj_odd_kernel(h_ref, mvb_ref, mvc_ref, win_ref, gq_ref, gkv_ref, wuq_ref, wkn_ref, wkr_ref, wv_ref,
                       cosf_ref, sinf_ref, cost_ref, sina_ref, sinb_ref,
                       lng_ref, lnb_ref, ws_ref, gb_ref,
                       q_ref, k_ref, vt_ref, gm_ref):
    nt = (((1,), (1,)), ((), ()))
    n_sub = TM_ALL // TM
    for j in range(n_sub):
        tile = slice(j * TM, (j + 1) * TM)
        mv = _tile_mod(mvb_ref, mvc_ref, j, n_sub)
        u = (h_ref[0, tile, :] * (1.0 + mv[1:2]) + mv[0:1]).astype(BF16)
        p = jnp.dot(u, win_ref[...], preferred_element_type=F32)
        cq = _rms(p[:, :MLA_Q_RANK], gq_ref[...]).astype(BF16)
        ckv = _rms(p[:, MLA_Q_RANK:MLA_Q_RANK + MLA_KV_RANK], gkv_ref[...]).astype(BF16)
        kr = p[:, 640:768]
        qt = lax.dot_general(wuq_ref[...], cq, nt, preferred_element_type=F32)
        qt = qt * ((MLA_NOPE + MLA_ROPE) ** -0.5 * LOG2E)
        cf, sf = cosf_ref[:, tile], sinf_ref[:, tile]
        for hd in range(MLA_HEADS):
            base = hd * 128
            parts = [qt[base:base + 64]]
            for half in range(2):
                x1 = qt[base + 64 + half * 16:base + 72 + half * 16]
                x2 = qt[base + 72 + half * 16:base + 80 + half * 16]
                c = cf[half * 8:(half + 1) * 8]
                sn = sf[half * 8:(half + 1) * 8]
                parts += [x1 * c - x2 * sn, x2 * c + x1 * sn]
            parts.append(qt[base + 96:base + 128])
            q_ref[0, base:base + 128, tile] = jnp.concatenate(parts, axis=0).astype(BF16)
        kr = (kr * cost_ref[tile, :] + pltpu.roll(kr, LANE - 8, 1) * sina_ref[tile, :]
              + pltpu.roll(kr, 8, 1) * sinb_ref[tile, :])
        k = (jnp.dot(ckv, wkn_ref[...], preferred_element_type=F32)
             + jnp.dot(kr.astype(BF16), wkr_ref[...], preferred_element_type=F32))
        k_ref[0, tile, :] = k.astype(BF16)
        vt_ref[0, :, tile] = lax.dot_general(wv_ref[...], ckv, nt, preferred_element_type=F32).astype(BF16)
        gu = _gelu(p[:, 768:1280])
        gv = _layer_norm(_gelu(p[:, 1280:1792]), lng_ref[...], lnb_ref[...]).astype(BF16)
        for ch in range(TM // GMLP_CHUNK):
            rows = slice(ch * GMLP_CHUNK, (ch + 1) * GMLP_CHUNK)
            out_rows = slice(j * TM + ch * GMLP_CHUNK, j * TM + (ch + 1) * GMLP_CHUNK)
            for grp in range(GMLP_GROUPS):
                cols = slice(grp * GMLP_CH, (grp + 1) * GMLP_CH)
                mixed = jnp.dot(ws_ref[grp], gv[rows, cols], preferred_element_type=F32) + gb_ref[grp]
                gm_ref[0, out_rows, cols] = (gu[rows, cols] * mixed).astype(BF16)


def _inproj_odd(h, modv, w, tabs):
    cosf, sinf, cost, sina, sinb = tabs
    const2 = lambda b, i: (0, 0)
    const3 = lambda b, i: (0, 0, 0)
    tm = TM_ALL
    return pl.pallas_call(
        _inproj_odd_kernel,
        out_shape=(jax.ShapeDtypeStruct((B, 1024, N), BF16),
                   jax.ShapeDtypeStruct((B, N, 1024), BF16),
                   jax.ShapeDtypeStruct((B, 512, N), BF16),
                   jax.ShapeDtypeStruct((B, N, 512), BF16)),
        grid=(B, N // tm),
        in_specs=[
            pl.BlockSpec((1, tm, D), lambda b, i: (b, i, 0)),
            pl.BlockSpec((1, N_MOD, D), lambda b, i: (b, 0, 0)),
            pl.BlockSpec((1, N_MOD, D), lambda b, i: (2, 0, 0)),
            pl.BlockSpec((D, 1792), const2),
            pl.BlockSpec((1, MLA_Q_RANK), const2),
            pl.BlockSpec((1, MLA_KV_RANK), const2),
            pl.BlockSpec((1024, MLA_Q_RANK), const2),
            pl.BlockSpec((MLA_KV_RANK, 1024), const2),
            pl.BlockSpec((LANE, 1024), const2),
            pl.BlockSpec((512, MLA_KV_RANK), const2),
            pl.BlockSpec((16, tm), lambda b, i: (0, i)),
            pl.BlockSpec((16, tm), lambda b, i: (0, i)),
            pl.BlockSpec((tm, LANE), lambda b, i: (i, 0)),
            pl.BlockSpec((tm, LANE), lambda b, i: (i, 0)),
            pl.BlockSpec((tm, LANE), lambda b, i: (i, 0)),
            pl.BlockSpec((1, GMLP_WIDTH), const2),
            pl.BlockSpec((1, GMLP_WIDTH), const2),
            pl.BlockSpec((GMLP_GROUPS, GMLP_CHUNK, GMLP_CHUNK), const3),
            pl.BlockSpec((GMLP_GROUPS, GMLP_CHUNK, 1), const3),
        ],
        out_specs=(pl.BlockSpec((1, 1024, tm), lambda b, i: (b, 0, i)),
                   pl.BlockSpec((1, tm, 1024), lambda b, i: (b, i, 0)),
                   pl.BlockSpec((1, 512, tm), lambda b, i: (b, 0, i)),
                   pl.BlockSpec((1, tm, 512), lambda b, i: (b, i, 0))),
        compiler_params=_cparams(2),
        name="inproj_odd",
    )(h, modv, modv, w["win"], w["gq"], w["gkv"], w["wuq"], w["wkn"], w["wkr"], w["wv"],
      cosf, sinf, cost, sina, sinb, w["lng"], w["lnb"], w["ws"], w["gb"])


def _mla_attn_kernel(q_ref, k_ref, vt_ref, o_ref):
    chunks = _latent_chunks(TK_MLA)
    rhs = [q_ref[0, :, t * TQ_MLA:(t + 1) * TQ_MLA] for t in range(TILES_MLA)]
    results, overflow = _flash_fixed_ref(rhs, k_ref, vt_ref, chunks)
    for t, (acc, l) in enumerate(results):
        o_ref[0, :, t * TQ_MLA:(t + 1) * TQ_MLA] = (acc / l).astype(BF16)

    @pl.when(overflow)
    def _recompute():
        def tile(t, carry):
            cols = pl.ds(pl.multiple_of(t * TQ_MLA, TQ_MLA), TQ_MLA)
            acc, l = _flash(q_ref[0, :, cols], k_ref, vt_ref, chunks)
            o_ref[0, :, cols] = (acc / l).astype(BF16)
            return carry
        lax.fori_loop(0, TILES_MLA, tile, 0)


def _mla_attention(q_t, k, v_t):
    bq = TQ_MLA * TILES_MLA
    return pl.pallas_call(
        _mla_attn_kernel,
        out_shape=jax.ShapeDtypeStruct((B, 512, S), BF16),
        grid=(B, MLA_HEADS, S // bq),
        in_specs=[
            pl.BlockSpec((1, 128, bq), lambda b, h, i: (b, h, i)),
            pl.BlockSpec((1, N, 128), lambda b, h, i: (b, 0, h)),
            pl.BlockSpec((1, MLA_V, N), lambda b, h, i: (b, h, 0)),
        ],
        out_specs=pl.BlockSpec((1, MLA_V, bq), lambda b, h, i: (b, h, i)),
        compiler_params=_cparams(3),
        name="mla_attention",
    )(q_t, k, v_t)


def _rope_tables(n_freq):
    t = np.arange(S)
    inv = np.power(np.float32(ROPE_BASE), -np.arange(0, 2 * n_freq, 2, dtype=np.float32) / np.float32(2 * n_freq))
    ang_r = (t // GRID_W).astype(np.float32)[:, None] * inv[None, :]
    ang_c = (t % GRID_W).astype(np.float32)[:, None] * inv[None, :]
    cos = np.concatenate([np.cos(ang_r), np.cos(ang_c)], axis=1)
    sin = np.concatenate([np.sin(ang_r), np.sin(ang_c)], axis=1)
    cos = np.concatenate([cos, np.ones((L, 2 * n_freq), np.float32)], axis=0).astype(np.float32)
    sin = np.concatenate([sin, np.zeros((L, 2 * n_freq), np.float32)], axis=0).astype(np.float32)
    lane = np.arange(LANE)
    period = 4 * n_freq
    src = ((lane % period) // (2 * n_freq)) * n_freq + lane % n_freq
    is_x1 = (lane % (2 * n_freq)) < n_freq
    cos_t = cos[:, src]
    sin_t = sin[:, src]
    sin_a = np.where(is_x1[None, :], -sin_t, 0.0).astype(np.float32)
    sin_b = np.where(is_x1[None, :], 0.0, sin_t).astype(np.float32)
    return np.ascontiguousarray(cos.T), np.ascontiguousarray(sin.T), cos_t, sin_a, sin_b


def _even_weights(ev_w_in):
    aq, ak, av, bq, bk, bv = jnp.split(ev_w_in, [512, 1024, 1536, 2048, 2560], axis=1)
    scale = HEAD_DIM ** -0.5
    wtok = jnp.concatenate([ak, bk], axis=1).astype(BF16)
    wfeat = jnp.concatenate([aq * scale, av, bq * scale, bv], axis=1).T.astype(BF16)
    return wtok, wfeat


def _odd_weights(od_w_in, gq, w_uq, gkv, w_ukv, ln_g, ln_b, ws, gb):
    cq, ckv, kr, gu, gv = jnp.split(od_w_in, [384, 640, 672, 1184], axis=1)
    win = jnp.concatenate([cq, ckv, kr, jnp.zeros((D, LANE - MLA_ROPE), F32), gu, gv], axis=1)
    dq = MLA_NOPE + MLA_ROPE
    wuq = w_uq.reshape(MLA_Q_RANK, MLA_HEADS, dq)
    wuq = jnp.pad(wuq, ((0, 0), (0, 0), (0, LANE - dq))).reshape(MLA_Q_RANK, MLA_HEADS * LANE)
    wukv = w_ukv.reshape(MLA_KV_RANK, MLA_HEADS, MLA_NOPE + MLA_V)
    wkn = jnp.pad(wukv[:, :, :MLA_NOPE], ((0, 0), (0, 0), (0, LANE - MLA_NOPE)))
    wkn = wkn.reshape(MLA_KV_RANK, MLA_HEADS * LANE)
    wv = wukv[:, :, MLA_NOPE:].reshape(MLA_KV_RANK, MLA_HEADS * MLA_V)
    place = np.zeros((LANE, MLA_HEADS * LANE), np.float32)
    for hd in range(MLA_HEADS):
        place[np.arange(MLA_ROPE), hd * LANE + MLA_NOPE + np.arange(MLA_ROPE)] = 1.0
    return {
        "win": win.astype(BF16), "gq": gq.reshape(1, -1), "gkv": gkv.reshape(1, -1),
        "wuq": wuq.T.astype(BF16), "wkn": wkn.astype(BF16), "wkr": jnp.asarray(place, BF16),
        "wv": wv.T.astype(BF16), "lng": ln_g.reshape(1, -1), "lnb": ln_b.reshape(1, -1),
        "ws": ws.astype(BF16), "gb": gb.reshape(GMLP_GROUPS, GMLP_CHUNK, 1),
    }


def kernel(x, c, ctx, c_ctx, mod_w, mod_b, ln_mix_g, ln_mix_b, ln_ffn_g, ln_ffn_b, ffn_w_in, ffn_w_out,
           ev_w_in, ev_w_out, diff_lambda, diff_subln_g, na_rpb, od_w_in, od_w_out, mla_q_norm_g,
           mla_w_uq, mla_kv_norm_g, mla_w_ukv, gmlp_ln_g, gmlp_ln_b, gmlp_ws, gmlp_b):
    cond = jnp.concatenate([c, c_ctx[None], jnp.zeros((8 - B - 1, D), F32)], axis=0)
    mod = _modulation(cond, mod_w, mod_b).reshape(DEPTH, 8, N_MOD, D)

    wtok, wfeat = _even_weights(ev_w_in[0])
    cosf, sinf, cost, sina, sinb = _rope_tables(16)
    tok, feat = _inproj_even(x, ctx, mod[0], wtok, wfeat,
                             (cosf * np.float32(LOG2E), sinf * np.float32(LOG2E), cost, sina, sinb))
    h = _outproj_even(_diff_attention(tok, feat, diff_lambda[0], diff_subln_g[0], False),
                      _diff_attention(tok, feat, diff_lambda[0], diff_subln_g[0], True),
                      _na_attention(tok, feat, _na_bias_table(na_rpb[0])),
                      _na_attention_ctx(tok, feat),
                      ev_w_out[0].astype(BF16), x, ctx, mod[0], ln_mix_g[0], ln_mix_b[0])
    ffn_w_in, ffn_w_out = ffn_w_in.astype(BF16), ffn_w_out.astype(BF16)
    h = _ffn(h, mod[0], ffn_w_in, ffn_w_out, 0, ln_ffn_g[0], ln_ffn_b[0], N, TM_ALL)

    w1 = _odd_weights(od_w_in[0], mla_q_norm_g[0], mla_w_uq[0], mla_kv_norm_g[0], mla_w_ukv[0],
                      gmlp_ln_g[0], gmlp_ln_b[0], gmlp_ws[0], gmlp_b[0])
    q_t, k, v_t, gm = _inproj_odd(h, mod[1], w1, _rope_tables(8))
    mix_a = _mla_attention(q_t, k, v_t)
    h = _outproj_odd(mix_a, gm, od_w_out[0].astype(BF16), h, mod[1], ln_mix_g[1], ln_mix_b[1])
    return _ffn(h, mod[1], ffn_w_in, ffn_w_out, 1, ln_ffn_g[1], ln_ffn_b[1], S, TM_LAT)
```
